```python
import jax, jax.numpy as jnp
from jax import lax
import numpy as np

D_MODEL = 1024
BATCH = 8
SEQ = 2048
DEPTH = 4
DEC_BATCH = 128
DEC_SEQ = 8
PAST_LEN = 8192
PAGE_SIZE = 128

GDN_DK = 128
GDN_DV = 128
GDN_HEADS = D_MODEL // GDN_DV
GDN_KEY_DIM = GDN_HEADS * GDN_DK
GDN_VAL_DIM = GDN_HEADS * GDN_DV
CONV_WIDTH = 4
CONV_DIM = 2 * GDN_KEY_DIM + GDN_VAL_DIM
GDN_CHUNK = 64
SWA_HD = 64
SWA_HEADS = D_MODEL // SWA_HD
SWA_KV_HEADS = SWA_HEADS // 4
SWA_GROUP = SWA_HEADS // SWA_KV_HEADS
SWA_Q_DIM = SWA_HEADS * SWA_HD
SWA_KV_DIM = SWA_KV_HEADS * SWA_HD
WINDOW = 128
D_FF = 4 * D_MODEL
PLE_DIM = 256
EPS = 1e-6
IN_DIM = CONV_DIM + GDN_VAL_DIM + 2 * GDN_HEADS + SWA_Q_DIM + 2 * SWA_KV_DIM + 2 * D_MODEL

kernel_name = 'hybrid_gdn_swa_decoder_step'


def rmsnorm(x, w):
    xf = x.astype(jnp.float32)
    xf = xf * lax.rsqrt(jnp.mean(xf * xf, axis=-1, keepdims=True) + EPS)
    return xf.astype(x.dtype) * w


def l2norm(x):
    xf = x.astype(jnp.float32)
    return (xf * lax.rsqrt(jnp.sum(xf * xf, axis=-1, keepdims=True) + EPS)).astype(x.dtype)


def causal_conv(u, buf, w):
    t = u.shape[1]
    full = jnp.concatenate([buf.astype(u.dtype), u], axis=1)
    out = full[:, 0:t] * w[0]
    for j in range(1, CONV_WIDTH):
        out = out + full[:, j:j + t] * w[j]
    return jax.nn.silu(out), full[:, t:]


def gated_delta_rule(q, k, v, g, beta, s0):
    b, t, h, dk = q.shape
    dv = v.shape[-1]
    c = min(GDN_CHUNK, t)
    n = -(-t // c)
    pad = n * c - t
    f32 = jnp.float32

    def blocks(a):
        a = a.astype(f32)
        a = jnp.pad(a, [(0, 0), (0, pad)] + [(0, 0)] * (a.ndim - 2))
        a = a.reshape((b, n, c) + a.shape[2:])
        return jnp.moveaxis(a, 2, 3)

    q, k, v, g, beta = blocks(q), blocks(k), blocks(v), blocks(g), blocks(beta)
    gc = jnp.cumsum(g, axis=-1)
    incl = jnp.tril(jnp.ones((c, c), dtype=bool))
    strict = jnp.tril(jnp.ones((c, c), dtype=bool), -1)
    decay = jnp.exp(jnp.where(incl, gc[..., :, None] - gc[..., None, :], -jnp.inf))
    kb = k * beta[..., None]
    a_mat = jnp.where(strict, jnp.einsum('bnhid,bnhjd->bnhij', kb, k) * decay, 0.0)
    rhs = jnp.concatenate([v * beta[..., None], kb * jnp.exp(gc)[..., None]], axis=-1)
    sol = lax.linalg.triangular_solve(a_mat, rhs, left_side=True, lower=True, unit_diagonal=True)
    u, w = sol[..., :dv], sol[..., dv:]
    qk = jnp.einsum('bnhid,bnhjd->bnhij', q, k) * decay

    def step(s, xs):
        q_c, k_c, u_c, w_c, qk_c, g_c = xs
        v_new = u_c - jnp.einsum('bhck,bhkv->bhcv', w_c, s)
        o_c = (jnp.einsum('bhck,bhkv->bhcv', q_c * jnp.exp(g_c)[..., None], s)
               + jnp.einsum('bhij,bhjv->bhiv', qk_c, v_new))
        g_last = g_c[..., -1:]
        s = s * jnp.exp(g_last)[..., None] + jnp.einsum(
            'bhck,bhcv->bhkv', k_c * jnp.exp(g_last - g_c)[..., None], v_new)
        return s, o_c

    xs = (jnp.moveaxis(q, 1, 0), jnp.moveaxis(k, 1, 0), jnp.moveaxis(u, 1, 0),
          jnp.moveaxis(w, 1, 0), jnp.moveaxis(qk, 1, 0), jnp.moveaxis(gc, 1, 0))
    s_final, o = lax.scan(step, s0.astype(f32), xs)
    o = jnp.moveaxis(o, 0, 1)
    o = jnp.moveaxis(o, 3, 2).reshape(b, n * c, h, dv)[:, :t]
    return o, s_final


def gdn_branch(qkv, z, b_logit, a_logit, conv_buf, s0, conv_w, a_log, dt_bias, gdn_norm):
    b, t, _ = qkv.shape
    if conv_buf is None:
        conv_buf = jnp.zeros((b, CONV_WIDTH - 1, CONV_DIM), qkv.dtype)
    if s0 is None:
        s0 = jnp.zeros((b, GDN_HEADS, GDN_DK, GDN_DV), jnp.float32)
    qkv_c, new_buf = causal_conv(qkv, conv_buf, conv_w)
    q = qkv_c[..., :GDN_KEY_DIM].reshape(b, t, GDN_HEADS, GDN_DK)
    k = qkv_c[..., GDN_KEY_DIM:2 * GDN_KEY_DIM].reshape(b, t, GDN_HEADS, GDN_DK)
    v = qkv_c[..., 2 * GDN_KEY_DIM:].reshape(b, t, GDN_HEADS, GDN_DV)
    q = l2norm(q) * (GDN_DK ** -0.5)
    k = l2norm(k)
    beta = jax.nn.sigmoid(b_logit.astype(jnp.float32))
    g = -jnp.exp(a_log.astype(jnp.float32)) * jax.nn.softplus(a_logit.astype(jnp.float32) + dt_bias.astype(jnp.float32))
    o, s_new = gated_delta_rule(q, k, v, g, beta, s0)
    o = rmsnorm(o.astype(v.dtype), gdn_norm) * jax.nn.silu(z.reshape(b, t, GDN_HEADS, GDN_DV))
    return o.reshape(b, t, GDN_VAL_DIM), new_buf, s_new


def alibi_slopes():
    hh = jnp.arange(1, SWA_HEADS + 1, dtype=jnp.float32)
    return jnp.exp2(-8.0 * hh / SWA_HEADS).reshape(SWA_KV_HEADS, SWA_GROUP)


def swa_attend(q, k, v, q_pos, k_pos, sinks, slopes):
    s = jnp.einsum('bnqkgd,bnskd->bnkgqs', q, k).astype(jnp.float32) * (SWA_HD ** -0.5)
    dist = q_pos[:, :, None] - k_pos[:, None, :]
    ok = (dist >= 0) & (dist <= WINDOW) & (k_pos[:, None, :] >= 0)
    s = s - slopes[:, :, None, None] * dist.astype(jnp.float32)[:, None, None]
    s = jnp.where(ok[:, None, None], s, -jnp.inf)
    sink = jnp.broadcast_to(sinks.astype(jnp.float32)[:, :, None, None], s.shape[:-1] + (1,))
    p = jax.nn.softmax(jnp.concatenate([s, sink], axis=-1), axis=-1)[..., :-1]
    return jnp.einsum('bnkgqs,bnskd->bnqkgd', p.astype(v.dtype), v)


def swa_branch(sq, sk, sv, q_norm, k_norm, sinks, cache_k, cache_v):
    b, t, _ = sq.shape
    q = rmsnorm(sq.reshape(b, t, SWA_KV_HEADS, SWA_GROUP, SWA_HD), q_norm)
    k = rmsnorm(sk.reshape(b, t, SWA_KV_HEADS, SWA_HD), k_norm)
    v = sv.reshape(b, t, SWA_KV_HEADS, SWA_HD)
    slopes = alibi_slopes()
    sinks = sinks.reshape(SWA_KV_HEADS, SWA_GROUP)
    if cache_k is None:
        nb = t // WINDOW
        qb = q.reshape(b, nb, WINDOW, SWA_KV_HEADS, SWA_GROUP, SWA_HD)
        zpad = jnp.zeros((b, WINDOW, SWA_KV_HEADS, SWA_HD), k.dtype)

        def band(a):
            prev = jnp.concatenate([zpad, a[:, :-WINDOW]], axis=1).reshape(b, nb, WINDOW, SWA_KV_HEADS, SWA_HD)
            return jnp.concatenate([prev, a.reshape(b, nb, WINDOW, SWA_KV_HEADS, SWA_HD)], axis=2)

        q_pos = jnp.arange(t, dtype=jnp.int32).reshape(nb, WINDOW)
        k_pos = (jnp.arange(nb, dtype=jnp.int32)[:, None] - 1) * WINDOW + jnp.arange(2 * WINDOW, dtype=jnp.int32)[None, :]
        o = swa_attend(qb, band(k), band(v), q_pos, k_pos, sinks, slopes)
        k_all, v_all = k, v
    else:
        k_all = jnp.concatenate([cache_k.astype(k.dtype), k], axis=1)
        v_all = jnp.concatenate([cache_v.astype(v.dtype), v], axis=1)
        q_pos = (PAST_LEN + jnp.arange(t, dtype=jnp.int32))[None, :]
        k_pos = (PAST_LEN - WINDOW + jnp.arange(WINDOW + t, dtype=jnp.int32))[None, :]
        o = swa_attend(q[:, None], k_all[:, None], v_all[:, None], q_pos, k_pos, sinks, slopes)
    return o.reshape(b, t, SWA_Q_DIM), k_all[:, -WINDOW:], v_all[:, -WINDOW:]


def trunk_layer(h, p_l, conv_buf, gdn_s0, swa_k_buf, swa_v_buf, norm_mix, w_in, conv_w, a_log, dt_bias,
                gdn_norm, q_norm, k_norm, sinks, w_out, norm_ffn, w_up, w_down, norm_ple, w_ple_gate, w_ple_proj):
    proj = rmsnorm(h, norm_mix) @ w_in
    cuts = [int(c) for c in np.cumsum([CONV_DIM, GDN_VAL_DIM, GDN_HEADS, GDN_HEADS,
                                       SWA_Q_DIM, SWA_KV_DIM, SWA_KV_DIM])]
    qkv, z, b_logit, a_logit, sq, sk, sv, gates = jnp.split(proj, cuts, axis=-1)
    o_a, conv_new, s_new = gdn_branch(qkv, z, b_logit, a_logit, conv_buf, gdn_s0, conv_w, a_log, dt_bias, gdn_norm)
    o_b, k_new, v_new = swa_branch(sq, sk, sv, q_norm, k_norm, sinks, swa_k_buf, swa_v_buf)
    gates = jax.nn.sigmoid(gates)
    mix = gates[..., :D_MODEL] * o_a + gates[..., D_MODEL:] * o_b
    h = h + mix @ w_out
    hid = jax.nn.relu(rmsnorm(h, norm_ffn) @ w_up)
    h = h + (hid * hid) @ w_down
    h = h + jax.nn.sigmoid(rmsnorm(h, norm_ple) @ w_ple_gate) * (p_l @ w_ple_proj)
    return h, conv_new, s_new, k_new, v_new


def setup_inputs(seed: int = 0) -> dict:
    key = jax.random.key(seed)
    ks = jax.random.split(key, 32)
    f32 = jnp.float32
    nrm = lambda k, shape, s: jax.random.normal(k, shape, f32) * s
    dt = jnp.exp(jax.random.uniform(ks[10], (DEPTH, GDN_HEADS), f32, np.log(1e-3), np.log(1e-1)))
    return {
        'x_prompt': nrm(ks[0], (BATCH, SEQ, D_MODEL), 1.0),
        'x_sample': nrm(ks[1], (DEC_BATCH, DEC_SEQ, D_MODEL), 1.0),
        'cache_conv': nrm(ks[2], (DEPTH, DEC_BATCH, CONV_WIDTH - 1, CONV_DIM), 1.0),
        'state_gdn': nrm(ks[3], (DEPTH, DEC_BATCH, GDN_HEADS, GDN_DK, GDN_DV), 0.1),
        'cache_swa_k': nrm(ks[4], (DEPTH, DEC_BATCH, WINDOW, SWA_KV_HEADS, SWA_HD), 1.0),
        'cache_swa_v': nrm(ks[5], (DEPTH, DEC_BATCH, WINDOW, SWA_KV_HEADS, SWA_HD), 1.0),
        'p_prompt': nrm(ks[6], (DEPTH, BATCH, SEQ, PLE_DIM), 1.0),
        'p_sample': nrm(ks[7], (DEPTH, DEC_BATCH, DEC_SEQ, PLE_DIM), 1.0),
        'norm_mix': 1.0 + nrm(ks[8], (DEPTH, D_MODEL), 0.1),
        'w_in': nrm(ks[9], (DEPTH, D_MODEL, IN_DIM), D_MODEL ** -0.5),
        'conv_w': nrm(ks[11], (DEPTH, CONV_WIDTH, CONV_DIM), 0.5),
        'a_log': jnp.log(jax.random.uniform(ks[12], (DEPTH, GDN_HEADS), f32, 1.0, 16.0)),
        'dt_bias': dt + jnp.log(-jnp.expm1(-dt)),
        'gdn_norm': 1.0 + nrm(ks[13], (DEPTH, GDN_DV), 0.1),
        'q_norm': 1.0 + nrm(ks[14], (DEPTH, SWA_HD), 0.1),
        'k_norm': 1.0 + nrm(ks[15], (DEPTH, SWA_HD), 0.1),
        'attn_sinks': nrm(ks[16], (DEPTH, SWA_HEADS), 0.5),
        'w_out': nrm(ks[17], (DEPTH, D_MODEL, D_MODEL), D_MODEL ** -0.5),
        'norm_ffn': 1.0 + nrm(ks[18], (DEPTH, D_MODEL), 0.1),
        'w_up': nrm(ks[19], (DEPTH, D_MODEL, D_FF), D_MODEL ** -0.5),
        'w_down': nrm(ks[20], (DEPTH, D_FF, D_MODEL), D_FF ** -0.5),
        'norm_ple': 1.0 + nrm(ks[21], (DEPTH, D_MODEL), 0.1),
        'w_ple_gate': nrm(ks[22], (DEPTH, D_MODEL, D_MODEL), D_MODEL ** -0.5),
        'w_ple_proj': nrm(ks[23], (DEPTH, PLE_DIM, D_MODEL), PLE_DIM ** -0.5),
    }


def reference(x_prompt, x_sample, cache_conv, state_gdn, cache_swa_k, cache_swa_v, p_prompt, p_sample,
              norm_mix, w_in, conv_w, a_log, dt_bias, gdn_norm, q_norm, k_norm, attn_sinks, w_out,
              norm_ffn, w_up, w_down, norm_ple, w_ple_gate, w_ple_proj):
    hp, hs = x_prompt, x_sample
    conv_p, gdn_p, kp_l, vp_l = [], [], [], []
    conv_s, gdn_s, ks_l, vs_l = [], [], [], []
    for i in range(DEPTH):
        wts = (norm_mix[i], w_in[i], conv_w[i], a_log[i], dt_bias[i], gdn_norm[i], q_norm[i], k_norm[i],
               attn_sinks[i], w_out[i], norm_ffn[i], w_up[i], w_down[i], norm_ple[i], w_ple_gate[i], w_ple_proj[i])
        hp, c1, s1, k1, v1 = trunk_layer(hp, p_prompt[i], None, None, None, None, *wts)
        hs, c2, s2, k2, v2 = trunk_layer(hs, p_sample[i], cache_conv[i], state_gdn[i],
                                         cache_swa_k[i], cache_swa_v[i], *wts)
        conv_p.append(c1); gdn_p.append(s1); kp_l.append(k1); vp_l.append(v1)
        conv_s.append(c2); gdn_s.append(s2); ks_l.append(k2); vs_l.append(v2)
    return (hp, hs,
            jnp.stack(conv_p), jnp.stack(gdn_p), jnp.stack(kp_l), jnp.stack(vp_l),
            jnp.stack(conv_s), jnp.stack(gdn_s), jnp.stack(ks_l), jnp.stack(vs_l))
```

```python
import functools

import numpy as np
import jax
import jax.numpy as jnp
from jax import lax
from jax.experimental import pallas as pl
from jax.experimental.pallas import tpu as pltpu

F32 = jnp.float32
BF16 = jnp.bfloat16
EPS = 1e-6
LANES = 128
SUBLANES = 8
GDN_CHUNK = 128
VMEM_LIMIT = 56 * 1024 * 1024


def _silu(x):
    return x * jax.nn.sigmoid(x)


def _dot(a, b):
    return jnp.dot(a.astype(BF16), b.astype(BF16), preferred_element_type=F32)


def _dot_nt(a, b):
    return lax.dot_general(a.astype(BF16), b.astype(BF16), (((1,), (1,)), ((), ())), preferred_element_type=F32)


def _dot_tn(a, b):
    return lax.dot_general(a.astype(BF16), b.astype(BF16), (((0,), (0,)), ((), ())), preferred_element_type=F32)


def _rms(x):
    return x * lax.rsqrt(jnp.mean(x * x, axis=-1, keepdims=True) + EPS)


def _l2(x):
    return x * lax.rsqrt(jnp.sum(x * x, axis=-1, keepdims=True) + EPS)


def _inproj_kernel(x_ref, nw_ref, w_ref, wba_ref, avec_ref, mask_ref, proj_ref, bgc_ref, xn_ref, *, heads):
    n = pl.program_id(1)

    @pl.when(n == 0)
    def _():
        xn = (_rms(x_ref[...]) * nw_ref[...]).astype(BF16)
        xn_ref[...] = xn
        ba = jnp.dot(xn, wba_ref[...], preferred_element_type=F32)
        beta = jax.nn.sigmoid(ba)
        xs = ba + avec_ref[1:2, :]
        softplus = jnp.maximum(xs, 0.0) + jnp.log1p(jnp.exp(-jnp.abs(xs)))
        g = -jnp.exp(avec_ref[0:1, :]) * softplus
        g1 = g.astype(BF16)
        r1 = g - g1.astype(F32)
        g2 = r1.astype(BF16)
        g3 = (r1 - g2.astype(F32)).astype(BF16)
        m = mask_ref[...]
        rows = g.shape[0]
        lane = lax.broadcasted_iota(jnp.int32, (GDN_CHUNK, LANES), 1)
        for r in range(rows // GDN_CHUNK):
            sl = slice(r * GDN_CHUNK, (r + 1) * GDN_CHUNK)
            gc = (jnp.dot(m, g1[sl], preferred_element_type=F32)
                  + jnp.dot(m, g2[sl], preferred_element_type=F32)
                  + jnp.dot(m, g3[sl], preferred_element_type=F32))
            bgc_ref[sl, :] = jnp.where(lane < heads, beta[sl], gc)

    proj_ref[...] = jnp.dot(xn_ref[...], w_ref[...], preferred_element_type=F32)


def _inproj(h, nw, w, wba, avec, masks, *, n_prompt_rows, heads, tm, tn):
    t_all, d = h.shape
    n_out = w.shape[1]
    npt = n_prompt_rows // tm
    return pl.pallas_call(
        functools.partial(_inproj_kernel, heads=heads),
        grid=(t_all // tm, n_out // tn),
        in_specs=[
            pl.BlockSpec((tm, d), lambda m, n: (m, 0)),
            pl.BlockSpec((1, d), lambda m, n: (0, 0)),
            pl.BlockSpec((d, tn), lambda m, n: (0, n)),
            pl.BlockSpec((d, LANES), lambda m, n: (0, 0)),
            pl.BlockSpec((2, LANES), lambda m, n: (0, 0)),
            pl.BlockSpec((None, GDN_CHUNK, GDN_CHUNK), lambda m, n: (jnp.where(m >= npt, 1, 0), 0, 0)),
        ],
        out_specs=[
            pl.BlockSpec((tm, tn), lambda m, n: (m, n)),
            pl.BlockSpec((tm, LANES), lambda m, n: (m, 0)),
        ],
        out_shape=[
            jax.ShapeDtypeStruct((t_all, n_out), F32),
            jax.ShapeDtypeStruct((t_all, LANES), F32),
        ],
        scratch_shapes=[pltpu.VMEM((tm, d), BF16)],
        compiler_params=pltpu.CompilerParams(
            dimension_semantics=("parallel", "arbitrary"), vmem_limit_bytes=VMEM_LIMIT),
        name="inproj",
    )(h, nw, w, wba, avec, masks)


def _gdn_chunk_local(q, k, v, beta, gc, mincl, seq_len):
    c = q.shape[0]
    dv = v.shape[1]
    gcb = jnp.broadcast_to(gc, (c, c))
    dec = jnp.exp(jnp.where(mincl > 0, gcb - gcb.T, -jnp.inf))
    eg = jnp.exp(gc)
    kb = k * beta
    kk = _dot_nt(jnp.concatenate([kb, q], axis=0), k)
    row = lax.broadcasted_iota(jnp.int32, (c, c), 0)
    col = lax.broadcasted_iota(jnp.int32, (c, c), 1)
    a = jnp.where(row == col, 0.0, kk[:c] * dec)
    qk = kk[c:] * dec
    same = (row >> 1) == (col >> 1)
    t = jnp.where(row == col, 1.0, 0.0) - jnp.where(same, a, 0.0)
    a16 = a.astype(BF16)
    zero = jnp.zeros_like(a16)
    for lvl in range(1, seq_len.bit_length() - 1):
        wider = (row >> (lvl + 1)) == (col >> (lvl + 1))
        a_off = jnp.where(wider & jnp.logical_not(same), a16, zero)
        t16 = t.astype(BF16)
        t = t - jnp.dot(t16, jnp.dot(a_off, t16, preferred_element_type=F32).astype(BF16),
                        preferred_element_type=F32)
        same = wider
    sol = _dot(t, jnp.concatenate([v * beta, kb * eg], axis=1))
    return sol[:, :dv], sol[:, dv:], qk, eg


def _conv_taps(u, ru_fn, cw):
    width = cw.shape[0]
    out = u * cw[width - 1:width, :]
    for j in range(1, width):
        out = out + ru_fn(j) * cw[width - 1 - j:width - j, :]
    return _silu(out)


def _gdn_finish(o, z, ga, gn):
    return (_rms(o) * gn * _silu(z) * jax.nn.sigmoid(ga)).astype(BF16)


def _gdn_prompt_kernel(q_ref, k_ref, v_ref, z_ref, ga_ref, bgc_ref, cwq_ref, cwk_ref, cwv_ref, gn_ref, mincl_ref,
                       o_ref, sfin_ref, s_ref, tail_ref, *, heads):
    h = pl.program_id(1)
    c = pl.program_id(2)
    rows, dk = q_ref.shape

    @pl.when(c == 0)
    def _():
        s_ref[...] = jnp.zeros_like(s_ref)
        tail_ref[...] = jnp.zeros_like(tail_ref)

    row8 = lax.broadcasted_iota(jnp.int32, (SUBLANES, dk), 0)

    def conv(idx, u_ref, cw_ref):
        u = u_ref[...]
        tail = tail_ref[idx]

        def delayed(j):
            ru = pltpu.roll(u, j, 0)
            head = jnp.where(row8 < j, pltpu.roll(tail, j, 0), ru[:SUBLANES])
            return jnp.concatenate([head, ru[SUBLANES:]], axis=0)

        out = _conv_taps(u, delayed, cw_ref[...])
        tail_ref[idx] = u[rows - SUBLANES:]
        return out

    q = _l2(conv(0, q_ref, cwq_ref)) * (dk ** -0.5)
    k = _l2(conv(1, k_ref, cwk_ref))
    v = conv(2, v_ref, cwv_ref)

    lane = lax.broadcasted_iota(jnp.int32, (rows, LANES), 1)
    bgc = bgc_ref[...]
    beta = jnp.sum(jnp.where(lane == h, bgc, 0.0), axis=-1, keepdims=True)
    gc = jnp.sum(jnp.where(lane == h + heads, bgc, 0.0), axis=-1, keepdims=True)
    mincl = mincl_ref[...]

    s = s_ref[...]
    outs = []
    for ci in range(rows // GDN_CHUNK):
        sl = slice(ci * GDN_CHUNK, (ci + 1) * GDN_CHUNK)
        qc, kc, gcc = q[sl], k[sl], gc[sl]
        u, w, qk, eg = _gdn_chunk_local(qc, kc, v[sl], beta[sl], gcc, mincl, GDN_CHUNK)
        wq = _dot(jnp.concatenate([w, qc * eg], axis=0), s)
        v_new = u - wq[:GDN_CHUNK]
        outs.append(wq[GDN_CHUNK:] + _dot(qk, v_new))
        g_last = gcc[GDN_CHUNK - 1:GDN_CHUNK, :]
        s = s * jnp.exp(g_last) + _dot_tn(kc * jnp.exp(g_last - gcc), v_new)
    s_ref[...] = s
    sfin_ref[...] = s
    o_ref[...] = _gdn_finish(jnp.concatenate(outs, axis=0), z_ref[...], ga_ref[...], gn_ref[...])


def _gdn_prompt(proj, bgc, conv_w, gn, mincl, *, batch, seq, heads, dk, col, rows):
    ncb = seq // rows
    row_map = lambda off: (lambda b, h, c: (b * ncb + c, off + h))
    cw_map = lambda off: (lambda b, h, c: (0, off + h))
    blk = lambda off: pl.BlockSpec((rows, dk), row_map(off))
    return pl.pallas_call(
        functools.partial(_gdn_prompt_kernel, heads=heads),
        grid=(batch, heads, ncb),
        in_specs=[
            blk(col["q"]), blk(col["k"]), blk(col["v"]), blk(col["z"]), blk(col["ga"]),
            pl.BlockSpec((rows, LANES), lambda b, h, c: (b * ncb + c, 0)),
            pl.BlockSpec((conv_w.shape[0], dk), cw_map(col["q"])),
            pl.BlockSpec((conv_w.shape[0], dk), cw_map(col["k"])),
            pl.BlockSpec((conv_w.shape[0], dk), cw_map(col["v"])),
            pl.BlockSpec((1, dk), lambda b, h, c: (0, 0)),
            pl.BlockSpec((GDN_CHUNK, GDN_CHUNK), lambda b, h, c: (0, 0)),
        ],
        out_specs=[
            pl.BlockSpec((rows, dk), lambda b, h, c: (b * ncb + c, h)),
            pl.BlockSpec((None, None, dk, dk), lambda b, h, c: (b, h, 0, 0)),
        ],
        out_shape=[
            jax.ShapeDtypeStruct((batch * seq, heads * dk), BF16),
            jax.ShapeDtypeStruct((batch, heads, dk, dk), F32),
        ],
        scratch_shapes=[pltpu.VMEM((dk, dk), F32), pltpu.VMEM((3, SUBLANES, dk), F32)],
        compiler_params=pltpu.CompilerParams(
            dimension_semantics=("parallel", "parallel", "arbitrary"), vmem_limit_bytes=VMEM_LIMIT),
        name="gdn_prompt",
    )(proj, proj, proj, proj, proj, bgc, conv_w, conv_w, conv_w, gn, mincl)


def _gdn_sample_kernel(q_ref, k_ref, v_ref, z_ref, ga_ref, bgc_ref, hq_ref, hk_ref, hv_ref,
                       cwq_ref, cwk_ref, cwv_ref, gn_ref, mincl_ref, s0_ref,
                       o_ref, sout_ref, *, heads, seq):
    h = pl.program_id(1)
    rows, dk = q_ref.shape
    nseq = rows // seq
    t_idx = lax.broadcasted_iota(jnp.int32, (nseq, seq, dk), 1)

    def conv(u_ref, hist_ref, cw_ref):
        u = u_ref[...]
        u3 = u.reshape(nseq, seq, dk)
        hist3 = hist_ref[...].reshape(nseq, seq, dk)

        def delayed(j):
            xj = jnp.where(t_idx < j, pltpu.roll(hist3, j, 1), pltpu.roll(u3, j, 1))
            return xj.reshape(rows, dk)

        return _conv_taps(u, delayed, cw_ref[...])

    q = _l2(conv(q_ref, hq_ref, cwq_ref)) * (dk ** -0.5)
    k = _l2(conv(k_ref, hk_ref, cwk_ref))
    v = conv(v_ref, hv_ref, cwv_ref)

    lane = lax.broadcasted_iota(jnp.int32, (rows, LANES), 1)
    bgc = bgc_ref[...]
    beta = jnp.sum(jnp.where(lane == h, bgc, 0.0), axis=-1, keepdims=True)
    gc = jnp.sum(jnp.where(lane == h + heads, bgc, 0.0), axis=-1, keepdims=True)

    u, w, qk, eg = _gdn_chunk_local(q, k, v, beta, gc, mincl_ref[...], seq)
    qe = q * eg
    pad = jnp.zeros((seq, dk), F32)
    v_news, o_states = [], []
    for si in range(nseq):
        sl = slice(si * seq, (si + 1) * seq)
        s = s0_ref[si]
        wq = _dot(jnp.concatenate([w[sl], qe[sl]], axis=0), s)
        v_new = u[sl] - wq[:seq]
        g_last = gc[(si + 1) * seq - 1:(si + 1) * seq, :]
        kd = k[sl] * jnp.exp(g_last - gc[sl])
        sout_ref[si] = s * jnp.exp(g_last) + _dot_tn(jnp.concatenate([kd, pad], axis=0),
                                                      jnp.concatenate([v_new, pad], axis=0))
        v_news.append(v_new)
        o_states.append(wq[seq:])
    o = jnp.concatenate(o_states, axis=0) + _dot(qk, jnp.concatenate(v_news, axis=0))
    o_ref[...] = _gdn_finish(o, z_ref[...], ga_ref[...], gn_ref[...])


def _gdn_sample(proj, bgc, hist, s0, conv_w, gn, mincl, *, row0, batch, seq, heads, dk, col):
    rows = GDN_CHUNK
    nseq = rows // seq
    r0 = row0 // rows
    row_map = lambda off: (lambda b, h: (r0 + b, off + h))
    hist_map = lambda off: (lambda b, h: (b, off + h))
    cw_map = lambda off: (lambda b, h: (0, off + h))
    blk = lambda off: pl.BlockSpec((rows, dk), row_map(off))
    hblk = lambda off: pl.BlockSpec((rows, dk), hist_map(off))
    cwb = lambda off: pl.BlockSpec((conv_w.shape[0], dk), cw_map(off))
    return pl.pallas_call(
        functools.partial(_gdn_sample_kernel, heads=heads, seq=seq),
        grid=(batch * seq // rows, heads),
        in_specs=[
            blk(col["q"]), blk(col["k"]), blk(col["v"]), blk(col["z"]), blk(col["ga"]),
            pl.BlockSpec((rows, LANES), lambda b, h: (r0 + b, 0)),
            hblk(col["q"]), hblk(col["k"]), hblk(col["v"]),
            cwb(col["q"]), cwb(col["k"]), cwb(col["v"]),
            pl.BlockSpec((1, dk), lambda b, h: (0, 0)),
            pl.BlockSpec((GDN_CHUNK, GDN_CHUNK), lambda b, h: (0, 0)),
            pl.BlockSpec((nseq, None, dk, dk), lambda b, h: (b, h, 0, 0)),
        ],
        out_specs=[
            pl.BlockSpec((rows, dk), lambda b, h: (b, h)),
            pl.BlockSpec((nseq, None, dk, dk), lambda b, h: (b, h, 0, 0)),
        ],
        out_shape=[
            jax.ShapeDtypeStruct((batch * seq, heads * dk), BF16),
            jax.ShapeDtypeStruct((batch, heads, dk, dk), F32),
        ],
        compiler_params=pltpu.CompilerParams(
            dimension_semantics=("parallel", "parallel"), vmem_limit_bytes=VMEM_LIMIT),
        name="gdn_sample",
    )(proj, proj, proj, proj, proj, bgc, hist, hist, hist, conv_w, conv_w, conv_w, gn, mincl, s0)


def _chunk_rms(x, bd, w):
    ms = _dot(x * x, bd)
    return x * lax.rsqrt(ms + EPS) * w


def _swa_prompt_kernel(sink_ref, q_ref, kp_ref, kc_ref, vp_ref, vc_ref, gb_ref, qw_ref, kw_ref, bd_ref, bias_ref,
                       o_ref, kout_ref, qs_ref, s_ref, p_ref, *, kvh, hd):
    blk = q_ref.shape[0]
    slab = kvh * hd
    groups = q_ref.shape[1] // slab
    bd = bd_ref[...]
    kn = _chunk_rms(kc_ref[...], bd, kw_ref[...])
    kout_ref[...] = kn
    kband = jnp.concatenate([_chunk_rms(kp_ref[...], bd, kw_ref[...]), kn], axis=0).astype(BF16)
    vband = jnp.concatenate([vp_ref[...], vc_ref[...]], axis=0).astype(BF16)
    lane_head = lax.broadcasted_iota(jnp.int32, (blk, slab), 1) // hd

    for g in range(groups):
        qn = _chunk_rms(q_ref[:, g * slab:(g + 1) * slab], bd, qw_ref[...]) * (hd ** -0.5)
        for j in range(kvh):
            p = g * kvh + j
            qs_ref[p * blk:(p + 1) * blk, :] = jnp.where(lane_head == j, qn, 0.0).astype(BF16)
    s_ref[...] = lax.dot_general(qs_ref[...], kband, (((1,), (1,)), ((), ())), preferred_element_type=F32)
    for p in range(groups * kvh):
        sl = slice(p * blk, (p + 1) * blk)
        s = s_ref[sl, :] + bias_ref[p]
        sink = sink_ref[p]
        m = jnp.maximum(jnp.max(s, axis=-1, keepdims=True), sink)
        e = jnp.exp(s - m)
        den = jnp.sum(e, axis=-1, keepdims=True) + jnp.exp(sink - m)
        p_ref[sl, :] = (e / den).astype(BF16)
    s_ref[...] = jnp.dot(p_ref[...], vband, preferred_element_type=F32)
    for g in range(groups):
        acc = jnp.zeros((blk, slab), F32)
        for j in range(kvh):
            p = g * kvh + j
            acc = acc + jnp.where(lane_head == j, s_ref[p * blk:(p + 1) * blk, :], 0.0)
        gate = jax.nn.sigmoid(gb_ref[:, g * slab:(g + 1) * slab])
        o_ref[:, g * slab:(g + 1) * slab] = (acc * gate).astype(BF16)


def _swa_prompt(proj, sinks, qw, kw, bd, bias, *, batch, seq, window, kvh, hd, nq, col):
    nb = seq // window
    slab = kvh * hd
    nheads = nq // hd
    cur = lambda off: (lambda b, c: (b * nb + c, off))
    prev = lambda off: (lambda b, c: (b * nb + jnp.maximum(c - 1, 0), off))
    const2 = lambda b, c: (0, 0)
    return pl.pallas_call(
        functools.partial(_swa_prompt_kernel, kvh=kvh, hd=hd),
        grid=(batch, nb),
        in_specs=[
            pl.BlockSpec(memory_space=pltpu.SMEM),
            pl.BlockSpec((window, nq), cur(col["sq"])),
            pl.BlockSpec((window, slab), prev(col["sk"])),
            pl.BlockSpec((window, slab), cur(col["sk"])),
            pl.BlockSpec((window, slab), prev(col["sv"])),
            pl.BlockSpec((window, slab), cur(col["sv"])),
            pl.BlockSpec((window, nq), cur(col["gb"])),
            pl.BlockSpec((1, slab), const2),
            pl.BlockSpec((1, slab), const2),
            pl.BlockSpec((slab, slab), const2),
            pl.BlockSpec((None, nheads, window, 2 * window), lambda b, c: (jnp.minimum(c, 1), 0, 0, 0)),
        ],
        out_specs=[
            pl.BlockSpec((window, nq), lambda b, c: (b * nb + c, 0)),
            pl.BlockSpec((None, window, slab), lambda b, c: (b, 0, 0)),
        ],
        out_shape=[
            jax.ShapeDtypeStruct((batch * seq, nq), BF16),
            jax.ShapeDtypeStruct((batch, window, slab), F32),
        ],
        scratch_shapes=[
            pltpu.VMEM((nheads * window, slab), BF16),
            pltpu.VMEM((nheads * window, 2 * window), F32),
            pltpu.VMEM((nheads * window, 2 * window), BF16),
        ],
        compiler_params=pltpu.CompilerParams(
            dimension_semantics=("parallel", "arbitrary"), vmem_limit_bytes=VMEM_LIMIT),
        name="swa_prompt",
    )(sinks, proj, proj, proj, proj, proj, proj, qw, kw, bd, bias)


def _swa_sample_kernel(q_ref, k_ref, v_ref, gb_ref, ck_ref, cv_ref, qw_ref, kw_ref, bd_ref, bias_ref, sink_ref,
                       o_ref, kout_ref, vout_ref, *, kvh, hd, seq):
    rows = q_ref.shape[0]
    nseq = rows // seq
    slab = kvh * hd
    groups = q_ref.shape[1] // slab
    window = ck_ref.shape[1]
    nkeys = bias_ref.shape[1]
    bd = bd_ref[...]
    kn = _chunk_rms(k_ref[...], bd, kw_ref[...]).reshape(nseq, seq, slab)
    vn = v_ref[...].reshape(nseq, seq, slab)
    ck = ck_ref[...]
    cv = cv_ref[...]
    kout_ref[...] = jnp.concatenate([ck[:, seq:, :], kn], axis=1)
    vout_ref[...] = jnp.concatenate([cv[:, seq:, :], vn], axis=1)
    zpad = jnp.zeros((nseq, nkeys - window - seq, slab), F32)
    k_all = jnp.concatenate([ck, kn, zpad], axis=1).astype(BF16)
    v_all = jnp.concatenate([cv, vn, zpad], axis=1).astype(BF16)

    lane_head = lax.broadcasted_iota(jnp.int32, (rows, slab), 1) // hd
    pieces = []
    for g in range(groups):
        qn = _chunk_rms(q_ref[:, g * slab:(g + 1) * slab], bd, qw_ref[...]) * (hd ** -0.5)
        for j in range(kvh):
            pieces.append(jnp.where(lane_head == j, qn, 0.0).reshape(nseq, seq, slab))
    q_all = jnp.concatenate(pieces, axis=1).astype(BF16)
    s = jnp.einsum("bqd,bkd->bqk", q_all, k_all, preferred_element_type=F32) + bias_ref[...][None]
    sink = sink_ref[...][None]
    m = jnp.maximum(jnp.max(s, axis=-1, keepdims=True), sink)
    e = jnp.exp(s - m)
    den = jnp.sum(e, axis=-1, keepdims=True) + jnp.exp(sink - m)
    pv = jnp.einsum("bqk,bkd->bqd", (e / den).astype(BF16), v_all, preferred_element_type=F32)
    lane_head3 = lax.broadcasted_iota(jnp.int32, (nseq, seq, slab), 2) // hd
    for g in range(groups):
        acc = jnp.zeros((nseq, seq, slab), F32)
        for j in range(kvh):
            p = g * kvh + j
            acc = acc + jnp.where(lane_head3 == j, pv[:, p * seq:(p + 1) * seq, :], 0.0)
        gate = jax.nn.sigmoid(gb_ref[:, g * slab:(g + 1) * slab])
        o_ref[:, g * slab:(g + 1) * slab] = (acc.reshape(rows, slab) * gate).astype(BF16)


def _swa_sample(proj, cache_k, cache_v, qw, kw, bd, bias, sink_col, *, row0, batch, seq, kvh, hd, nq, col, nseq):
    rows = nseq * seq
    slab = kvh * hd
    window = cache_k.shape[1]
    r0 = row0 // rows
    cur = lambda off: (lambda b: (r0 + b, off))
    const2 = lambda b: (0, 0)
    return pl.pallas_call(
        functools.partial(_swa_sample_kernel, kvh=kvh, hd=hd, seq=seq),
        grid=(batch // nseq,),
        in_specs=[
            pl.BlockSpec((rows, nq), cur(col["sq"])),
            pl.BlockSpec((rows, slab), cur(col["sk"])),
            pl.BlockSpec((rows, slab), cur(col["sv"])),
            pl.BlockSpec((rows, nq), cur(col["gb"])),
            pl.BlockSpec((nseq, window, slab), lambda b: (b, 0, 0)),
            pl.BlockSpec((nseq, window, slab), lambda b: (b, 0, 0)),
            pl.BlockSpec((1, slab), const2),
            pl.BlockSpec((1, slab), const2),
            pl.BlockSpec((slab, slab), const2),
            pl.BlockSpec(bias.shape, const2),
            pl.BlockSpec(sink_col.shape, const2),
        ],
        out_specs=[
            pl.BlockSpec((rows, nq), lambda b: (b, 0)),
            pl.BlockSpec((nseq, window, slab), lambda b: (b, 0, 0)),
            pl.BlockSpec((nseq, window, slab), lambda b: (b, 0, 0)),
        ],
        out_shape=[
            jax.ShapeDtypeStruct((batch * seq, nq), BF16),
            jax.ShapeDtypeStruct((batch, window, slab), F32),
            jax.ShapeDtypeStruct((batch, window, slab), F32),
        ],
        compiler_params=pltpu.CompilerParams(
            dimension_semantics=("parallel",), vmem_limit_bytes=VMEM_LIMIT),
        name="swa_sample",
    )(proj, proj, proj, proj, cache_k, cache_v, qw, kw, bd, bias, sink_col)


def _dense_kernel(h_ref, oap_ref, oas_ref, obp_ref, obs_ref, p_ref, wo_ref, wop_ref, nf_ref, wu_ref, wd_ref,
                  np_ref, wg_ref, wp_ref, out_ref, *, npt, ff_chunk):
    is_prompt = pl.program_id(0) < npt
    oa = jnp.where(is_prompt, oap_ref[...], oas_ref[...])
    ob = jnp.where(is_prompt, obp_ref[...], obs_ref[...])
    h = (h_ref[...] + jnp.dot(oa, wo_ref[...], preferred_element_type=F32)
         + jnp.dot(ob, wop_ref[...], preferred_element_type=F32))
    xn = (_rms(h) * nf_ref[...]).astype(BF16)
    acc = jnp.zeros_like(h)
    for c in range(wu_ref.shape[1] // ff_chunk):
        sl = slice(c * ff_chunk, (c + 1) * ff_chunk)
        hid = jnp.maximum(jnp.dot(xn, wu_ref[:, sl], preferred_element_type=F32), 0.0)
        acc = acc + jnp.dot((hid * hid).astype(BF16), wd_ref[sl, :], preferred_element_type=F32)
    h = h + acc
    xn = (_rms(h) * np_ref[...]).astype(BF16)
    gate = jax.nn.sigmoid(jnp.dot(xn, wg_ref[...], preferred_element_type=F32))
    pe = jnp.dot(p_ref[...].astype(BF16), wp_ref[...], preferred_element_type=F32)
    out_ref[...] = h + gate * pe


def _dense(h, oa_p, oa_s, ob_p, ob_s, p, wo, wop, nf, wu, wd, npl, wg, wp, *, n_prompt_rows, tm):
    t_all, d = h.shape
    npt = n_prompt_rows // tm
    rows = lambda m: (m, 0)
    rows_p = lambda m: (jnp.minimum(m, npt - 1), 0)
    rows_s = lambda m: (jnp.maximum(m - npt, 0), 0)
    resident = lambda a: pl.BlockSpec(a.shape, lambda m: (0, 0), pipeline_mode=pl.Buffered(1))
    return pl.pallas_call(
        functools.partial(_dense_kernel, npt=npt, ff_chunk=1024),
        grid=(t_all // tm,),
        in_specs=[
            pl.BlockSpec((tm, d), rows),
            pl.BlockSpec((tm, d), rows_p), pl.BlockSpec((tm, d), rows_s),
            pl.BlockSpec((tm, d), rows_p), pl.BlockSpec((tm, d), rows_s),
            pl.BlockSpec((tm, p.shape[1]), rows),
            resident(wo), resident(wop), resident(nf), resident(wu), resident(wd),
            resident(npl), resident(wg), resident(wp),
        ],
        out_specs=pl.BlockSpec((tm, d), rows),
        out_shape=jax.ShapeDtypeStruct((t_all, d), F32),
        compiler_params=pltpu.CompilerParams(
            dimension_semantics=("parallel",), vmem_limit_bytes=VMEM_LIMIT),
        name="dense",
    )(h, oa_p, oa_s, ob_p, ob_s, p, wo, wop, nf, wu, wd, npl, wg, wp)


def _seq_mask(seq_len):
    i = np.arange(GDN_CHUNK)
    m = (i[:, None] // seq_len == i[None, :] // seq_len) & (i[None, :] <= i[:, None])
    return m.astype(np.float32)


def kernel(x_prompt, x_sample, cache_conv, state_gdn, cache_swa_k, cache_swa_v, p_prompt, p_sample, norm_mix, w_in, conv_w, a_log, dt_bias, gdn_norm, q_norm, k_norm, attn_sinks, w_out, norm_ffn, w_up, w_down, norm_ple, w_ple_gate, w_ple_proj):
    batch, seq, d = x_prompt.shape
    dbatch, dseq, _ = x_sample.shape
    depth = w_in.shape[0]
    heads, dk, dv = state_gdn.shape[2:]
    window, kvh, hd = cache_swa_k.shape[2:]
    nq = d
    nheads = nq // hd
    groups = nheads // kvh
    slab = kvh * hd
    key_dim = heads * dk
    val_dim = heads * dv
    conv_dim = 2 * key_dim + val_dim
    tp, ts = batch * seq, dbatch * dseq
    assert dk == LANES and dv == LANES and val_dim == d and slab % LANES == 0
    assert seq % window == 0 and window == GDN_CHUNK and GDN_CHUNK % dseq == 0 and dseq >= conv_w.shape[1] - 1
    assert dseq == SUBLANES and ts % GDN_CHUNK == 0

    o_z = conv_dim
    o_b = o_z + val_dim
    o_a = o_b + heads
    o_sq = o_a + heads
    o_sk = o_sq + nq
    o_sv = o_sk + slab
    o_g = o_sv + slab
    perm = np.concatenate([np.arange(hd) + (j * groups + g) * hd for g in range(groups) for j in range(kvh)])
    col = {"q": 0, "k": heads, "v": 2 * heads, "z": 3 * heads, "ga": (conv_dim + val_dim + nq) // LANES}
    col_wide = {"sq": (conv_dim + val_dim) // nq, "gb": (conv_dim + val_dim + nq + d) // nq,
                "sk": (conv_dim + val_dim + nq + 2 * d) // slab, "sv": (conv_dim + val_dim + nq + 2 * d + slab) // slab}
    n_proj = conv_dim + val_dim + nq + 2 * d + 2 * slab

    h = jnp.concatenate([x_prompt.reshape(tp, d), x_sample.reshape(ts, d)], axis=0)
    p_all = jnp.concatenate([p_prompt.reshape(depth, tp, -1), p_sample.reshape(depth, ts, -1)], axis=1)

    masks = jnp.asarray(np.stack([_seq_mask(GDN_CHUNK), _seq_mask(dseq)]), BF16)
    mincl_p = jnp.asarray(_seq_mask(GDN_CHUNK), F32)
    mincl_s = jnp.asarray(_seq_mask(dseq), F32)
    bd = jnp.asarray(np.kron(np.eye(kvh), np.full((hd, hd), 1.0 / hd)), BF16)

    head_of = np.array([(p % kvh) * groups + p // kvh for p in range(nheads)])
    slopes = jnp.exp2(-8.0 * (jnp.asarray(head_of, F32) + 1.0) / nheads)
    qi = np.arange(window)[:, None]
    kj = np.arange(2 * window)[None, :]
    dist = window + qi - kj
    ok = (dist >= 0) & (dist <= window)
    ok_first = ok & (kj >= window)
    dist_f = jnp.asarray(dist, F32)
    bias_p = jnp.stack([
        jnp.where(jnp.asarray(okv)[None], -slopes[:, None, None] * dist_f[None], -jnp.inf) for okv in (ok_first, ok)])
    nkeys = -(-(window + dseq) // 16) * 16
    ti = np.tile(np.arange(dseq), nheads)[:, None]
    sj = np.arange(nkeys)[None, :]
    dist_s = ti + window - sj
    ok_s = (dist_s >= 0) & (dist_s <= window) & (sj < window + dseq)
    slopes_rows = jnp.repeat(slopes, dseq)[:, None]
    bias_s = jnp.where(jnp.asarray(ok_s), -slopes_rows * jnp.asarray(dist_s, F32), -jnp.inf)

    hist_all = jnp.pad(cache_conv, ((0, 0), (0, 0), (dseq - cache_conv.shape[2], 0), (0, 0)))
    hist_all = hist_all.reshape(depth, ts, conv_dim)
    ck_all = cache_swa_k.reshape(depth, dbatch, window, slab)
    cv_all = cache_swa_v.reshape(depth, dbatch, window, slab)

    outs = {n: [] for n in ("conv_p", "gdn_p", "k_p", "v_p", "conv_s", "gdn_s", "k_s", "v_s")}
    for i in range(depth):
        wi = w_in[i]
        w_main = jnp.concatenate([
            wi[:, :o_b], wi[:, o_sq:o_sk][:, perm], wi[:, o_g:o_g + d], wi[:, o_g + d:][:, perm],
            wi[:, o_sk:o_g]], axis=1).astype(BF16)
        w_ba = jnp.pad(wi[:, o_b:o_sq], ((0, 0), (0, LANES - 2 * heads))).astype(BF16)
        avec = jnp.zeros((2, LANES), F32).at[0, heads:2 * heads].set(a_log[i]).at[1, heads:2 * heads].set(dt_bias[i])
        proj, bgc = _inproj(h, norm_mix[i][None], w_main, w_ba, avec, masks,
                            n_prompt_rows=tp, heads=heads, tm=1024, tn=1536)

        gn = gdn_norm[i][None]
        oa_p, gdn_p = _gdn_prompt(proj, bgc, conv_w[i], gn, mincl_p, batch=batch, seq=seq, heads=heads, dk=dk,
                                  col=col, rows=512)
        oa_s, gdn_s = _gdn_sample(proj, bgc, hist_all[i], state_gdn[i], conv_w[i], gn, mincl_s, row0=tp,
                                  batch=dbatch, seq=dseq, heads=heads, dk=dk, col=col)

        sinks = attn_sinks[i][head_of]
        qw = jnp.tile(q_norm[i], kvh)[None]
        kw = jnp.tile(k_norm[i], kvh)[None]
        ob_p, k_p = _swa_prompt(proj, sinks, qw, kw, bd, bias_p, batch=batch, seq=seq, window=window, kvh=kvh,
                                hd=hd, nq=nq, col=col_wide)
        ob_s, k_s, v_s = _swa_sample(proj, ck_all[i], cv_all[i], qw, kw, bd, bias_s, jnp.repeat(sinks, dseq)[:, None],
                                     row0=tp, batch=dbatch, seq=dseq, kvh=kvh, hd=hd, nq=nq, col=col_wide, nseq=8)

        h = _dense(h, oa_p, oa_s, ob_p, ob_s, p_all[i], w_out[i].astype(BF16), w_out[i][perm].astype(BF16),
                   norm_ffn[i][None], w_up[i].astype(BF16), w_down[i].astype(BF16), norm_ple[i][None],
                   w_ple_gate[i].astype(BF16), w_ple_proj[i].astype(BF16), n_prompt_rows=tp, tm=512)

        proj_p = proj[:tp].reshape(batch, seq, n_proj)
        proj_s = proj[tp:].reshape(dbatch, dseq, n_proj)
        hist = conv_w.shape[1] - 1
        o_v = col_wide["sv"] * slab
        outs["conv_p"].append(proj_p[:, seq - hist:, :conv_dim])
        outs["conv_s"].append(proj_s[:, dseq - hist:, :conv_dim])
        outs["gdn_p"].append(gdn_p)
        outs["gdn_s"].append(gdn_s)
        outs["k_p"].append(k_p.reshape(batch, window, kvh, hd))
        outs["v_p"].append(proj_p[:, seq - window:, o_v:o_v + slab].reshape(batch, window, kvh, hd))
        outs["k_s"].append(k_s.reshape(dbatch, window, kvh, hd))
        outs["v_s"].append(v_s.reshape(dbatch, window, kvh, hd))

    st = lambda n: jnp.stack(outs[n])
    return (h[:tp].reshape(batch, seq, d), h[tp:].reshape(dbatch, dseq, d),
            st("conv_p"), st("gdn_p"), st("k_p"), st("v_p"), st("conv_s"), st("gdn_s"), st("k_s"), st("v_s"))
```

```python
import functools

import numpy as np
import jax
import jax.numpy as jnp
from jax import lax
from jax.experimental import pallas as pl
from jax.experimental.pallas import tpu as pltpu

F32 = jnp.float32
BF16 = jnp.bfloat16
EPS = 1e-6
LANES = 128
SUBLANES = 8
MXU_N = 256
GDN_CHUNK = 128
VMEM_LIMIT = 56 * 1024 * 1024


def _silu(x):
    return x * jax.nn.sigmoid(x)


def _dot(a, b):
    return jnp.dot(a.astype(BF16), b.astype(BF16), preferred_element_type=F32)


def _dot_nt(a, b):
    return lax.dot_general(a.astype(BF16), b.astype(BF16), (((1,), (1,)), ((), ())), preferred_element_type=F32)


def _dot_tn(a, b):
    return lax.dot_general(a.astype(BF16), b.astype(BF16), (((0,), (0,)), ((), ())), preferred_element_type=F32)


def _rms(x):
    return x * lax.rsqrt(jnp.mean(x * x, axis=-1, keepdims=True) + EPS)


def _l2(x):
    return x * lax.rsqrt(jnp.sum(x * x, axis=-1, keepdims=True) + EPS)


def _inproj_kernel(x_ref, nw_ref, w_ref, wba_ref, avec_ref, mask_ref, proj_ref, bgc_ref, xn_ref, *, heads):
    n = pl.program_id(1)

    @pl.when(n == 0)
    def _():
        xn = (_rms(x_ref[...]) * nw_ref[...]).astype(BF16)
        xn_ref[...] = xn
        ba = jnp.dot(xn, wba_ref[...], preferred_element_type=F32)
        beta = jax.nn.sigmoid(ba)
        xs = ba + avec_ref[1:2, :]
        softplus = jnp.maximum(xs, 0.0) + jnp.log1p(jnp.exp(-jnp.abs(xs)))
        g = -jnp.exp(avec_ref[0:1, :]) * softplus
        g1 = g.astype(BF16)
        r1 = g - g1.astype(F32)
        g2 = r1.astype(BF16)
        g3 = (r1 - g2.astype(F32)).astype(BF16)
        m = mask_ref[...]
        rows = g.shape[0]
        lane = lax.broadcasted_iota(jnp.int32, (GDN_CHUNK, LANES), 1)
        for r in range(rows // GDN_CHUNK):
            sl = slice(r * GDN_CHUNK, (r + 1) * GDN_CHUNK)
            gc = (jnp.dot(m, g1[sl], preferred_element_type=F32)
                  + jnp.dot(m, g2[sl], preferred_element_type=F32)
                  + jnp.dot(m, g3[sl], preferred_element_type=F32))
            bgc_ref[sl, :] = jnp.where(lane < heads, beta[sl], gc)

    xn = xn_ref[...]
    for j in range(w_ref.shape[1] // MXU_N):
        r = jnp.dot(xn, w_ref[:, j * MXU_N:(j + 1) * MXU_N], preferred_element_type=F32)
        for i in range(MXU_N // LANES):
            proj_ref[j * (MXU_N // LANES) + i] = r[:, i * LANES:(i + 1) * LANES]


def _inproj(h, nw, w, wba, avec, masks, *, n_prompt_rows, heads, tm, tn):
    t_all, d = h.shape
    n_out = w.shape[1]
    npt = n_prompt_rows // tm
    return pl.pallas_call(
        functools.partial(_inproj_kernel, heads=heads),
        grid=(t_all // tm, n_out // tn),
        in_specs=[
            pl.BlockSpec((tm, d), lambda m, n: (m, 0)),
            pl.BlockSpec((1, d), lambda m, n: (0, 0)),
            pl.BlockSpec((d, tn), lambda m, n: (0, n)),
            pl.BlockSpec((d, LANES), lambda m, n: (0, 0)),
            pl.BlockSpec((2, LANES), lambda m, n: (0, 0)),
            pl.BlockSpec((None, GDN_CHUNK, GDN_CHUNK), lambda m, n: (jnp.where(m >= npt, 1, 0), 0, 0)),
        ],
        out_specs=[
            pl.BlockSpec((tn // LANES, tm, LANES), lambda m, n: (n, m, 0)),
            pl.BlockSpec((tm, LANES), lambda m, n: (m, 0)),
        ],
        out_shape=[
            jax.ShapeDtypeStruct((n_out // LANES, t_all, LANES), F32),
            jax.ShapeDtypeStruct((t_all, LANES), F32),
        ],
        scratch_shapes=[pltpu.VMEM((tm, d), BF16)],
        compiler_params=pltpu.CompilerParams(
            dimension_semantics=("parallel", "arbitrary"), vmem_limit_bytes=VMEM_LIMIT),
        name="inproj",
    )(h, nw, w, wba, avec, masks)


def _gdn_chunks_local(qs, ks, vs, betas, gcs, mincl, seq_len):
    n = len(qs)
    c = qs[0].shape[0]
    dv = vs[0].shape[1]
    row = lax.broadcasted_iota(jnp.int32, (c, c), 0)
    col = lax.broadcasted_iota(jnp.int32, (c, c), 1)
    decs, egs = [], []
    for gc in gcs:
        gcb = jnp.broadcast_to(gc, (c, c))
        decs.append(jnp.exp(jnp.where(mincl > 0, gcb - gcb.T, -jnp.inf)))
        egs.append(jnp.exp(gc))
    kbs = [ks[i] * betas[i] for i in range(n)]
    kks = [_dot_nt(jnp.concatenate([kbs[i], qs[i]], axis=0), ks[i]) for i in range(n)]
    a_s = [jnp.where(row == col, 0.0, kks[i][:c] * decs[i]) for i in range(n)]
    qks = [kks[i][c:] * decs[i] for i in range(n)]
    same = (row >> 1) == (col >> 1)
    eye = jnp.where(row == col, 1.0, 0.0)
    ts = [eye - jnp.where(same, a, 0.0) for a in a_s]
    for lvl in range(1, seq_len.bit_length() - 1):
        wider = (row >> (lvl + 1)) == (col >> (lvl + 1))
        sel = wider & jnp.logical_not(same)
        t16 = [t.astype(BF16) for t in ts]
        mids = [jnp.dot(jnp.where(sel, a_s[i], 0.0).astype(BF16), t16[i], preferred_element_type=F32)
                for i in range(n)]
        ts = [ts[i] - jnp.dot(t16[i], mids[i].astype(BF16), preferred_element_type=F32) for i in range(n)]
        same = wider
    sols = [_dot(ts[i], jnp.concatenate([vs[i] * betas[i], kbs[i] * egs[i]], axis=1)) for i in range(n)]
    return [s[:, :dv] for s in sols], [s[:, dv:] for s in sols], qks, egs


def _conv_taps(u, ru_fn, cw):
    width = cw.shape[0]
    out = u * cw[width - 1:width, :]
    for j in range(1, width):
        out = out + ru_fn(j) * cw[width - 1 - j:width - j, :]
    return _silu(out)


def _gdn_finish(o, z, ga, gn):
    return (_rms(o) * gn * _silu(z) * jax.nn.sigmoid(ga)).astype(BF16)


def _head_scalars(bgc, h, heads):
    lane = lax.broadcasted_iota(jnp.int32, bgc.shape, 1)
    beta = jnp.sum(jnp.where(lane == h, bgc, 0.0), axis=-1, keepdims=True)
    gc = jnp.sum(jnp.where(lane == h + heads, bgc, 0.0), axis=-1, keepdims=True)
    return beta, gc


def _gdn_prompt_kernel(q_ref, k_ref, v_ref, z_ref, ga_ref, bgc_ref, cw_ref, gn_ref, mincl_ref,
                       o_ref, sfin_ref,
                       s_ref, tail_ref, u_s, wqe_s, qk_s, kd_s, el_s, *, heads_per_iter):
    c_id = pl.program_id(1)
    heads, rows, dk = q_ref.shape
    nchunk = rows // GDN_CHUNK
    cc = GDN_CHUNK

    @pl.when(c_id == 0)
    def _():
        s_ref[...] = jnp.zeros_like(s_ref)
        tail_ref[...] = jnp.zeros_like(tail_ref)

    row8 = lax.broadcasted_iota(jnp.int32, (SUBLANES, dk), 0)
    bgc = bgc_ref[...]
    mincl = mincl_ref[...]

    def conv(kind, h, u_ref):
        u = u_ref[h]
        tail = tail_ref[kind * heads + h]

        def delayed(j):
            ru = pltpu.roll(u, j, 0)
            head = jnp.where(row8 < j, pltpu.roll(tail, j, 0), ru[:SUBLANES])
            return jnp.concatenate([head, ru[SUBLANES:]], axis=0)

        out = _conv_taps(u, delayed, cw_ref[kind * heads + h])
        tail_ref[kind * heads + h] = u[rows - SUBLANES:]
        return out

    def local_body(it, carry):
        hs, qs, ks, vs, betas, gcs = [], [], [], [], [], []
        for hh in range(heads_per_iter):
            h = it * heads_per_iter + hh
            q = _l2(conv(0, h, q_ref)) * (dk ** -0.5)
            k = _l2(conv(1, h, k_ref))
            v = conv(2, h, v_ref)
            beta, gc = _head_scalars(bgc, h, heads)
            for ci in range(nchunk):
                sl = slice(ci * cc, (ci + 1) * cc)
                hs.append((h, ci))
                qs.append(q[sl]); ks.append(k[sl]); vs.append(v[sl]); betas.append(beta[sl]); gcs.append(gc[sl])
        us, ws, qks, egs = _gdn_chunks_local(qs, ks, vs, betas, gcs, mincl, cc)
        for i, (h, ci) in enumerate(hs):
            g_last = gcs[i][cc - 1:cc, :]
            u_s[h, ci] = us[i]
            wqe_s[h, ci, :cc] = ws[i].astype(BF16)
            wqe_s[h, ci, cc:] = (qs[i] * egs[i]).astype(BF16)
            qk_s[h, ci] = qks[i].astype(BF16)
            kd_s[h, ci] = (ks[i] * jnp.exp(g_last - gcs[i])).astype(BF16)
            el_s[h, ci] = jnp.broadcast_to(jnp.exp(g_last), (SUBLANES, dk))
        return carry

    lax.fori_loop(0, heads // heads_per_iter, local_body, 0)

    states = [s_ref[h] for h in range(heads)]
    gn = gn_ref[...]
    for ci in range(nchunk):
        sl = slice(ci * cc, (ci + 1) * cc)
        wqs = [jnp.dot(wqe_s[h, ci], states[h].astype(BF16), preferred_element_type=F32) for h in range(heads)]
        v_news = [u_s[h, ci] - wqs[h][:cc] for h in range(heads)]
        vn16 = [v.astype(BF16) for v in v_news]
        outs = [wqs[h][cc:] + jnp.dot(qk_s[h, ci], vn16[h], preferred_element_type=F32) for h in range(heads)]
        states = [states[h] * el_s[h, ci][0:1, :]
                  + lax.dot_general(kd_s[h, ci], vn16[h], (((0,), (0,)), ((), ())), preferred_element_type=F32)
                  for h in range(heads)]
        for h in range(heads):
            o_ref[sl, h * dk:(h + 1) * dk] = _gdn_finish(outs[h], z_ref[h, sl, :], ga_ref[h, sl, :], gn)
    for h in range(heads):
        s_ref[h] = states[h]
        sfin_ref[h] = states[h]


def _gdn_prompt(proj, bgc, cw, gn, mincl, *, batch, seq, heads, dk, col, rows, heads_per_iter):
    ncb = seq // rows
    nchunk = rows // GDN_CHUNK
    blk = lambda name: pl.BlockSpec((heads, rows, dk), lambda b, c, o=col[name]: (o, b * ncb + c, 0))
    return pl.pallas_call(
        functools.partial(_gdn_prompt_kernel, heads_per_iter=heads_per_iter),
        grid=(batch, ncb),
        in_specs=[
            blk("q"), blk("k"), blk("v"), blk("z"), blk("ga"),
            pl.BlockSpec((rows, LANES), lambda b, c: (b * ncb + c, 0)),
            pl.BlockSpec(cw.shape, lambda b, c: (0, 0, 0)),
            pl.BlockSpec((1, dk), lambda b, c: (0, 0)),
            pl.BlockSpec((GDN_CHUNK, GDN_CHUNK), lambda b, c: (0, 0)),
        ],
        out_specs=[
            pl.BlockSpec((rows, heads * dk), lambda b, c: (b * ncb + c, 0)),
            pl.BlockSpec((None, heads, dk, dk), lambda b, c: (b, 0, 0, 0)),
        ],
        out_shape=[
            jax.ShapeDtypeStruct((batch * seq, heads * dk), BF16),
            jax.ShapeDtypeStruct((batch, heads, dk, dk), F32),
        ],
        scratch_shapes=[
            pltpu.VMEM((heads, dk, dk), F32),
            pltpu.VMEM((3 * heads, SUBLANES, dk), F32),
            pltpu.VMEM((heads, nchunk, GDN_CHUNK, dk), F32),
            pltpu.VMEM((heads, nchunk, 2 * GDN_CHUNK, dk), BF16),
            pltpu.VMEM((heads, nchunk, GDN_CHUNK, GDN_CHUNK), BF16),
            pltpu.VMEM((heads, nchunk, GDN_CHUNK, dk), BF16),
            pltpu.VMEM((heads, nchunk, SUBLANES, dk), F32),
        ],
        compiler_params=pltpu.CompilerParams(
            dimension_semantics=("parallel", "arbitrary"), vmem_limit_bytes=VMEM_LIMIT),
        name="gdn_prompt",
    )(proj, proj, proj, proj, proj, bgc, cw, gn, mincl)


def _gdn_sample_kernel(q_ref, k_ref, v_ref, z_ref, ga_ref, bgc_ref, hq_ref, hk_ref, hv_ref,
                       cw_ref, gn_ref, mincl_ref, s0_ref, o_ref, sout_ref, *, heads, seq):
    h = pl.program_id(1)
    rows, dk = q_ref.shape
    nseq = rows // seq
    t_idx = lax.broadcasted_iota(jnp.int32, (nseq, seq, dk), 1)

    def conv(kind, u_ref, hist_ref):
        u = u_ref[...]
        u3 = u.reshape(nseq, seq, dk)
        hist3 = hist_ref[...].reshape(nseq, seq, dk)

        def delayed(j):
            xj = jnp.where(t_idx < j, pltpu.roll(hist3, j, 1), pltpu.roll(u3, j, 1))
            return xj.reshape(rows, dk)

        return _conv_taps(u, delayed, cw_ref[kind * heads + h])

    q = _l2(conv(0, q_ref, hq_ref)) * (dk ** -0.5)
    k = _l2(conv(1, k_ref, hk_ref))
    v = conv(2, v_ref, hv_ref)
    beta, gc = _head_scalars(bgc_ref[...], h, heads)

    (u,), (w,), (qk,), (eg,) = _gdn_chunks_local([q], [k], [v], [beta], [gc], mincl_ref[...], seq)
    qe = q * eg
    pad = jnp.zeros((seq, dk), F32)
    sls = [slice(si * seq, (si + 1) * seq) for si in range(nseq)]
    g_lasts = [gc[(si + 1) * seq - 1:(si + 1) * seq, :] for si in range(nseq)]
    s0 = [s0_ref[si] for si in range(nseq)]
    wqs = [_dot(jnp.concatenate([w[sl], qe[sl]], axis=0), s0[si]) for si, sl in enumerate(sls)]
    v_news = [u[sl] - wqs[si][:seq] for si, sl in enumerate(sls)]
    for si, sl in enumerate(sls):
        kd = k[sl] * jnp.exp(g_lasts[si] - gc[sl])
        sout_ref[si] = s0[si] * jnp.exp(g_lasts[si]) + _dot_tn(jnp.concatenate([kd, pad], axis=0),
                                                               jnp.concatenate([v_news[si], pad], axis=0))
    o = jnp.concatenate([wq[seq:] for wq in wqs], axis=0) + _dot(qk, jnp.concatenate(v_news, axis=0))
    o_ref[...] = _gdn_finish(o, z_ref[...], ga_ref[...], gn_ref[...])


def _gdn_sample(proj, bgc, hist, s0, cw, gn, mincl, *, row0, batch, seq, heads, dk, col):
    rows = GDN_CHUNK
    nseq = rows // seq
    r0 = row0 // rows
    blk = lambda name: pl.BlockSpec((None, rows, dk), lambda b, h, o=col[name]: (o * heads + h, r0 + b, 0))
    hblk = lambda name: pl.BlockSpec((rows, dk), lambda b, h, o=col[name]: (b, o * heads + h))
    return pl.pallas_call(
        functools.partial(_gdn_sample_kernel, heads=heads, seq=seq),
        grid=(batch * seq // rows, heads),
        in_specs=[
            blk("q"), blk("k"), blk("v"), blk("z"), blk("ga"),
            pl.BlockSpec((rows, LANES), lambda b, h: (r0 + b, 0)),
            hblk("q"), hblk("k"), hblk("v"),
            pl.BlockSpec(cw.shape, lambda b, h: (0, 0, 0)),
            pl.BlockSpec((1, dk), lambda b, h: (0, 0)),
            pl.BlockSpec((GDN_CHUNK, GDN_CHUNK), lambda b, h: (0, 0)),
            pl.BlockSpec((nseq, None, dk, dk), lambda b, h: (b, h, 0, 0)),
        ],
        out_specs=[
            pl.BlockSpec((rows, dk), lambda b, h: (b, h)),
            pl.BlockSpec((nseq, None, dk, dk), lambda b, h: (b, h, 0, 0)),
        ],
        out_shape=[
            jax.ShapeDtypeStruct((batch * seq, heads * dk), BF16),
            jax.ShapeDtypeStruct((batch, heads, dk, dk), F32),
        ],
        compiler_params=pltpu.CompilerParams(
            dimension_semantics=("parallel", "parallel"), vmem_limit_bytes=VMEM_LIMIT),
        name="gdn_sample",
    )(proj, proj, proj, proj, proj, bgc, hist, hist, hist, cw, gn, mincl, s0)


def _chunk_rms(x, bd, w):
    ms = _dot(x * x, bd)
    return x * lax.rsqrt(ms + EPS) * w


def _wide(ref, first, n):
    return jnp.concatenate([ref[first + i] for i in range(n)], axis=-1)


def _swa_prompt_kernel(sink_ref, q_ref, kp_ref, kc_ref, vp_ref, vc_ref, gb_ref, qw_ref, kw_ref, bd_ref, bias_ref,
                       o_ref, kout_ref, qs_ref, s_ref, p_ref, *, kvh, hd):
    blk = q_ref.shape[1]
    slab = kvh * hd
    spb = slab // LANES
    groups = q_ref.shape[0] // spb
    bd = bd_ref[...]
    kn = _chunk_rms(_wide(kc_ref, 0, spb), bd, kw_ref[...])
    kout_ref[...] = kn
    kband = jnp.concatenate([_chunk_rms(_wide(kp_ref, 0, spb), bd, kw_ref[...]), kn], axis=0).astype(BF16)
    vband = jnp.concatenate([_wide(vp_ref, 0, spb), _wide(vc_ref, 0, spb)], axis=0).astype(BF16)
    lane_head = lax.broadcasted_iota(jnp.int32, (blk, slab), 1) // hd

    for g in range(groups):
        qn = _chunk_rms(_wide(q_ref, g * spb, spb), bd, qw_ref[...]) * (hd ** -0.5)
        for j in range(kvh):
            p = g * kvh + j
            qs_ref[p * blk:(p + 1) * blk, :] = jnp.where(lane_head == j, qn, 0.0).astype(BF16)
    s_ref[...] = lax.dot_general(qs_ref[...], kband, (((1,), (1,)), ((), ())), preferred_element_type=F32)
    for p in range(groups * kvh):
        sl = slice(p * blk, (p + 1) * blk)
        s = s_ref[sl, :] + bias_ref[p]
        sink = sink_ref[p]
        m = jnp.maximum(jnp.max(s, axis=-1, keepdims=True), sink)
        e = jnp.exp(s - m)
        den = jnp.sum(e, axis=-1, keepdims=True) + jnp.exp(sink - m)
        p_ref[sl, :] = (e / den).astype(BF16)
    s_ref[...] = jnp.dot(p_ref[...], vband, preferred_element_type=F32)
    for g in range(groups):
        acc = jnp.zeros((blk, slab), F32)
        for j in range(kvh):
            p = g * kvh + j
            acc = acc + jnp.where(lane_head == j, s_ref[p * blk:(p + 1) * blk, :], 0.0)
        gate = jax.nn.sigmoid(_wide(gb_ref, g * spb, spb))
        o_ref[:, g * slab:(g + 1) * slab] = (acc * gate).astype(BF16)


def _swa_prompt(proj, sinks, qw, kw, bd, bias, *, batch, seq, window, kvh, hd, nq, col):
    nb = seq // window
    slab = kvh * hd
    nheads = nq // hd
    qb, sb = nq // LANES, slab // LANES
    cur = lambda name, n: pl.BlockSpec((n, window, LANES), lambda b, c, o=col[name] // n: (o, b * nb + c, 0))
    prev = lambda name, n: pl.BlockSpec(
        (n, window, LANES), lambda b, c, o=col[name] // n: (o, b * nb + jnp.maximum(c - 1, 0), 0))
    const2 = lambda b, c: (0, 0)
    return pl.pallas_call(
        functools.partial(_swa_prompt_kernel, kvh=kvh, hd=hd),
        grid=(batch, nb),
        in_specs=[
            pl.BlockSpec(memory_space=pltpu.SMEM),
            cur("sq", qb), prev("sk", sb), cur("sk", sb), prev("sv", sb), cur("sv", sb), cur("gb", qb),
            pl.BlockSpec((1, slab), const2),
            pl.BlockSpec((1, slab), const2),
            pl.BlockSpec((slab, slab), const2),
            pl.BlockSpec((None, nheads, window, 2 * window), lambda b, c: (jnp.minimum(c, 1), 0, 0, 0)),
        ],
        out_specs=[
            pl.BlockSpec((window, nq), lambda b, c: (b * nb + c, 0)),
            pl.BlockSpec((None, window, slab), lambda b, c: (b, 0, 0)),
        ],
        out_shape=[
            jax.ShapeDtypeStruct((batch * seq, nq), BF16),
            jax.ShapeDtypeStruct((batch, window, slab), F32),
        ],
        scratch_shapes=[
            pltpu.VMEM((nheads * window, slab), BF16),
            pltpu.VMEM((nheads * window, 2 * window), F32),
            pltpu.VMEM((nheads * window, 2 * window), BF16),
        ],
        compiler_params=pltpu.CompilerParams(
            dimension_semantics=("parallel", "arbitrary"), vmem_limit_bytes=VMEM_LIMIT),
        name="swa_prompt",
    )(sinks, proj, proj, proj, proj, proj, proj, qw, kw, bd, bias)


def _swa_sample_kernel(q_ref, k_ref, v_ref, gb_ref, ck_ref, cv_ref, qw_ref, kw_ref, bd_ref, bias_ref, sink_ref,
                       o_ref, kout_ref, vout_ref, *, kvh, hd, seq):
    rows = q_ref.shape[1]
    nseq = rows // seq
    slab = kvh * hd
    spb = slab // LANES
    groups = q_ref.shape[0] // spb
    window = ck_ref.shape[1]
    nkeys = bias_ref.shape[1]
    bd = bd_ref[...]
    kn = _chunk_rms(_wide(k_ref, 0, spb), bd, kw_ref[...]).reshape(nseq, seq, slab)
    vn = _wide(v_ref, 0, spb).reshape(nseq, seq, slab)
    ck = ck_ref[...]
    cv = cv_ref[...]
    kout_ref[...] = jnp.concatenate([ck[:, seq:, :], kn], axis=1)
    vout_ref[...] = jnp.concatenate([cv[:, seq:, :], vn], axis=1)
    zpad = jnp.zeros((nseq, nkeys - window - seq, slab), F32)
    k_all = jnp.concatenate([ck, kn, zpad], axis=1).astype(BF16)
    v_all = jnp.concatenate([cv, vn, zpad], axis=1).astype(BF16)

    lane_head = lax.broadcasted_iota(jnp.int32, (rows, slab), 1) // hd
    pieces = []
    for g in range(groups):
        qn = _chunk_rms(_wide(q_ref, g * spb, spb), bd, qw_ref[...]) * (hd ** -0.5)
        for j in range(kvh):
            pieces.append(jnp.where(lane_head == j, qn, 0.0).reshape(nseq, seq, slab))
    q_all = jnp.concatenate(pieces, axis=1).astype(BF16)
    s = jnp.einsum("bqd,bkd->bqk", q_all, k_all, preferred_element_type=F32) + bias_ref[...][None]
    sink = sink_ref[...][None]
    m = jnp.maximum(jnp.max(s, axis=-1, keepdims=True), sink)
    e = jnp.exp(s - m)
    den = jnp.sum(e, axis=-1, keepdims=True) + jnp.exp(sink - m)
    pv = jnp.einsum("bqk,bkd->bqd", (e / den).astype(BF16), v_all, preferred_element_type=F32)
    lane_head3 = lax.broadcasted_iota(jnp.int32, (nseq, seq, slab), 2) // hd
    for g in range(groups):
        acc = jnp.zeros((nseq, seq, slab), F32)
        for j in range(kvh):
            p = g * kvh + j
            acc = acc + jnp.where(lane_head3 == j, pv[:, p * seq:(p + 1) * seq, :], 0.0)
        gate = jax.nn.sigmoid(_wide(gb_ref, g * spb, spb))
        o_ref[:, g * slab:(g + 1) * slab] = (acc.reshape(rows, slab) * gate).astype(BF16)


def _swa_sample(proj, cache_k, cache_v, qw, kw, bd, bias, sink_col, *, row0, batch, seq, kvh, hd, nq, col, nseq):
    rows = nseq * seq
    slab = kvh * hd
    window = cache_k.shape[1]
    r0 = row0 // rows
    qb, sb = nq // LANES, slab // LANES
    cur = lambda name, n: pl.BlockSpec((n, rows, LANES), lambda b, o=col[name] // n: (o, r0 + b, 0))
    const2 = lambda b: (0, 0)
    return pl.pallas_call(
        functools.partial(_swa_sample_kernel, kvh=kvh, hd=hd, seq=seq),
        grid=(batch // nseq,),
        in_specs=[
            cur("sq", qb), cur("sk", sb), cur("sv", sb), cur("gb", qb),
            pl.BlockSpec((nseq, window, slab), lambda b: (b, 0, 0)),
            pl.BlockSpec((nseq, window, slab), lambda b: (b, 0, 0)),
            pl.BlockSpec((1, slab), const2),
            pl.BlockSpec((1, slab), const2),
            pl.BlockSpec((slab, slab), const2),
            pl.BlockSpec(bias.shape, const2),
            pl.BlockSpec(sink_col.shape, const2),
        ],
        out_specs=[
            pl.BlockSpec((rows, nq), lambda b: (b, 0)),
            pl.BlockSpec((nseq, window, slab), lambda b: (b, 0, 0)),
            pl.BlockSpec((nseq, window, slab), lambda b: (b, 0, 0)),
        ],
        out_shape=[
            jax.ShapeDtypeStruct((batch * seq, nq), BF16),
            jax.ShapeDtypeStruct((batch, window, slab), F32),
            jax.ShapeDtypeStruct((batch, window, slab), F32),
        ],
        compiler_params=pltpu.CompilerParams(
            dimension_semantics=("parallel",), vmem_limit_bytes=VMEM_LIMIT),
        name="swa_sample",
    )(proj, proj, proj, proj, cache_k, cache_v, qw, kw, bd, bias, sink_col)


def _dense_kernel(h_ref, oap_ref, oas_ref, obp_ref, obs_ref, p_ref, wo_ref, wop_ref, nf_ref, wu_ref, wd_ref,
                  np_ref, wg_ref, wp_ref, out_ref, *, npt, ff_chunk):
    is_prompt = pl.program_id(0) < npt
    oa = jnp.where(is_prompt, oap_ref[...], oas_ref[...])
    ob = jnp.where(is_prompt, obp_ref[...], obs_ref[...])
    h = (h_ref[...] + jnp.dot(oa, wo_ref[...], preferred_element_type=F32)
         + jnp.dot(ob, wop_ref[...], preferred_element_type=F32))
    xn = (_rms(h) * nf_ref[...]).astype(BF16)
    acc = jnp.zeros_like(h)
    for c in range(wu_ref.shape[1] // ff_chunk):
        sl = slice(c * ff_chunk, (c + 1) * ff_chunk)
        hid = jnp.maximum(jnp.dot(xn, wu_ref[:, sl], preferred_element_type=F32), 0.0)
        acc = acc + jnp.dot((hid * hid).astype(BF16), wd_ref[sl, :], preferred_element_type=F32)
    h = h + acc
    xn = (_rms(h) * np_ref[...]).astype(BF16)
    gate = jax.nn.sigmoid(jnp.dot(xn, wg_ref[...], preferred_element_type=F32))
    pe = jnp.dot(p_ref[...].astype(BF16), wp_ref[...], preferred_element_type=F32)
    out_ref[...] = h + gate * pe


def _dense(h, oa_p, oa_s, ob_p, ob_s, p, wo, wop, nf, wu, wd, npl, wg, wp, *, n_prompt_rows, tm):
    t_all, d = h.shape
    npt = n_prompt_rows // tm
    rows = lambda m: (m, 0)
    rows_p = lambda m: (jnp.minimum(m, npt - 1), 0)
    rows_s = lambda m: (jnp.maximum(m - npt, 0), 0)
    resident = lambda a: pl.BlockSpec(a.shape, lambda m: (0, 0), pipeline_mode=pl.Buffered(1))
    return pl.pallas_call(
        functools.partial(_dense_kernel, npt=npt, ff_chunk=1024),
        grid=(t_all // tm,),
        in_specs=[
            pl.BlockSpec((tm, d), rows),
            pl.BlockSpec((tm, d), rows_p), pl.BlockSpec((tm, d), rows_s),
            pl.BlockSpec((tm, d), rows_p), pl.BlockSpec((tm, d), rows_s),
            pl.BlockSpec((tm, p.shape[1]), rows),
            resident(wo), resident(wop), resident(nf), resident(wu), resident(wd),
            resident(npl), resident(wg), resident(wp),
        ],
        out_specs=pl.BlockSpec((tm, d), rows),
        out_shape=jax.ShapeDtypeStruct((t_all, d), F32),
        compiler_params=pltpu.CompilerParams(
            dimension_semantics=("parallel",), vmem_limit_bytes=VMEM_LIMIT),
        name="dense",
    )(h, oa_p, oa_s, ob_p, ob_s, p, wo, wop, nf, wu, wd, npl, wg, wp)


def _seq_mask(seq_len):
    i = np.arange(GDN_CHUNK)
    m = (i[:, None] // seq_len == i[None, :] // seq_len) & (i[None, :] <= i[:, None])
    return m.astype(np.float32)


def _tail_rows(proj, blocks, starts, n):
    b0, b1 = blocks
    parts = [lax.slice(proj, (b0, s, 0), (b1, s + n, proj.shape[2])) for s in starts]
    x = jnp.stack(parts)
    return jnp.transpose(x, (0, 2, 1, 3)).reshape(len(starts), n, (b1 - b0) * proj.shape[2])


def kernel(x_prompt, x_sample, cache_conv, state_gdn, cache_swa_k, cache_swa_v, p_prompt, p_sample, norm_mix, w_in, conv_w, a_log, dt_bias, gdn_norm, q_norm, k_norm, attn_sinks, w_out, norm_ffn, w_up, w_down, norm_ple, w_ple_gate, w_ple_proj):
    batch, seq, d = x_prompt.shape
    dbatch, dseq, _ = x_sample.shape
    depth = w_in.shape[0]
    heads, dk, dv = state_gdn.shape[2:]
    window, kvh, hd = cache_swa_k.shape[2:]
    nq = d
    nheads = nq // hd
    groups = nheads // kvh
    slab = kvh * hd
    key_dim = heads * dk
    val_dim = heads * dv
    conv_dim = 2 * key_dim + val_dim
    width = conv_w.shape[1]
    tp, ts = batch * seq, dbatch * dseq
    assert dk == LANES and dv == LANES and val_dim == d and key_dim == d and slab % LANES == 0
    assert seq % window == 0 and window == GDN_CHUNK and GDN_CHUNK % dseq == 0 and dseq >= width - 1
    assert dseq == SUBLANES and ts % GDN_CHUNK == 0

    o_z = conv_dim
    o_b = o_z + val_dim
    o_a = o_b + heads
    o_sq = o_a + heads
    o_sk = o_sq + nq
    o_sv = o_sk + slab
    o_g = o_sv + slab
    perm = np.concatenate([np.arange(hd) + (j * groups + g) * hd for g in range(groups) for j in range(kvh)])
    col_h = {"q": 0, "k": 1, "v": 2, "z": 3, "ga": 5}
    off = {"sq": conv_dim + val_dim, "gb": conv_dim + val_dim + nq + d,
           "sk": conv_dim + val_dim + nq + 2 * d, "sv": conv_dim + val_dim + nq + 2 * d + slab}
    col_b = {name: o // LANES for name, o in off.items()}
    cb = conv_dim // LANES

    h = jnp.concatenate([x_prompt.reshape(tp, d), x_sample.reshape(ts, d)], axis=0)
    p_all = jnp.concatenate([p_prompt.reshape(depth, tp, -1), p_sample.reshape(depth, ts, -1)], axis=1)

    masks = jnp.asarray(np.stack([_seq_mask(GDN_CHUNK), _seq_mask(dseq)]), BF16)
    mincl_p = jnp.asarray(_seq_mask(GDN_CHUNK), F32)
    mincl_s = jnp.asarray(_seq_mask(dseq), F32)
    bd = jnp.asarray(np.kron(np.eye(kvh), np.full((hd, hd), 1.0 / hd)), BF16)

    head_of = np.array([(p % kvh) * groups + p // kvh for p in range(nheads)])
    slopes = jnp.exp2(-8.0 * (jnp.asarray(head_of, F32) + 1.0) / nheads)
    qi = np.arange(window)[:, None]
    kj = np.arange(2 * window)[None, :]
    dist = window + qi - kj
    ok = (dist >= 0) & (dist <= window)
    ok_first = ok & (kj >= window)
    dist_f = jnp.asarray(dist, F32)
    bias_p = jnp.stack([
        jnp.where(jnp.asarray(okv)[None], -slopes[:, None, None] * dist_f[None], -jnp.inf) for okv in (ok_first, ok)])
    nkeys = -(-(window + dseq) // 16) * 16
    ti = np.tile(np.arange(dseq), nheads)[:, None]
    sj = np.arange(nkeys)[None, :]
    dist_s = ti + window - sj
    ok_s = (dist_s >= 0) & (dist_s <= window) & (sj < window + dseq)
    slopes_rows = jnp.repeat(slopes, dseq)[:, None]
    bias_s = jnp.where(jnp.asarray(ok_s), -slopes_rows * jnp.asarray(dist_s, F32), -jnp.inf)

    hist_all = jnp.pad(cache_conv, ((0, 0), (0, 0), (dseq - cache_conv.shape[2], 0), (0, 0)))
    hist_all = hist_all.reshape(depth, ts, conv_dim)
    ck_all = cache_swa_k.reshape(depth, dbatch, window, slab)
    cv_all = cache_swa_v.reshape(depth, dbatch, window, slab)
    cw_all = jnp.transpose(conv_w.reshape(depth, width, cb, LANES), (0, 2, 1, 3))

    outs = {n: [] for n in ("conv_p", "gdn_p", "k_p", "v_p", "conv_s", "gdn_s", "k_s", "v_s")}
    p_tail = [b * seq + seq - (width - 1) for b in range(batch)]
    p_win = [b * seq + seq - window for b in range(batch)]
    for i in range(depth):
        wi = w_in[i]
        w_main = jnp.concatenate([
            wi[:, :o_b], wi[:, o_sq:o_sk][:, perm], wi[:, o_g:o_g + d], wi[:, o_g + d:][:, perm],
            wi[:, o_sk:o_g]], axis=1).astype(BF16)
        w_ba = jnp.pad(wi[:, o_b:o_sq], ((0, 0), (0, LANES - 2 * heads))).astype(BF16)
        avec = jnp.zeros((2, LANES), F32).at[0, heads:2 * heads].set(a_log[i]).at[1, heads:2 * heads].set(dt_bias[i])
        proj, bgc = _inproj(h, norm_mix[i][None], w_main, w_ba, avec, masks,
                            n_prompt_rows=tp, heads=heads, tm=1024, tn=1536)

        gn = gdn_norm[i][None]
        oa_p, gdn_p = _gdn_prompt(proj, bgc, cw_all[i], gn, mincl_p, batch=batch, seq=seq, heads=heads, dk=dk,
                                  col=col_h, rows=512, heads_per_iter=2)
        oa_s, gdn_s = _gdn_sample(proj, bgc, hist_all[i], state_gdn[i], cw_all[i], gn, mincl_s, row0=tp,
                                  batch=dbatch, seq=dseq, heads=heads, dk=dk, col=col_h)

        sinks = attn_sinks[i][head_of]
        qw = jnp.tile(q_norm[i], kvh)[None]
        kw = jnp.tile(k_norm[i], kvh)[None]
        ob_p, k_p = _swa_prompt(proj, sinks, qw, kw, bd, bias_p, batch=batch, seq=seq, window=window, kvh=kvh,
                                hd=hd, nq=nq, col=col_b)
        ob_s, k_s, v_s = _swa_sample(proj, ck_all[i], cv_all[i], qw, kw, bd, bias_s, jnp.repeat(sinks, dseq)[:, None],
                                     row0=tp, batch=dbatch, seq=dseq, kvh=kvh, hd=hd, nq=nq, col=col_b, nseq=8)

        h = _dense(h, oa_p, oa_s, ob_p, ob_s, p_all[i], w_out[i].astype(BF16), w_out[i][perm].astype(BF16),
                   norm_ffn[i][None], w_up[i].astype(BF16), w_down[i].astype(BF16), norm_ple[i][None],
                   w_ple_gate[i].astype(BF16), w_ple_proj[i].astype(BF16), n_prompt_rows=tp, tm=512)

        outs["conv_p"].append(_tail_rows(proj, (0, cb), p_tail, width - 1))
        conv_s = lax.slice(proj, (0, tp, 0), (cb, tp + ts, LANES)).reshape(cb, dbatch, dseq, LANES)
        outs["conv_s"].append(
            jnp.transpose(conv_s[:, :, dseq - (width - 1):], (1, 2, 0, 3)).reshape(dbatch, width - 1, conv_dim))
        outs["gdn_p"].append(gdn_p)
        outs["gdn_s"].append(gdn_s)
        outs["k_p"].append(k_p.reshape(batch, window, kvh, hd))
        v_blocks = (col_b["sv"], col_b["sv"] + slab // LANES)
        outs["v_p"].append(_tail_rows(proj, v_blocks, p_win, window).reshape(batch, window, kvh, hd))
        outs["k_s"].append(k_s.reshape(dbatch, window, kvh, hd))
        outs["v_s"].append(v_s.reshape(dbatch, window, kvh, hd))

    st = lambda n: jnp.stack(outs[n])
    return (h[:tp].reshape(batch, seq, d), h[tp:].reshape(dbatch, dseq, d),
            st("conv_p"), st("gdn_p"), st("k_p"), st("v_p"), st("conv_s"), st("gdn_s"), st("k_s"), st("v_s"))
```

```python
import functools

import numpy as np
import jax
import jax.numpy as jnp
from jax import lax
from jax.experimental import pallas as pl
from jax.experimental.pallas import tpu as pltpu

F32 = jnp.float32
BF16 = jnp.bfloat16
EPS = 1e-6
LANES = 128
SUBLANES = 8
MXU_N = 256
GDN_CHUNK = 128
VMEM_LIMIT = 56 * 1024 * 1024


def _sigmoid(x):
    return 0.5 * jnp.tanh(0.5 * x) + 0.5


def _silu(x):
    hx = 0.5 * x
    return hx + hx * jnp.tanh(hx)


def _dot(a, b):
    return jnp.dot(a.astype(BF16), b.astype(BF16), preferred_element_type=F32)


def _dot_nt(a, b):
    return lax.dot_general(a.astype(BF16), b.astype(BF16), (((1,), (1,)), ((), ())), preferred_element_type=F32)


def _dot_tn(a, b):
    return lax.dot_general(a.astype(BF16), b.astype(BF16), (((0,), (0,)), ((), ())), preferred_element_type=F32)


def _rms(x):
    return x * lax.rsqrt(jnp.mean(x * x, axis=-1, keepdims=True) + EPS)


def _l2(x):
    return x * lax.rsqrt(jnp.sum(x * x, axis=-1, keepdims=True) + EPS)


def _inproj_kernel(x_ref, nw_ref, w_ref, wba_ref, avec_ref, mask_ref, proj_ref, bgc_ref, xn_ref, *, heads):
    n = pl.program_id(1)

    @pl.when(n == 0)
    def _():
        xn = (_rms(x_ref[...]) * nw_ref[...]).astype(BF16)
        xn_ref[...] = xn
        ba = jnp.dot(xn, wba_ref[...], preferred_element_type=F32)
        beta = _sigmoid(ba)
        xs = ba + avec_ref[1:2, :]
        softplus = jnp.maximum(xs, 0.0) + jnp.log1p(jnp.exp(-jnp.abs(xs)))
        g = -jnp.exp(avec_ref[0:1, :]) * softplus
        g1 = g.astype(BF16)
        r1 = g - g1.astype(F32)
        g2 = r1.astype(BF16)
        g3 = (r1 - g2.astype(F32)).astype(BF16)
        m = mask_ref[...]
        rows = g.shape[0]
        lane = lax.broadcasted_iota(jnp.int32, (GDN_CHUNK, LANES), 1)
        for r in range(rows // GDN_CHUNK):
            sl = slice(r * GDN_CHUNK, (r + 1) * GDN_CHUNK)
            gc = (jnp.dot(m, g1[sl], preferred_element_type=F32)
                  + jnp.dot(m, g2[sl], preferred_element_type=F32)
                  + jnp.dot(m, g3[sl], preferred_element_type=F32))
            bgc_ref[sl, :] = jnp.where(lane < heads, beta[sl], gc)

    xn = xn_ref[...]
    for j in range(w_ref.shape[1] // MXU_N):
        r = jnp.dot(xn, w_ref[:, j * MXU_N:(j + 1) * MXU_N], preferred_element_type=F32)
        for i in range(MXU_N // LANES):
            proj_ref[j * (MXU_N // LANES) + i] = r[:, i * LANES:(i + 1) * LANES]


def _inproj(h, nw, w, wba, avec, masks, *, n_prompt_rows, heads, tm, tn):
    t_all, d = h.shape
    n_out = w.shape[1]
    npt = n_prompt_rows // tm
    return pl.pallas_call(
        functools.partial(_inproj_kernel, heads=heads),
        grid=(t_all // tm, n_out // tn),
        in_specs=[
            pl.BlockSpec((tm, d), lambda m, n: (m, 0)),
            pl.BlockSpec((1, d), lambda m, n: (0, 0)),
            pl.BlockSpec((d, tn), lambda m, n: (0, n)),
            pl.BlockSpec((d, LANES), lambda m, n: (0, 0)),
            pl.BlockSpec((2, LANES), lambda m, n: (0, 0)),
            pl.BlockSpec((None, GDN_CHUNK, GDN_CHUNK), lambda m, n: (jnp.where(m >= npt, 1, 0), 0, 0)),
        ],
        out_specs=[
            pl.BlockSpec((tn // LANES, tm, LANES), lambda m, n: (n, m, 0)),
            pl.BlockSpec((tm, LANES), lambda m, n: (m, 0)),
        ],
        out_shape=[
            jax.ShapeDtypeStruct((n_out // LANES, t_all, LANES), F32),
            jax.ShapeDtypeStruct((t_all, LANES), F32),
        ],
        scratch_shapes=[pltpu.VMEM((tm, d), BF16)],
        compiler_params=pltpu.CompilerParams(
            dimension_semantics=("parallel", "arbitrary"), vmem_limit_bytes=VMEM_LIMIT),
        name="inproj",
    )(h, nw, w, wba, avec, masks)


def _gdn_chunks_local(qs, ks, vs, betas, gcs, mincl, seq_len):
    n = len(qs)
    c = qs[0].shape[0]
    dv = vs[0].shape[1]
    row = lax.broadcasted_iota(jnp.int32, (c, c), 0)
    col = lax.broadcasted_iota(jnp.int32, (c, c), 1)
    decs, egs = [], []
    for gc in gcs:
        gcb = jnp.broadcast_to(gc, (c, c))
        decs.append(jnp.exp(jnp.where(mincl > 0, gcb - gcb.T, -jnp.inf)))
        egs.append(jnp.exp(gc))
    kbs = [ks[i] * betas[i] for i in range(n)]
    kks = [_dot_nt(jnp.concatenate([kbs[i], qs[i]], axis=0), ks[i]) for i in range(n)]
    a_s = [jnp.where(row == col, 0.0, kks[i][:c] * decs[i]) for i in range(n)]
    qks = [kks[i][c:] * decs[i] for i in range(n)]
    same = (row >> 1) == (col >> 1)
    eye = jnp.where(row == col, 1.0, 0.0)
    ts = [eye - jnp.where(same, a, 0.0) for a in a_s]
    na16 = [(-a).astype(BF16) for a in a_s]
    for lvl in range(1, seq_len.bit_length() - 1):
        wider = (row >> (lvl + 1)) == (col >> (lvl + 1))
        sel = wider & jnp.logical_not(same)
        t16 = [t.astype(BF16) for t in ts]
        mids = [jnp.dot(na16[i], t16[i], preferred_element_type=F32) for i in range(n)]
        ts = [jnp.where(sel, jnp.dot(t16[i], mids[i].astype(BF16), preferred_element_type=F32), ts[i])
              for i in range(n)]
        same = wider
    sols = [_dot(ts[i], jnp.concatenate([vs[i] * betas[i], kbs[i] * egs[i]], axis=1)) for i in range(n)]
    return [s[:, :dv] for s in sols], [s[:, dv:] for s in sols], qks, egs


def _conv_taps(u, ru_fn, cw):
    width = cw.shape[0]
    out = u * cw[width - 1:width, :]
    for j in range(1, width):
        out = out + ru_fn(j) * cw[width - 1 - j:width - j, :]
    return _silu(out)


def _gdn_finish(o, z, ga, gn):
    return (_rms(o) * gn * _silu(z) * _sigmoid(ga)).astype(BF16)


def _head_scalars(bgc, h, heads):
    lane = lax.broadcasted_iota(jnp.int32, bgc.shape, 1)
    beta = jnp.sum(jnp.where(lane == h, bgc, 0.0), axis=-1, keepdims=True)
    gc = jnp.sum(jnp.where(lane == h + heads, bgc, 0.0), axis=-1, keepdims=True)
    return beta, gc


def _gdn_prompt_kernel(q_ref, k_ref, v_ref, z_ref, ga_ref, bgc_ref, cw_ref, gn_ref, mincl_ref,
                       o_ref, sfin_ref,
                       s_ref, tail_ref, u_s, wqe_s, qk_s, kd_s, el_s, *, heads_per_iter):
    c_id = pl.program_id(1)
    heads, rows, dk = q_ref.shape
    nchunk = rows // GDN_CHUNK
    cc = GDN_CHUNK

    @pl.when(c_id == 0)
    def _():
        s_ref[...] = jnp.zeros_like(s_ref)
        tail_ref[...] = jnp.zeros_like(tail_ref)

    row8 = lax.broadcasted_iota(jnp.int32, (SUBLANES, dk), 0)
    bgc = bgc_ref[...]
    mincl = mincl_ref[...]

    def conv(kind, h, u_ref):
        u = u_ref[h]
        tail = tail_ref[kind * heads + h]

        def delayed(j):
            head = jnp.where(row8 < j, pltpu.roll(tail, j, 0), pltpu.roll(u[:SUBLANES], j, 0))
            return jnp.concatenate([head, u_ref[h, pl.ds(SUBLANES - j, rows - SUBLANES), :]], axis=0)

        out = _conv_taps(u, delayed, cw_ref[kind * heads + h])
        tail_ref[kind * heads + h] = u[rows - SUBLANES:]
        return out

    def local_body(it, carry):
        hs, qs, ks, vs, betas, gcs = [], [], [], [], [], []
        for hh in range(heads_per_iter):
            h = it * heads_per_iter + hh
            q = _l2(conv(0, h, q_ref)) * (dk ** -0.5)
            k = _l2(conv(1, h, k_ref))
            v = conv(2, h, v_ref)
            beta, gc = _head_scalars(bgc, h, heads)
            for ci in range(nchunk):
                sl = slice(ci * cc, (ci + 1) * cc)
                hs.append((h, ci))
                qs.append(q[sl]); ks.append(k[sl]); vs.append(v[sl]); betas.append(beta[sl]); gcs.append(gc[sl])
        us, ws, qks, egs = _gdn_chunks_local(qs, ks, vs, betas, gcs, mincl, cc)
        for i, (h, ci) in enumerate(hs):
            g_last = gcs[i][cc - 1:cc, :]
            u_s[h, ci] = us[i]
            wqe_s[h, ci, :cc] = ws[i].astype(BF16)
            wqe_s[h, ci, cc:] = (qs[i] * egs[i]).astype(BF16)
            qk_s[h, ci] = qks[i].astype(BF16)
            kd_s[h, ci] = (ks[i] * jnp.exp(g_last - gcs[i])).astype(BF16)
            el_s[h, ci] = jnp.broadcast_to(jnp.exp(g_last), (SUBLANES, dk))
        return carry

    lax.fori_loop(0, heads // heads_per_iter, local_body, 0)

    states = [s_ref[h] for h in range(heads)]
    gn = gn_ref[...]
    for ci in range(nchunk):
        sl = slice(ci * cc, (ci + 1) * cc)
        wqs = [jnp.dot(wqe_s[h, ci], states[h].astype(BF16), preferred_element_type=F32) for h in range(heads)]
        v_news = [u_s[h, ci] - wqs[h][:cc] for h in range(heads)]
        vn16 = [v.astype(BF16) for v in v_news]
        outs = [wqs[h][cc:] + jnp.dot(qk_s[h, ci], vn16[h], preferred_element_type=F32) for h in range(heads)]
        states = [states[h] * el_s[h, ci][0:1, :]
                  + lax.dot_general(kd_s[h, ci], vn16[h], (((0,), (0,)), ((), ())), preferred_element_type=F32)
                  for h in range(heads)]
        for h in range(heads):
            o_ref[sl, h * dk:(h + 1) * dk] = _gdn_finish(outs[h], z_ref[h, sl, :], ga_ref[h, sl, :], gn)
    for h in range(heads):
        s_ref[h] = states[h]
        sfin_ref[h] = states[h]


def _gdn_prompt(proj, bgc, cw, gn, mincl, *, batch, seq, heads, dk, col, rows, heads_per_iter):
    ncb = seq // rows
    nchunk = rows // GDN_CHUNK
    blk = lambda name: pl.BlockSpec((heads, rows, dk), lambda b, c, o=col[name]: (o, b * ncb + c, 0))
    return pl.pallas_call(
        functools.partial(_gdn_prompt_kernel, heads_per_iter=heads_per_iter),
        grid=(batch, ncb),
        in_specs=[
            blk("q"), blk("k"), blk("v"), blk("z"), blk("ga"),
            pl.BlockSpec((rows, LANES), lambda b, c: (b * ncb + c, 0)),
            pl.BlockSpec(cw.shape, lambda b, c: (0, 0, 0)),
            pl.BlockSpec((1, dk), lambda b, c: (0, 0)),
            pl.BlockSpec((GDN_CHUNK, GDN_CHUNK), lambda b, c: (0, 0)),
        ],
        out_specs=[
            pl.BlockSpec((rows, heads * dk), lambda b, c: (b * ncb + c, 0)),
            pl.BlockSpec((None, heads, dk, dk), lambda b, c: (b, 0, 0, 0)),
        ],
        out_shape=[
            jax.ShapeDtypeStruct((batch * seq, heads * dk), BF16),
            jax.ShapeDtypeStruct((batch, heads, dk, dk), F32),
        ],
        scratch_shapes=[
            pltpu.VMEM((heads, dk, dk), F32),
            pltpu.VMEM((3 * heads, SUBLANES, dk), F32),
            pltpu.VMEM((heads, nchunk, GDN_CHUNK, dk), F32),
            pltpu.VMEM((heads, nchunk, 2 * GDN_CHUNK, dk), BF16),
            pltpu.VMEM((heads, nchunk, GDN_CHUNK, GDN_CHUNK), BF16),
            pltpu.VMEM((heads, nchunk, GDN_CHUNK, dk), BF16),
            pltpu.VMEM((heads, nchunk, SUBLANES, dk), F32),
        ],
        compiler_params=pltpu.CompilerParams(
            dimension_semantics=("parallel", "arbitrary"), vmem_limit_bytes=VMEM_LIMIT),
        name="gdn_prompt",
    )(proj, proj, proj, proj, proj, bgc, cw, gn, mincl)


def _gdn_sample_kernel(q_ref, k_ref, v_ref, z_ref, ga_ref, bgc_ref, hq_ref, hk_ref, hv_ref,
                       cw_ref, gn_ref, mincl_ref, s0_ref, *rest, heads, seq):
    o_ref, sout_ref = rest[-2:]
    h = pl.program_id(1)
    rows, dk = q_ref.shape
    nseq = rows // seq
    t_idx = lax.broadcasted_iota(jnp.int32, (nseq, seq, dk), 1)

    def conv(kind, u_ref, hist_ref):
        u = u_ref[...]
        u3 = u.reshape(nseq, seq, dk)
        hist3 = hist_ref[...].reshape(nseq, seq, dk)

        def delayed(j):
            xj = jnp.where(t_idx < j, pltpu.roll(hist3, j, 1), pltpu.roll(u3, j, 1))
            return xj.reshape(rows, dk)

        return _conv_taps(u, delayed, cw_ref[kind * heads + h])

    q = _l2(conv(0, q_ref, hq_ref)) * (dk ** -0.5)
    k = _l2(conv(1, k_ref, hk_ref))
    v = conv(2, v_ref, hv_ref)
    beta, gc = _head_scalars(bgc_ref[...], h, heads)

    (u,), (w,), (qk,), (eg,) = _gdn_chunks_local([q], [k], [v], [beta], [gc], mincl_ref[...], seq)
    qe = q * eg
    pad = jnp.zeros((seq, dk), F32)
    sls = [slice(si * seq, (si + 1) * seq) for si in range(nseq)]
    g_lasts = [gc[(si + 1) * seq - 1:(si + 1) * seq, :] for si in range(nseq)]
    s0 = [s0_ref[si] for si in range(nseq)]
    wqs = [_dot(jnp.concatenate([w[sl], qe[sl]], axis=0), s0[si]) for si, sl in enumerate(sls)]
    v_news = [u[sl] - wqs[si][:seq] for si, sl in enumerate(sls)]
    for si, sl in enumerate(sls):
        kd = k[sl] * jnp.exp(g_lasts[si] - gc[sl])
        sout_ref[si] = s0[si] * jnp.exp(g_lasts[si]) + _dot_tn(jnp.concatenate([kd, pad], axis=0),
                                                               jnp.concatenate([v_news[si], pad], axis=0))
    o = jnp.concatenate([wq[seq:] for wq in wqs], axis=0) + _dot(qk, jnp.concatenate(v_news, axis=0))
    o_ref[...] = _gdn_finish(o, z_ref[...], ga_ref[...], gn_ref[...])


def _gdn_sample(proj, bgc, hist, s0, cw, gn, mincl, s_prev, *, layer, row0, batch, seq, heads, dk, col):
    rows = GDN_CHUNK
    nseq = rows // seq
    r0 = row0 // rows
    depth = s0.shape[0]
    blk = lambda name: pl.BlockSpec((None, rows, dk), lambda b, h, o=col[name]: (o * heads + h, r0 + b, 0))
    hblk = lambda name: pl.BlockSpec((None, rows, dk), lambda b, h, o=col[name]: (layer, b, o * heads + h))
    state_blk = pl.BlockSpec((None, nseq, None, dk, dk), lambda b, h: (layer, b, h, 0, 0))
    extra_in = [] if s_prev is None else [s_prev]
    return pl.pallas_call(
        functools.partial(_gdn_sample_kernel, heads=heads, seq=seq),
        grid=(batch * seq // rows, heads),
        in_specs=[
            blk("q"), blk("k"), blk("v"), blk("z"), blk("ga"),
            pl.BlockSpec((rows, LANES), lambda b, h: (r0 + b, 0)),
            hblk("q"), hblk("k"), hblk("v"),
            pl.BlockSpec(cw.shape, lambda b, h: (0, 0, 0)),
            pl.BlockSpec((1, dk), lambda b, h: (0, 0)),
            pl.BlockSpec((GDN_CHUNK, GDN_CHUNK), lambda b, h: (0, 0)),
            state_blk,
        ] + [pl.BlockSpec(memory_space=pl.ANY)] * len(extra_in),
        out_specs=[
            pl.BlockSpec((rows, dk), lambda b, h: (b, h)),
            state_blk,
        ],
        out_shape=[
            jax.ShapeDtypeStruct((batch * seq, heads * dk), BF16),
            jax.ShapeDtypeStruct((depth, batch, heads, dk, dk), F32),
        ],
        input_output_aliases={13: 1} if extra_in else {},
        compiler_params=pltpu.CompilerParams(
            dimension_semantics=("parallel", "parallel"), vmem_limit_bytes=VMEM_LIMIT),
        name="gdn_sample",
    )(proj, proj, proj, proj, proj, bgc, hist, hist, hist, cw, gn, mincl, s0, *extra_in)


def _chunk_rms(x, bd, w):
    ms = _dot(x * x, bd)
    return x * lax.rsqrt(ms + EPS) * w


def _wide(ref, first, n):
    return jnp.concatenate([ref[first + i] for i in range(n)], axis=-1)


def _swa_prompt_kernel(sink_ref, q_ref, kp_ref, kc_ref, vp_ref, vc_ref, gb_ref, qw_ref, kw_ref, bd_ref, bias_ref,
                       o_ref, kout_ref, qs_ref, s_ref, p_ref, *, kvh, hd):
    blk = q_ref.shape[1]
    slab = kvh * hd
    spb = slab // LANES
    groups = q_ref.shape[0] // spb
    bd = bd_ref[...]
    kn = _chunk_rms(_wide(kc_ref, 0, spb), bd, kw_ref[...])
    kout_ref[...] = kn
    kband = jnp.concatenate([_chunk_rms(_wide(kp_ref, 0, spb), bd, kw_ref[...]), kn], axis=0).astype(BF16)
    vband = jnp.concatenate([_wide(vp_ref, 0, spb), _wide(vc_ref, 0, spb)], axis=0).astype(BF16)
    lane_head = lax.broadcasted_iota(jnp.int32, (blk, slab), 1) // hd

    for g in range(groups):
        qn = _chunk_rms(_wide(q_ref, g * spb, spb), bd, qw_ref[...]) * (hd ** -0.5)
        for j in range(kvh):
            p = g * kvh + j
            qs_ref[p * blk:(p + 1) * blk, :] = jnp.where(lane_head == j, qn, 0.0).astype(BF16)
    s_ref[...] = lax.dot_general(qs_ref[...], kband, (((1,), (1,)), ((), ())), preferred_element_type=F32)
    for p in range(groups * kvh):
        sl = slice(p * blk, (p + 1) * blk)
        s = s_ref[sl, :] + bias_ref[p]
        sink = sink_ref[p]
        m = jnp.maximum(jnp.max(s, axis=-1, keepdims=True), sink)
        e = jnp.exp(s - m)
        den = jnp.sum(e, axis=-1, keepdims=True) + jnp.exp(sink - m)
        p_ref[sl, :] = (e * (1.0 / den)).astype(BF16)
    s_ref[...] = jnp.dot(p_ref[...], vband, preferred_element_type=F32)
    for g in range(groups):
        acc = jnp.zeros((blk, slab), F32)
        for j in range(kvh):
            p = g * kvh + j
            acc = acc + jnp.where(lane_head == j, s_ref[p * blk:(p + 1) * blk, :], 0.0)
        gate = _sigmoid(_wide(gb_ref, g * spb, spb))
        o_ref[:, g * slab:(g + 1) * slab] = (acc * gate).astype(BF16)


def _swa_prompt(proj, sinks, qw, kw, bd, bias, *, batch, seq, window, kvh, hd, nq, col):
    nb = seq // window
    slab = kvh * hd
    nheads = nq // hd
    qb, sb = nq // LANES, slab // LANES
    cur = lambda name, n: pl.BlockSpec((n, window, LANES), lambda b, c, o=col[name] // n: (o, b * nb + c, 0))
    prev = lambda name, n: pl.BlockSpec(
        (n, window, LANES), lambda b, c, o=col[name] // n: (o, b * nb + jnp.maximum(c - 1, 0), 0))
    const2 = lambda b, c: (0, 0)
    return pl.pallas_call(
        functools.partial(_swa_prompt_kernel, kvh=kvh, hd=hd),
        grid=(batch, nb),
        in_specs=[
            pl.BlockSpec(memory_space=pltpu.SMEM),
            cur("sq", qb), prev("sk", sb), cur("sk", sb), prev("sv", sb), cur("sv", sb), cur("gb", qb),
            pl.BlockSpec((1, slab), const2),
            pl.BlockSpec((1, slab), const2),
            pl.BlockSpec((slab, slab), const2),
            pl.BlockSpec((None, nheads, window, 2 * window), lambda b, c: (jnp.minimum(c, 1), 0, 0, 0)),
        ],
        out_specs=[
            pl.BlockSpec((window, nq), lambda b, c: (b * nb + c, 0)),
            pl.BlockSpec((None, window, slab), lambda b, c: (b, 0, 0)),
        ],
        out_shape=[
            jax.ShapeDtypeStruct((batch * seq, nq), BF16),
            jax.ShapeDtypeStruct((batch, window, slab), F32),
        ],
        scratch_shapes=[
            pltpu.VMEM((nheads * window, slab), BF16),
            pltpu.VMEM((nheads * window, 2 * window), F32),
            pltpu.VMEM((nheads * window, 2 * window), BF16),
        ],
        compiler_params=pltpu.CompilerParams(
            dimension_semantics=("parallel", "arbitrary"), vmem_limit_bytes=VMEM_LIMIT),
        name="swa_prompt",
    )(sinks, proj, proj, proj, proj, proj, proj, qw, kw, bd, bias)


def _swa_sample_kernel(q_ref, k_ref, v_ref, gb_ref, ck_ref, cv_ref, qw_ref, kw_ref, bd_ref, bias_ref, sink_ref,
                       *rest, kvh, hd, seq):
    o_ref, kout_ref, vout_ref = rest[-3:]
    rows = q_ref.shape[1]
    nseq = rows // seq
    slab = kvh * hd
    spb = slab // LANES
    groups = q_ref.shape[0] // spb
    window = ck_ref.shape[1]
    nkeys = bias_ref.shape[1]
    bd = bd_ref[...]
    kn = _chunk_rms(_wide(k_ref, 0, spb), bd, kw_ref[...]).reshape(nseq, seq, slab)
    vn = _wide(v_ref, 0, spb).reshape(nseq, seq, slab)
    ck = ck_ref[...]
    cv = cv_ref[...]
    kout_ref[...] = jnp.concatenate([ck[:, seq:, :], kn], axis=1)
    vout_ref[...] = jnp.concatenate([cv[:, seq:, :], vn], axis=1)
    zpad = jnp.zeros((nseq, nkeys - window - seq, slab), F32)
    k_all = jnp.concatenate([ck, kn, zpad], axis=1).astype(BF16)
    v_all = jnp.concatenate([cv, vn, zpad], axis=1).astype(BF16)

    lane_head = lax.broadcasted_iota(jnp.int32, (rows, slab), 1) // hd
    pieces = []
    for g in range(groups):
        qn = _chunk_rms(_wide(q_ref, g * spb, spb), bd, qw_ref[...]) * (hd ** -0.5)
        for j in range(kvh):
            pieces.append(jnp.where(lane_head == j, qn, 0.0).reshape(nseq, seq, slab))
    q_all = jnp.concatenate(pieces, axis=1).astype(BF16)
    s = jnp.einsum("bqd,bkd->bqk", q_all, k_all, preferred_element_type=F32) + bias_ref[...][None]
    sink = sink_ref[...][None]
    m = jnp.maximum(jnp.max(s, axis=-1, keepdims=True), sink)
    e = jnp.exp(s - m)
    den = jnp.sum(e, axis=-1, keepdims=True) + jnp.exp(sink - m)
    pv = jnp.einsum("bqk,bkd->bqd", (e * (1.0 / den)).astype(BF16), v_all, preferred_element_type=F32)
    lane_head3 = lax.broadcasted_iota(jnp.int32, (nseq, seq, slab), 2) // hd
    for g in range(groups):
        acc = jnp.zeros((nseq, seq, slab), F32)
        for j in range(kvh):
            p = g * kvh + j
            acc = acc + jnp.where(lane_head3 == j, pv[:, p * seq:(p + 1) * seq, :], 0.0)
        gate = _sigmoid(_wide(gb_ref, g * spb, spb))
        o_ref[:, g * slab:(g + 1) * slab] = (acc.reshape(rows, slab) * gate).astype(BF16)


def _swa_sample(proj, cache_k, cache_v, qw, kw, bd, bias, sink_col, kv_prev, *, layer, row0, batch, seq, kvh, hd,
                nq, col, nseq):
    rows = nseq * seq
    slab = kvh * hd
    depth, _, window, _ = cache_k.shape
    r0 = row0 // rows
    cache_blk = pl.BlockSpec((None, nseq, window, slab), lambda b: (layer, b, 0, 0))
    extra_in = [] if kv_prev is None else list(kv_prev)
    qb, sb = nq // LANES, slab // LANES
    cur = lambda name, n: pl.BlockSpec((n, rows, LANES), lambda b, o=col[name] // n: (o, r0 + b, 0))
    const2 = lambda b: (0, 0)
    return pl.pallas_call(
        functools.partial(_swa_sample_kernel, kvh=kvh, hd=hd, seq=seq),
        grid=(batch // nseq,),
        in_specs=[
            cur("sq", qb), cur("sk", sb), cur("sv", sb), cur("gb", qb),
            cache_blk, cache_blk,
            pl.BlockSpec((1, slab), const2),
            pl.BlockSpec((1, slab), const2),
            pl.BlockSpec((slab, slab), const2),
            pl.BlockSpec(bias.shape, const2),
            pl.BlockSpec(sink_col.shape, const2),
        ] + [pl.BlockSpec(memory_space=pl.ANY)] * len(extra_in),
        out_specs=[
            pl.BlockSpec((rows, nq), lambda b: (b, 0)),
            cache_blk, cache_blk,
        ],
        out_shape=[
            jax.ShapeDtypeStruct((batch * seq, nq), BF16),
            jax.ShapeDtypeStruct((depth, batch, window, slab), F32),
            jax.ShapeDtypeStruct((depth, batch, window, slab), F32),
        ],
        input_output_aliases={11: 1, 12: 2} if extra_in else {},
        compiler_params=pltpu.CompilerParams(
            dimension_semantics=("parallel",), vmem_limit_bytes=VMEM_LIMIT),
        name="swa_sample",
    )(proj, proj, proj, proj, cache_k, cache_v, qw, kw, bd, bias, sink_col, *extra_in)


def _dense_kernel(h_ref, oap_ref, oas_ref, obp_ref, obs_ref, pp_ref, ps_ref, wo_ref, wop_ref, nf_ref, wu_ref, wd_ref,
                  np_ref, wg_ref, wp_ref, out_ref, *, npt, ff_chunk):
    is_prompt = pl.program_id(0) < npt
    oa = jnp.where(is_prompt, oap_ref[...], oas_ref[...])
    ob = jnp.where(is_prompt, obp_ref[...], obs_ref[...])
    h = (h_ref[...] + jnp.dot(oa, wo_ref[...], preferred_element_type=F32)
         + jnp.dot(ob, wop_ref[...], preferred_element_type=F32))
    xn = (_rms(h) * nf_ref[...]).astype(BF16)
    acc = jnp.zeros_like(h)
    for c in range(wu_ref.shape[1] // ff_chunk):
        sl = slice(c * ff_chunk, (c + 1) * ff_chunk)
        hid = jnp.maximum(jnp.dot(xn, wu_ref[:, sl], preferred_element_type=F32), 0.0)
        acc = acc + jnp.dot((hid * hid).astype(BF16), wd_ref[sl, :], preferred_element_type=F32)
    h = h + acc
    xn = (_rms(h) * np_ref[...]).astype(BF16)
    gate = _sigmoid(jnp.dot(xn, wg_ref[...], preferred_element_type=F32))
    p = jnp.where(is_prompt, pp_ref[...], ps_ref[...])
    pe = jnp.dot(p.astype(BF16), wp_ref[...], preferred_element_type=F32)
    out_ref[...] = h + gate * pe


def _dense(h, oa_p, oa_s, ob_p, ob_s, p_p, p_s, wo, wop, nf, wu, wd, npl, wg, wp, *, layer, n_prompt_rows, tm):
    t_all, d = h.shape
    npt = n_prompt_rows // tm
    rows = lambda m: (m, 0)
    rows_p = lambda m: (jnp.minimum(m, npt - 1), 0)
    rows_s = lambda m: (jnp.maximum(m - npt, 0), 0)
    resident = lambda a: pl.BlockSpec(a.shape, lambda m: (0, 0), pipeline_mode=pl.Buffered(1))
    return pl.pallas_call(
        functools.partial(_dense_kernel, npt=npt, ff_chunk=1024),
        grid=(t_all // tm,),
        in_specs=[
            pl.BlockSpec((tm, d), rows),
            pl.BlockSpec((tm, d), rows_p), pl.BlockSpec((tm, d), rows_s),
            pl.BlockSpec((tm, d), rows_p), pl.BlockSpec((tm, d), rows_s),
            pl.BlockSpec((None, tm, p_p.shape[2]), lambda m: (layer, jnp.minimum(m, npt - 1), 0)),
            pl.BlockSpec((None, tm, p_s.shape[2]), lambda m: (layer, jnp.maximum(m - npt, 0), 0)),
            resident(wo), resident(wop), resident(nf), resident(wu), resident(wd),
            resident(npl), resident(wg), resident(wp),
        ],
        out_specs=pl.BlockSpec((tm, d), rows),
        out_shape=jax.ShapeDtypeStruct((t_all, d), F32),
        compiler_params=pltpu.CompilerParams(
            dimension_semantics=("parallel",), vmem_limit_bytes=VMEM_LIMIT),
        name="dense",
    )(h, oa_p, oa_s, ob_p, ob_s, p_p, p_s, wo, wop, nf, wu, wd, npl, wg, wp)


def _seq_mask(seq_len):
    i = np.arange(GDN_CHUNK)
    m = (i[:, None] // seq_len == i[None, :] // seq_len) & (i[None, :] <= i[:, None])
    return m.astype(np.float32)


def _tail_rows(proj, blocks, starts, n):
    b0, b1 = blocks
    parts = [lax.slice(proj, (b0, s, 0), (b1, s + n, proj.shape[2])) for s in starts]
    x = jnp.stack(parts)
    return jnp.transpose(x, (0, 2, 1, 3)).reshape(len(starts), n, (b1 - b0) * proj.shape[2])


def kernel(x_prompt, x_sample, cache_conv, state_gdn, cache_swa_k, cache_swa_v, p_prompt, p_sample, norm_mix, w_in, conv_w, a_log, dt_bias, gdn_norm, q_norm, k_norm, attn_sinks, w_out, norm_ffn, w_up, w_down, norm_ple, w_ple_gate, w_ple_proj):
    batch, seq, d = x_prompt.shape
    dbatch, dseq, _ = x_sample.shape
    depth = w_in.shape[0]
    heads, dk, dv = state_gdn.shape[2:]
    window, kvh, hd = cache_swa_k.shape[2:]
    nq = d
    nheads = nq // hd
    groups = nheads // kvh
    slab = kvh * hd
    key_dim = heads * dk
    val_dim = heads * dv
    conv_dim = 2 * key_dim + val_dim
    width = conv_w.shape[1]
    tp, ts = batch * seq, dbatch * dseq
    assert dk == LANES and dv == LANES and val_dim == d and key_dim == d and slab % LANES == 0
    assert seq % window == 0 and window == GDN_CHUNK and GDN_CHUNK % dseq == 0 and dseq >= width - 1
    assert dseq == SUBLANES and ts % GDN_CHUNK == 0

    o_z = conv_dim
    o_b = o_z + val_dim
    o_a = o_b + heads
    o_sq = o_a + heads
    o_sk = o_sq + nq
    o_sv = o_sk + slab
    o_g = o_sv + slab
    perm = np.concatenate([np.arange(hd) + (j * groups + g) * hd for g in range(groups) for j in range(kvh)])
    col_h = {"q": 0, "k": 1, "v": 2, "z": 3, "ga": 5}
    off = {"sq": conv_dim + val_dim, "gb": conv_dim + val_dim + nq + d,
           "sk": conv_dim + val_dim + nq + 2 * d, "sv": conv_dim + val_dim + nq + 2 * d + slab}
    col_b = {name: o // LANES for name, o in off.items()}
    cb = conv_dim // LANES

    h = jnp.concatenate([x_prompt.reshape(tp, d), x_sample.reshape(ts, d)], axis=0)
    p_p = p_prompt.reshape(depth, tp, -1)
    p_s = p_sample.reshape(depth, ts, -1)

    masks = jnp.asarray(np.stack([_seq_mask(GDN_CHUNK), _seq_mask(dseq)]), BF16)
    mincl_p = jnp.asarray(_seq_mask(GDN_CHUNK), F32)
    mincl_s = jnp.asarray(_seq_mask(dseq), F32)
    bd = jnp.asarray(np.kron(np.eye(kvh), np.full((hd, hd), 1.0 / hd)), BF16)

    head_of = np.array([(p % kvh) * groups + p // kvh for p in range(nheads)])
    slopes = jnp.exp2(-8.0 * (jnp.asarray(head_of, F32) + 1.0) / nheads)
    qi = np.arange(window)[:, None]
    kj = np.arange(2 * window)[None, :]
    dist = window + qi - kj
    ok = (dist >= 0) & (dist <= window)
    ok_first = ok & (kj >= window)
    dist_f = jnp.asarray(dist, F32)
    bias_p = jnp.stack([
        jnp.where(jnp.asarray(okv)[None], -slopes[:, None, None] * dist_f[None], -jnp.inf) for okv in (ok_first, ok)])
    nkeys = -(-(window + dseq) // 16) * 16
    ti = np.tile(np.arange(dseq), nheads)[:, None]
    sj = np.arange(nkeys)[None, :]
    dist_s = ti + window - sj
    ok_s = (dist_s >= 0) & (dist_s <= window) & (sj < window + dseq)
    slopes_rows = jnp.repeat(slopes, dseq)[:, None]
    bias_s = jnp.where(jnp.asarray(ok_s), -slopes_rows * jnp.asarray(dist_s, F32), -jnp.inf)

    hist_all = jnp.pad(cache_conv, ((0, 0), (0, 0), (dseq - cache_conv.shape[2], 0), (0, 0)))
    hist_all = hist_all.reshape(depth, ts, conv_dim)
    ck_all = cache_swa_k.reshape(depth, dbatch, window, slab)
    cv_all = cache_swa_v.reshape(depth, dbatch, window, slab)
    cw_all = jnp.transpose(conv_w.reshape(depth, width, cb, LANES), (0, 2, 1, 3))

    outs = {n: [] for n in ("conv_p", "gdn_p", "k_p", "v_p", "conv_s")}
    gdn_s = None
    kv_s = None
    p_tail = [b * seq + seq - (width - 1) for b in range(batch)]
    p_win = [b * seq + seq - window for b in range(batch)]
    for i in range(depth):
        wi = w_in[i]
        w_main = jnp.concatenate([
            wi[:, :o_b], wi[:, o_sq:o_sk][:, perm], wi[:, o_g:o_g + d], wi[:, o_g + d:][:, perm],
            wi[:, o_sk:o_g]], axis=1).astype(BF16)
        w_ba = jnp.pad(wi[:, o_b:o_sq], ((0, 0), (0, LANES - 2 * heads))).astype(BF16)
        avec = jnp.zeros((2, LANES), F32).at[0, heads:2 * heads].set(a_log[i]).at[1, heads:2 * heads].set(dt_bias[i])
        proj, bgc = _inproj(h, norm_mix[i][None], w_main, w_ba, avec, masks,
                            n_prompt_rows=tp, heads=heads, tm=1024, tn=1536)

        gn = gdn_norm[i][None]
        oa_p, gdn_p = _gdn_prompt(proj, bgc, cw_all[i], gn, mincl_p, batch=batch, seq=seq, heads=heads, dk=dk,
                                  col=col_h, rows=512, heads_per_iter=4)
        oa_s, gdn_s = _gdn_sample(proj, bgc, hist_all, state_gdn, cw_all[i], gn, mincl_s, gdn_s, layer=i, row0=tp,
                                  batch=dbatch, seq=dseq, heads=heads, dk=dk, col=col_h)

        sinks = attn_sinks[i][head_of]
        qw = jnp.tile(q_norm[i], kvh)[None]
        kw = jnp.tile(k_norm[i], kvh)[None]
        ob_p, k_p = _swa_prompt(proj, sinks, qw, kw, bd, bias_p, batch=batch, seq=seq, window=window, kvh=kvh,
                                hd=hd, nq=nq, col=col_b)
        ob_s, *kv_s = _swa_sample(proj, ck_all, cv_all, qw, kw, bd, bias_s, jnp.repeat(sinks, dseq)[:, None], kv_s,
                                  layer=i, row0=tp, batch=dbatch, seq=dseq, kvh=kvh, hd=hd, nq=nq, col=col_b, nseq=8)

        h = _dense(h, oa_p, oa_s, ob_p, ob_s, p_p, p_s, w_out[i].astype(BF16), w_out[i][perm].astype(BF16),
                   norm_ffn[i][None], w_up[i].astype(BF16), w_down[i].astype(BF16), norm_ple[i][None],
                   w_ple_gate[i].astype(BF16), w_ple_proj[i].astype(BF16), layer=i, n_prompt_rows=tp, tm=512)

        outs["conv_p"].append(_tail_rows(proj, (0, cb), p_tail, width - 1))
        conv_s = lax.slice(proj, (0, tp, 0), (cb, tp + ts, LANES)).reshape(cb, dbatch, dseq, LANES)
        outs["conv_s"].append(
            jnp.transpose(conv_s[:, :, dseq - (width - 1):], (1, 2, 0, 3)).reshape(dbatch, width - 1, conv_dim))
        outs["gdn_p"].append(gdn_p)
        outs["k_p"].append(k_p.reshape(batch, window, kvh, hd))
        v_blocks = (col_b["sv"], col_b["sv"] + slab // LANES)
        outs["v_p"].append(_tail_rows(proj, v_blocks, p_win, window).reshape(batch, window, kvh, hd))

    st = lambda n: jnp.stack(outs[n])
    return (h[:tp].reshape(batch, seq, d), h[tp:].reshape(dbatch, dseq, d),
            st("conv_p"), st("gdn_p"), st("k_p"), st("v_p"), st("conv_s"), gdn_s,
            kv_s[0].reshape(depth, dbatch, window, kvh, hd), kv_s[1].reshape(depth, dbatch, window, kvh, hd))
```

```python
import functools

import numpy as np
import jax
import jax.numpy as jnp
from jax import lax
from jax.experimental import pallas as pl
from jax.experimental.pallas import tpu as pltpu

F32 = jnp.float32
BF16 = jnp.bfloat16
EPS = 1e-6
LANES = 128
SUBLANES = 8
MXU_N = 256
GDN_CHUNK = 128
VMEM_LIMIT = 56 * 1024 * 1024


def _sigmoid(x):
    return 0.5 * jnp.tanh(0.5 * x) + 0.5


def _silu(x):
    hx = 0.5 * x
    return hx + hx * jnp.tanh(hx)


def _dot(a, b):
    return jnp.dot(a.astype(BF16), b.astype(BF16), preferred_element_type=F32)


def _dot_nt(a, b):
    return lax.dot_general(a.astype(BF16), b.astype(BF16), (((1,), (1,)), ((), ())), preferred_element_type=F32)


def _dot_tn(a, b):
    return lax.dot_general(a.astype(BF16), b.astype(BF16), (((0,), (0,)), ((), ())), preferred_element_type=F32)


def _rms(x):
    return x * lax.rsqrt(jnp.mean(x * x, axis=-1, keepdims=True) + EPS)


def _l2(x):
    return x * lax.rsqrt(jnp.sum(x * x, axis=-1, keepdims=True) + EPS)


def _inproj_kernel(xp_ref, xs_ref, nw_ref, w_ref, wba_ref, avec_ref, mask_ref, proj_ref, bgc_ref, xn_ref, *,
                   heads, npt):
    n = pl.program_id(1)

    @pl.when(n == 0)
    def _():
        x = jnp.where(pl.program_id(0) < npt, xp_ref[...], xs_ref[...])
        xn = (_rms(x) * nw_ref[...]).astype(BF16)
        xn_ref[...] = xn
        ba = jnp.dot(xn, wba_ref[...], preferred_element_type=F32)
        beta = _sigmoid(ba)
        xs = ba + avec_ref[1:2, :]
        softplus = jnp.maximum(xs, 0.0) + jnp.log1p(jnp.exp(-jnp.abs(xs)))
        g = -jnp.exp(avec_ref[0:1, :]) * softplus
        g1 = g.astype(BF16)
        r1 = g - g1.astype(F32)
        g2 = r1.astype(BF16)
        g3 = (r1 - g2.astype(F32)).astype(BF16)
        m = mask_ref[...]
        rows = g.shape[0]
        lane = lax.broadcasted_iota(jnp.int32, (GDN_CHUNK, LANES), 1)
        for r in range(rows // GDN_CHUNK):
            sl = slice(r * GDN_CHUNK, (r + 1) * GDN_CHUNK)
            gc = (jnp.dot(m, g1[sl], preferred_element_type=F32)
                  + jnp.dot(m, g2[sl], preferred_element_type=F32)
                  + jnp.dot(m, g3[sl], preferred_element_type=F32))
            bgc_ref[sl, :] = jnp.where(lane < heads, beta[sl], gc)

    xn = xn_ref[...]
    for j in range(w_ref.shape[1] // MXU_N):
        r = jnp.dot(xn, w_ref[:, j * MXU_N:(j + 1) * MXU_N], preferred_element_type=F32)
        for i in range(MXU_N // LANES):
            proj_ref[j * (MXU_N // LANES) + i] = r[:, i * LANES:(i + 1) * LANES]


def _inproj(h_p, h_s, nw, w, wba, avec, masks, *, heads, tm, tn):
    d = h_p.shape[1]
    t_all = h_p.shape[0] + h_s.shape[0]
    n_out = w.shape[1]
    npt = h_p.shape[0] // tm
    return pl.pallas_call(
        functools.partial(_inproj_kernel, heads=heads, npt=npt),
        grid=(t_all // tm, n_out // tn),
        in_specs=[
            pl.BlockSpec((tm, d), lambda m, n: (jnp.minimum(m, npt - 1), 0)),
            pl.BlockSpec((tm, d), lambda m, n: (jnp.maximum(m - npt, 0), 0)),
            pl.BlockSpec((1, d), lambda m, n: (0, 0)),
            pl.BlockSpec((d, tn), lambda m, n: (0, n)),
            pl.BlockSpec((d, LANES), lambda m, n: (0, 0)),
            pl.BlockSpec((2, LANES), lambda m, n: (0, 0)),
            pl.BlockSpec((None, GDN_CHUNK, GDN_CHUNK), lambda m, n: (jnp.where(m >= npt, 1, 0), 0, 0)),
        ],
        out_specs=[
            pl.BlockSpec((tn // LANES, tm, LANES), lambda m, n: (n, m, 0)),
            pl.BlockSpec((tm, LANES), lambda m, n: (m, 0)),
        ],
        out_shape=[
            jax.ShapeDtypeStruct((n_out // LANES, t_all, LANES), F32),
            jax.ShapeDtypeStruct((t_all, LANES), F32),
        ],
        scratch_shapes=[pltpu.VMEM((tm, d), BF16)],
        compiler_params=pltpu.CompilerParams(
            dimension_semantics=("parallel", "arbitrary"), vmem_limit_bytes=VMEM_LIMIT),
        name="inproj",
    )(h_p, h_s, nw, w, wba, avec, masks)


def _gdn_chunks_local(qs, ks, vs, betas, gcs, mincl, seq_len):
    n = len(qs)
    c = qs[0].shape[0]
    dv = vs[0].shape[1]
    row = lax.broadcasted_iota(jnp.int32, (c, c), 0)
    col = lax.broadcasted_iota(jnp.int32, (c, c), 1)
    decs, egs = [], []
    for gc in gcs:
        gcb = jnp.broadcast_to(gc, (c, c))
        decs.append(jnp.exp(jnp.where(mincl > 0, gcb - gcb.T, -jnp.inf)))
        egs.append(jnp.exp(gc))
    kbs = [ks[i] * betas[i] for i in range(n)]
    kks = [_dot_nt(jnp.concatenate([kbs[i], qs[i]], axis=0), ks[i]) for i in range(n)]
    a_s = [jnp.where(row == col, 0.0, kks[i][:c] * decs[i]) for i in range(n)]
    qks = [kks[i][c:] * decs[i] for i in range(n)]
    same = (row >> 1) == (col >> 1)
    eye = jnp.where(row == col, 1.0, 0.0)
    ts = [eye - jnp.where(same, a, 0.0) for a in a_s]
    na16 = [(-a).astype(BF16) for a in a_s]
    for lvl in range(1, seq_len.bit_length() - 1):
        wider = (row >> (lvl + 1)) == (col >> (lvl + 1))
        sel = wider & jnp.logical_not(same)
        t16 = [t.astype(BF16) for t in ts]
        mids = [jnp.dot(na16[i], t16[i], preferred_element_type=F32) for i in range(n)]
        ts = [jnp.where(sel, jnp.dot(t16[i], mids[i].astype(BF16), preferred_element_type=F32), ts[i])
              for i in range(n)]
        same = wider
    sols = [_dot(ts[i], jnp.concatenate([vs[i] * betas[i], kbs[i] * egs[i]], axis=1)) for i in range(n)]
    return [s[:, :dv] for s in sols], [s[:, dv:] for s in sols], qks, egs


def _conv_taps(u, ru_fn, cw):
    width = cw.shape[0]
    out = u * cw[width - 1:width, :]
    for j in range(1, width):
        out = out + ru_fn(j) * cw[width - 1 - j:width - j, :]
    return _silu(out)


def _gdn_finish(o, z, ga, gn):
    return (_rms(o) * gn * _silu(z) * _sigmoid(ga)).astype(BF16)


def _head_scalars(bgc, h, heads):
    lane = lax.broadcasted_iota(jnp.int32, bgc.shape, 1)
    beta = jnp.sum(jnp.where(lane == h, bgc, 0.0), axis=-1, keepdims=True)
    gc = jnp.sum(jnp.where(lane == h + heads, bgc, 0.0), axis=-1, keepdims=True)
    return beta, gc


def _gdn_prompt_kernel(q_ref, k_ref, v_ref, z_ref, ga_ref, bgc_ref, cw_ref, gn_ref, mincl_ref,
                       o_ref, sfin_ref,
                       s_ref, tail_ref, u_s, wqe_s, qk_s, kd_s, el_s, *, heads_per_iter):
    c_id = pl.program_id(1)
    heads, rows, dk = q_ref.shape
    nchunk = rows // GDN_CHUNK
    cc = GDN_CHUNK

    @pl.when(c_id == 0)
    def _():
        s_ref[...] = jnp.zeros_like(s_ref)
        tail_ref[...] = jnp.zeros_like(tail_ref)

    row8 = lax.broadcasted_iota(jnp.int32, (SUBLANES, dk), 0)
    bgc = bgc_ref[...]
    mincl = mincl_ref[...]

    def conv(kind, h, u_ref):
        u = u_ref[h]
        tail = tail_ref[kind * heads + h]

        def delayed(j):
            head = jnp.where(row8 < j, pltpu.roll(tail, j, 0), pltpu.roll(u[:SUBLANES], j, 0))
            return jnp.concatenate([head, u_ref[h, pl.ds(SUBLANES - j, rows - SUBLANES), :]], axis=0)

        out = _conv_taps(u, delayed, cw_ref[kind * heads + h])
        tail_ref[kind * heads + h] = u[rows - SUBLANES:]
        return out

    def local_body(it, carry):
        hs, qs, ks, vs, betas, gcs = [], [], [], [], [], []
        for hh in range(heads_per_iter):
            h = it * heads_per_iter + hh
            q = _l2(conv(0, h, q_ref)) * (dk ** -0.5)
            k = _l2(conv(1, h, k_ref))
            v = conv(2, h, v_ref)
            beta, gc = _head_scalars(bgc, h, heads)
            for ci in range(nchunk):
                sl = slice(ci * cc, (ci + 1) * cc)
                hs.append((h, ci))
                qs.append(q[sl]); ks.append(k[sl]); vs.append(v[sl]); betas.append(beta[sl]); gcs.append(gc[sl])
        us, ws, qks, egs = _gdn_chunks_local(qs, ks, vs, betas, gcs, mincl, cc)
        for i, (h, ci) in enumerate(hs):
            g_last = gcs[i][cc - 1:cc, :]
            u_s[h, ci] = us[i]
            wqe_s[h, ci, :cc] = ws[i].astype(BF16)
            wqe_s[h, ci, cc:] = (qs[i] * egs[i]).astype(BF16)
            qk_s[h, ci] = qks[i].astype(BF16)
            kd_s[h, ci] = (ks[i] * jnp.exp(g_last - gcs[i])).astype(BF16)
            el_s[h, ci] = jnp.broadcast_to(jnp.exp(g_last), (SUBLANES, dk))
        return carry

    lax.fori_loop(0, heads // heads_per_iter, local_body, 0)

    states = [s_ref[h] for h in range(heads)]
    gn = gn_ref[...]
    for ci in range(nchunk):
        sl = slice(ci * cc, (ci + 1) * cc)
        wqs = [jnp.dot(wqe_s[h, ci], states[h].astype(BF16), preferred_element_type=F32) for h in range(heads)]
        v_news = [u_s[h, ci] - wqs[h][:cc] for h in range(heads)]
        vn16 = [v.astype(BF16) for v in v_news]
        outs = [wqs[h][cc:] + jnp.dot(qk_s[h, ci], vn16[h], preferred_element_type=F32) for h in range(heads)]
        states = [states[h] * el_s[h, ci][0:1, :]
                  + lax.dot_general(kd_s[h, ci], vn16[h], (((0,), (0,)), ((), ())), preferred_element_type=F32)
                  for h in range(heads)]
        for h in range(heads):
            o_ref[sl, h * dk:(h + 1) * dk] = _gdn_finish(outs[h], z_ref[h, sl, :], ga_ref[h, sl, :], gn)
    for h in range(heads):
        s_ref[h] = states[h]
        sfin_ref[h] = states[h]


def _gdn_prompt(proj, bgc, cw, gn, mincl, *, batch, seq, heads, dk, col, rows, heads_per_iter):
    ncb = seq // rows
    nchunk = rows // GDN_CHUNK
    blk = lambda name: pl.BlockSpec((heads, rows, dk), lambda b, c, o=col[name]: (o, b * ncb + c, 0))
    return pl.pallas_call(
        functools.partial(_gdn_prompt_kernel, heads_per_iter=heads_per_iter),
        grid=(batch, ncb),
        in_specs=[
            blk("q"), blk("k"), blk("v"), blk("z"), blk("ga"),
            pl.BlockSpec((rows, LANES), lambda b, c: (b * ncb + c, 0)),
            pl.BlockSpec(cw.shape, lambda b, c: (0, 0, 0)),
            pl.BlockSpec((1, dk), lambda b, c: (0, 0)),
            pl.BlockSpec((GDN_CHUNK, GDN_CHUNK), lambda b, c: (0, 0)),
        ],
        out_specs=[
            pl.BlockSpec((rows, heads * dk), lambda b, c: (b * ncb + c, 0)),
            pl.BlockSpec((None, heads, dk, dk), lambda b, c: (b, 0, 0, 0)),
        ],
        out_shape=[
            jax.ShapeDtypeStruct((batch * seq, heads * dk), BF16),
            jax.ShapeDtypeStruct((batch, heads, dk, dk), F32),
        ],
        scratch_shapes=[
            pltpu.VMEM((heads, dk, dk), F32),
            pltpu.VMEM((3 * heads, SUBLANES, dk), F32),
            pltpu.VMEM((heads, nchunk, GDN_CHUNK, dk), F32),
            pltpu.VMEM((heads, nchunk, 2 * GDN_CHUNK, dk), BF16),
            pltpu.VMEM((heads, nchunk, GDN_CHUNK, GDN_CHUNK), BF16),
            pltpu.VMEM((heads, nchunk, GDN_CHUNK, dk), BF16),
            pltpu.VMEM((heads, nchunk, SUBLANES, dk), F32),
        ],
        compiler_params=pltpu.CompilerParams(
            dimension_semantics=("parallel", "arbitrary"), vmem_limit_bytes=VMEM_LIMIT),
        name="gdn_prompt",
    )(proj, proj, proj, proj, proj, bgc, cw, gn, mincl)


def _gdn_sample_kernel(q_ref, k_ref, v_ref, z_ref, ga_ref, bgc_ref, hq_ref, hk_ref, hv_ref,
                       cw_ref, gn_ref, mincl_ref, s0_ref, *rest, heads, seq):
    o_ref, sout_ref = rest[-2:]
    h = pl.program_id(1)
    rows, dk = q_ref.shape
    nseq = rows // seq
    t_idx = lax.broadcasted_iota(jnp.int32, (nseq, seq, dk), 1)

    def conv(kind, u_ref, hist_ref):
        u = u_ref[...]
        u3 = u.reshape(nseq, seq, dk)
        hist3 = hist_ref[...].reshape(nseq, seq, dk)

        def delayed(j):
            xj = jnp.where(t_idx < j, pltpu.roll(hist3, j, 1), pltpu.roll(u3, j, 1))
            return xj.reshape(rows, dk)

        return _conv_taps(u, delayed, cw_ref[kind * heads + h])

    q = _l2(conv(0, q_ref, hq_ref)) * (dk ** -0.5)
    k = _l2(conv(1, k_ref, hk_ref))
    v = conv(2, v_ref, hv_ref)
    beta, gc = _head_scalars(bgc_ref[...], h, heads)

    (u,), (w,), (qk,), (eg,) = _gdn_chunks_local([q], [k], [v], [beta], [gc], mincl_ref[...], seq)
    qe = q * eg
    pad = jnp.zeros((seq, dk), F32)
    sls = [slice(si * seq, (si + 1) * seq) for si in range(nseq)]
    g_lasts = [gc[(si + 1) * seq - 1:(si + 1) * seq, :] for si in range(nseq)]
    s0 = [s0_ref[si] for si in range(nseq)]
    wqs = [_dot(jnp.concatenate([w[sl], qe[sl]], axis=0), s0[si]) for si, sl in enumerate(sls)]
    v_news = [u[sl] - wqs[si][:seq] for si, sl in enumerate(sls)]
    for si, sl in enumerate(sls):
        kd = k[sl] * jnp.exp(g_lasts[si] - gc[sl])
        sout_ref[si] = s0[si] * jnp.exp(g_lasts[si]) + _dot_tn(jnp.concatenate([kd, pad], axis=0),
                                                               jnp.concatenate([v_news[si], pad], axis=0))
    o = jnp.concatenate([wq[seq:] for wq in wqs], axis=0) + _dot(qk, jnp.concatenate(v_news, axis=0))
    o_ref[...] = _gdn_finish(o, z_ref[...], ga_ref[...], gn_ref[...])


def _gdn_sample(proj, bgc, hist, s0, cw, gn, mincl, s_prev, *, layer, row0, batch, seq, heads, dk, col):
    rows = GDN_CHUNK
    nseq = rows // seq
    r0 = row0 // rows
    depth = s0.shape[0]
    blk = lambda name: pl.BlockSpec((None, rows, dk), lambda b, h, o=col[name]: (o * heads + h, r0 + b, 0))
    hblk = lambda name: pl.BlockSpec((None, rows, dk), lambda b, h, o=col[name]: (layer, b, o * heads + h))
    state_blk = pl.BlockSpec((None, nseq, None, dk, dk), lambda b, h: (layer, b, h, 0, 0))
    extra_in = [] if s_prev is None else [s_prev]
    return pl.pallas_call(
        functools.partial(_gdn_sample_kernel, heads=heads, seq=seq),
        grid=(batch * seq // rows, heads),
        in_specs=[
            blk("q"), blk("k"), blk("v"), blk("z"), blk("ga"),
            pl.BlockSpec((rows, LANES), lambda b, h: (r0 + b, 0)),
            hblk("q"), hblk("k"), hblk("v"),
            pl.BlockSpec(cw.shape, lambda b, h: (0, 0, 0)),
            pl.BlockSpec((1, dk), lambda b, h: (0, 0)),
            pl.BlockSpec((GDN_CHUNK, GDN_CHUNK), lambda b, h: (0, 0)),
            state_blk,
        ] + [pl.BlockSpec(memory_space=pl.ANY)] * len(extra_in),
        out_specs=[
            pl.BlockSpec((rows, dk), lambda b, h: (b, h)),
            state_blk,
        ],
        out_shape=[
            jax.ShapeDtypeStruct((batch * seq, heads * dk), BF16),
            jax.ShapeDtypeStruct((depth, batch, heads, dk, dk), F32),
        ],
        input_output_aliases={13: 1} if extra_in else {},
        compiler_params=pltpu.CompilerParams(
            dimension_semantics=("parallel", "parallel"), vmem_limit_bytes=VMEM_LIMIT),
        name="gdn_sample",
    )(proj, proj, proj, proj, proj, bgc, hist, hist, hist, cw, gn, mincl, s0, *extra_in)


def _chunk_rms(x, bd, w):
    ms = _dot(x * x, bd)
    return x * lax.rsqrt(ms + EPS) * w


def _wide(ref, first, n):
    return jnp.concatenate([ref[first + i] for i in range(n)], axis=-1)


def _swa_prompt_kernel(sink_ref, q_ref, kp_ref, kc_ref, vp_ref, vc_ref, gb_ref, qw_ref, kw_ref, bd_ref, bias_ref,
                       o_ref, kout_ref, *, kvh, hd):
    blk = q_ref.shape[1]
    slab = kvh * hd
    spb = slab // LANES
    groups = q_ref.shape[0] // spb
    bd = bd_ref[...]
    kn = _chunk_rms(_wide(kc_ref, 0, spb), bd, kw_ref[...])
    kout_ref[...] = kn
    kband = jnp.concatenate([_chunk_rms(_wide(kp_ref, 0, spb), bd, kw_ref[...]), kn], axis=0).astype(BF16)
    vband = jnp.concatenate([_wide(vp_ref, 0, spb), _wide(vc_ref, 0, spb)], axis=0).astype(BF16)
    lane_head = lax.broadcasted_iota(jnp.int32, (blk, slab), 1) // hd

    def scores(g):
        qn = _chunk_rms(_wide(q_ref, g * spb, spb), bd, qw_ref[...]) * (hd ** -0.5)
        qs = jnp.concatenate([jnp.where(lane_head == j, qn, 0.0) for j in range(kvh)], axis=0).astype(BF16)
        return lax.dot_general(qs, kband, (((1,), (1,)), ((), ())), preferred_element_type=F32)

    s_next = scores(0)
    for g in range(groups):
        s_all = s_next
        if g + 1 < groups:
            s_next = scores(g + 1)
        probs = []
        for j in range(kvh):
            p = g * kvh + j
            s = s_all[j * blk:(j + 1) * blk] + bias_ref[p]
            sink = sink_ref[p]
            m = jnp.maximum(jnp.max(s, axis=-1, keepdims=True), sink)
            e = jnp.exp(s - m)
            den = jnp.sum(e, axis=-1, keepdims=True) + jnp.exp(sink - m)
            probs.append((e * (1.0 / den)).astype(BF16))
        pv = jnp.dot(jnp.concatenate(probs, axis=0), vband, preferred_element_type=F32)
        acc = pv[(kvh - 1) * blk:]
        for j in range(kvh - 2, -1, -1):
            acc = jnp.where(lane_head == j, pv[j * blk:(j + 1) * blk], acc)
        gate = _sigmoid(_wide(gb_ref, g * spb, spb))
        o_ref[:, g * slab:(g + 1) * slab] = (acc * gate).astype(BF16)


def _swa_prompt(proj, sinks, qw, kw, bd, bias, *, batch, seq, window, kvh, hd, nq, col):
    nb = seq // window
    slab = kvh * hd
    nheads = nq // hd
    qb, sb = nq // LANES, slab // LANES
    cur = lambda name, n: pl.BlockSpec((n, window, LANES), lambda b, c, o=col[name] // n: (o, b * nb + c, 0))
    prev = lambda name, n: pl.BlockSpec(
        (n, window, LANES), lambda b, c, o=col[name] // n: (o, b * nb + jnp.maximum(c - 1, 0), 0))
    const2 = lambda b, c: (0, 0)
    return pl.pallas_call(
        functools.partial(_swa_prompt_kernel, kvh=kvh, hd=hd),
        grid=(batch, nb),
        in_specs=[
            pl.BlockSpec(memory_space=pltpu.SMEM),
            cur("sq", qb), prev("sk", sb), cur("sk", sb), prev("sv", sb), cur("sv", sb), cur("gb", qb),
            pl.BlockSpec((1, slab), const2),
            pl.BlockSpec((1, slab), const2),
            pl.BlockSpec((slab, slab), const2),
            pl.BlockSpec((None, nheads, window, 2 * window), lambda b, c: (jnp.minimum(c, 1), 0, 0, 0)),
        ],
        out_specs=[
            pl.BlockSpec((window, nq), lambda b, c: (b * nb + c, 0)),
            pl.BlockSpec((None, window, slab), lambda b, c: (b, 0, 0)),
        ],
        out_shape=[
            jax.ShapeDtypeStruct((batch * seq, nq), BF16),
            jax.ShapeDtypeStruct((batch, window, slab), F32),
        ],
        compiler_params=pltpu.CompilerParams(
            dimension_semantics=("parallel", "arbitrary"), vmem_limit_bytes=VMEM_LIMIT),
        name="swa_prompt",
    )(sinks, proj, proj, proj, proj, proj, proj, qw, kw, bd, bias)


def _swa_sample_kernel(q_ref, k_ref, v_ref, gb_ref, ck_ref, cv_ref, qw_ref, kw_ref, bd_ref, bias_ref, sink_ref,
                       *rest, kvh, hd, seq):
    o_ref, kout_ref, vout_ref = rest[-3:]
    rows = q_ref.shape[1]
    nseq = rows // seq
    slab = kvh * hd
    spb = slab // LANES
    groups = q_ref.shape[0] // spb
    window = ck_ref.shape[1]
    nkeys = bias_ref.shape[1]
    bd = bd_ref[...]
    kn = _chunk_rms(_wide(k_ref, 0, spb), bd, kw_ref[...]).reshape(nseq, seq, slab)
    vn = _wide(v_ref, 0, spb).reshape(nseq, seq, slab)
    ck = ck_ref[...]
    cv = cv_ref[...]
    kout_ref[...] = jnp.concatenate([ck[:, seq:, :], kn], axis=1)
    vout_ref[...] = jnp.concatenate([cv[:, seq:, :], vn], axis=1)
    zpad = jnp.zeros((nseq, nkeys - window - seq, slab), F32)
    k_all = jnp.concatenate([ck, kn, zpad], axis=1).astype(BF16)
    v_all = jnp.concatenate([cv, vn, zpad], axis=1).astype(BF16)

    lane_head = lax.broadcasted_iota(jnp.int32, (rows, slab), 1) // hd
    pieces = []
    for g in range(groups):
        qn = _chunk_rms(_wide(q_ref, g * spb, spb), bd, qw_ref[...]) * (hd ** -0.5)
        for j in range(kvh):
            pieces.append(jnp.where(lane_head == j, qn, 0.0).reshape(nseq, seq, slab))
    q_all = jnp.concatenate(pieces, axis=1).astype(BF16)
    s = jnp.einsum("bqd,bkd->bqk", q_all, k_all, preferred_element_type=F32) + bias_ref[...][None]
    sink = sink_ref[...][None]
    m = jnp.maximum(jnp.max(s, axis=-1, keepdims=True), sink)
    e = jnp.exp(s - m)
    den = jnp.sum(e, axis=-1, keepdims=True) + jnp.exp(sink - m)
    pv = jnp.einsum("bqk,bkd->bqd", (e * (1.0 / den)).astype(BF16), v_all, preferred_element_type=F32)
    lane_head3 = lax.broadcasted_iota(jnp.int32, (nseq, seq, slab), 2) // hd
    for g in range(groups):
        acc = jnp.zeros((nseq, seq, slab), F32)
        for j in range(kvh):
            p = g * kvh + j
            acc = acc + jnp.where(lane_head3 == j, pv[:, p * seq:(p + 1) * seq, :], 0.0)
        gate = _sigmoid(_wide(gb_ref, g * spb, spb))
        o_ref[:, g * slab:(g + 1) * slab] = (acc.reshape(rows, slab) * gate).astype(BF16)


def _swa_sample(proj, cache_k, cache_v, qw, kw, bd, bias, sink_col, kv_prev, *, layer, row0, batch, seq, kvh, hd,
                nq, col, nseq):
    rows = nseq * seq
    slab = kvh * hd
    depth, _, window, _ = cache_k.shape
    r0 = row0 // rows
    cache_blk = pl.BlockSpec((None, nseq, window, slab), lambda b: (layer, b, 0, 0))
    extra_in = [] if kv_prev is None else list(kv_prev)
    qb, sb = nq // LANES, slab // LANES
    cur = lambda name, n: pl.BlockSpec((n, rows, LANES), lambda b, o=col[name] // n: (o, r0 + b, 0))
    const2 = lambda b: (0, 0)
    return pl.pallas_call(
        functools.partial(_swa_sample_kernel, kvh=kvh, hd=hd, seq=seq),
        grid=(batch // nseq,),
        in_specs=[
            cur("sq", qb), cur("sk", sb), cur("sv", sb), cur("gb", qb),
            cache_blk, cache_blk,
            pl.BlockSpec((1, slab), const2),
            pl.BlockSpec((1, slab), const2),
            pl.BlockSpec((slab, slab), const2),
            pl.BlockSpec(bias.shape, const2),
            pl.BlockSpec(sink_col.shape, const2),
        ] + [pl.BlockSpec(memory_space=pl.ANY)] * len(extra_in),
        out_specs=[
            pl.BlockSpec((rows, nq), lambda b: (b, 0)),
            cache_blk, cache_blk,
        ],
        out_shape=[
            jax.ShapeDtypeStruct((batch * seq, nq), BF16),
            jax.ShapeDtypeStruct((depth, batch, window, slab), F32),
            jax.ShapeDtypeStruct((depth, batch, window, slab), F32),
        ],
        input_output_aliases={11: 1, 12: 2} if extra_in else {},
        compiler_params=pltpu.CompilerParams(
            dimension_semantics=("parallel",), vmem_limit_bytes=VMEM_LIMIT),
        name="swa_sample",
    )(proj, proj, proj, proj, cache_k, cache_v, qw, kw, bd, bias, sink_col, *extra_in)


def _dense_kernel(hp_ref, hs_ref, oap_ref, oas_ref, obp_ref, obs_ref, pp_ref, ps_ref, wo_ref, wop_ref, nf_ref,
                  wu_ref, wd_ref, np_ref, wg_ref, wp_ref, outp_ref, outs_ref, *, npt, ff_chunk):
    is_prompt = pl.program_id(0) < npt
    oa = jnp.where(is_prompt, oap_ref[...], oas_ref[...])
    ob = jnp.where(is_prompt, obp_ref[...], obs_ref[...])
    h = jnp.where(is_prompt, hp_ref[...], hs_ref[...])
    h = (h + jnp.dot(oa, wo_ref[...], preferred_element_type=F32)
         + jnp.dot(ob, wop_ref[...], preferred_element_type=F32))
    xn = (_rms(h) * nf_ref[...]).astype(BF16)
    acc = jnp.zeros_like(h)
    for c in range(wu_ref.shape[1] // ff_chunk):
        sl = slice(c * ff_chunk, (c + 1) * ff_chunk)
        hid = jnp.maximum(jnp.dot(xn, wu_ref[:, sl], preferred_element_type=F32), 0.0)
        acc = acc + jnp.dot((hid * hid).astype(BF16), wd_ref[sl, :], preferred_element_type=F32)
    h = h + acc
    xn = (_rms(h) * np_ref[...]).astype(BF16)
    gate = _sigmoid(jnp.dot(xn, wg_ref[...], preferred_element_type=F32))
    p = jnp.where(is_prompt, pp_ref[...], ps_ref[...])
    pe = jnp.dot(p.astype(BF16), wp_ref[...], preferred_element_type=F32)
    out = h + gate * pe

    @pl.when(is_prompt)
    def _():
        outp_ref[...] = out

    @pl.when(jnp.logical_not(is_prompt))
    def _():
        outs_ref[...] = out


def _dense(h_p, h_s, oa_p, oa_s, ob_p, ob_s, p_p, p_s, wo, wop, nf, wu, wd, npl, wg, wp, *, layer, tm):
    tp, d = h_p.shape
    ts = h_s.shape[0]
    npt = tp // tm
    rows_p = lambda m: (jnp.minimum(m, npt - 1), 0)
    rows_s = lambda m: (jnp.maximum(m - npt, 0), 0)
    resident = lambda a: pl.BlockSpec((None,) + a.shape[1:], lambda m: (layer, 0, 0), pipeline_mode=pl.Buffered(1))
    return pl.pallas_call(
        functools.partial(_dense_kernel, npt=npt, ff_chunk=1024),
        grid=((tp + ts) // tm,),
        in_specs=[
            pl.BlockSpec((tm, d), rows_p), pl.BlockSpec((tm, d), rows_s),
            pl.BlockSpec((tm, d), rows_p), pl.BlockSpec((tm, d), rows_s),
            pl.BlockSpec((tm, d), rows_p), pl.BlockSpec((tm, d), rows_s),
            pl.BlockSpec((None, tm, p_p.shape[2]), lambda m: (layer, jnp.minimum(m, npt - 1), 0)),
            pl.BlockSpec((None, tm, p_s.shape[2]), lambda m: (layer, jnp.maximum(m - npt, 0), 0)),
            resident(wo), resident(wop), resident(nf), resident(wu), resident(wd),
            resident(npl), resident(wg), resident(wp),
        ],
        out_specs=[pl.BlockSpec((tm, d), rows_p), pl.BlockSpec((tm, d), rows_s)],
        out_shape=[jax.ShapeDtypeStruct((tp, d), F32), jax.ShapeDtypeStruct((ts, d), F32)],
        compiler_params=pltpu.CompilerParams(
            dimension_semantics=("arbitrary",), vmem_limit_bytes=VMEM_LIMIT),
        name="dense",
    )(h_p, h_s, oa_p, oa_s, ob_p, ob_s, p_p, p_s, wo, wop, nf, wu, wd, npl, wg, wp)


def _seq_mask(seq_len):
    i = np.arange(GDN_CHUNK)
    m = (i[:, None] // seq_len == i[None, :] // seq_len) & (i[None, :] <= i[:, None])
    return m.astype(np.float32)


def _tail_rows(proj, blocks, starts, n):
    b0, b1 = blocks
    parts = [lax.slice(proj, (b0, s, 0), (b1, s + n, proj.shape[2])) for s in starts]
    x = jnp.stack(parts)
    return jnp.transpose(x, (0, 2, 1, 3)).reshape(len(starts), n, (b1 - b0) * proj.shape[2])


def kernel(x_prompt, x_sample, cache_conv, state_gdn, cache_swa_k, cache_swa_v, p_prompt, p_sample, norm_mix, w_in, conv_w, a_log, dt_bias, gdn_norm, q_norm, k_norm, attn_sinks, w_out, norm_ffn, w_up, w_down, norm_ple, w_ple_gate, w_ple_proj):
    batch, seq, d = x_prompt.shape
    dbatch, dseq, _ = x_sample.shape
    depth = w_in.shape[0]
    heads, dk, dv = state_gdn.shape[2:]
    window, kvh, hd = cache_swa_k.shape[2:]
    nq = d
    nheads = nq // hd
    groups = nheads // kvh
    slab = kvh * hd
    key_dim = heads * dk
    val_dim = heads * dv
    conv_dim = 2 * key_dim + val_dim
    width = conv_w.shape[1]
    tp, ts = batch * seq, dbatch * dseq
    assert dk == LANES and dv == LANES and val_dim == d and key_dim == d and slab % LANES == 0
    assert seq % window == 0 and window == GDN_CHUNK and GDN_CHUNK % dseq == 0 and dseq >= width - 1
    assert dseq == SUBLANES and ts % GDN_CHUNK == 0

    o_z = conv_dim
    o_b = o_z + val_dim
    o_a = o_b + heads
    o_sq = o_a + heads
    o_sk = o_sq + nq
    o_sv = o_sk + slab
    o_g = o_sv + slab
    perm = np.concatenate([np.arange(hd) + (j * groups + g) * hd for g in range(groups) for j in range(kvh)])
    col_h = {"q": 0, "k": 1, "v": 2, "z": 3, "ga": 5}
    off = {"sq": conv_dim + val_dim, "gb": conv_dim + val_dim + nq + d,
           "sk": conv_dim + val_dim + nq + 2 * d, "sv": conv_dim + val_dim + nq + 2 * d + slab}
    col_b = {name: o // LANES for name, o in off.items()}
    cb = conv_dim // LANES

    h_p = x_prompt.reshape(tp, d)
    h_s = x_sample.reshape(ts, d)
    bf = lambda a: a.astype(BF16)
    wo_all, wop_all, wu_all, wd_all = bf(w_out), bf(w_out[:, perm, :]), bf(w_up), bf(w_down)
    wg_all, wp_all = bf(w_ple_gate), bf(w_ple_proj)
    nf_all, np_all = norm_ffn[:, None, :], norm_ple[:, None, :]
    p_p = p_prompt.reshape(depth, tp, -1)
    p_s = p_sample.reshape(depth, ts, -1)

    masks = jnp.asarray(np.stack([_seq_mask(GDN_CHUNK), _seq_mask(dseq)]), BF16)
    mincl_p = jnp.asarray(_seq_mask(GDN_CHUNK), F32)
    mincl_s = jnp.asarray(_seq_mask(dseq), F32)
    bd = jnp.asarray(np.kron(np.eye(kvh), np.full((hd, hd), 1.0 / hd)), BF16)

    head_of = np.array([(p % kvh) * groups + p // kvh for p in range(nheads)])
    slopes = jnp.exp2(-8.0 * (jnp.asarray(head_of, F32) + 1.0) / nheads)
    qi = np.arange(window)[:, None]
    kj = np.arange(2 * window)[None, :]
    dist = window + qi - kj
    ok = (dist >= 0) & (dist <= window)
    ok_first = ok & (kj >= window)
    dist_f = jnp.asarray(dist, F32)
    bias_p = jnp.stack([
        jnp.where(jnp.asarray(okv)[None], -slopes[:, None, None] * dist_f[None], -jnp.inf) for okv in (ok_first, ok)])
    nkeys = -(-(window + dseq) // 16) * 16
    ti = np.tile(np.arange(dseq), nheads)[:, None]
    sj = np.arange(nkeys)[None, :]
    dist_s = ti + window - sj
    ok_s = (dist_s >= 0) & (dist_s <= window) & (sj < window + dseq)
    slopes_rows = jnp.repeat(slopes, dseq)[:, None]
    bias_s = jnp.where(jnp.asarray(ok_s), -slopes_rows * jnp.asarray(dist_s, F32), -jnp.inf)

    hist_all = jnp.pad(cache_conv, ((0, 0), (0, 0), (dseq - cache_conv.shape[2], 0), (0, 0)))
    hist_all = hist_all.reshape(depth, ts, conv_dim)
    ck_all = cache_swa_k.reshape(depth, dbatch, window, slab)
    cv_all = cache_swa_v.reshape(depth, dbatch, window, slab)
    cw_all = jnp.transpose(conv_w.reshape(depth, width, cb, LANES), (0, 2, 1, 3))

    outs = {n: [] for n in ("conv_p", "gdn_p", "k_p", "v_p", "conv_s")}
    gdn_s = None
    kv_s = None
    p_tail = [b * seq + seq - (width - 1) for b in range(batch)]
    p_win = [b * seq + seq - window for b in range(batch)]
    for i in range(depth):
        wi = w_in[i]
        w_main = jnp.concatenate([
            wi[:, :o_b], wi[:, o_sq:o_sk][:, perm], wi[:, o_g:o_g + d], wi[:, o_g + d:][:, perm],
            wi[:, o_sk:o_g]], axis=1).astype(BF16)
        w_ba = jnp.pad(wi[:, o_b:o_sq], ((0, 0), (0, LANES - 2 * heads))).astype(BF16)
        avec = jnp.zeros((2, LANES), F32).at[0, heads:2 * heads].set(a_log[i]).at[1, heads:2 * heads].set(dt_bias[i])
        proj, bgc = _inproj(h_p, h_s, norm_mix[i][None], w_main, w_ba, avec, masks, heads=heads, tm=1024, tn=1536)

        gn = gdn_norm[i][None]
        oa_p, gdn_p = _gdn_prompt(proj, bgc, cw_all[i], gn, mincl_p, batch=batch, seq=seq, heads=heads, dk=dk,
                                  col=col_h, rows=512, heads_per_iter=4)
        oa_s, gdn_s = _gdn_sample(proj, bgc, hist_all, state_gdn, cw_all[i], gn, mincl_s, gdn_s, layer=i, row0=tp,
                                  batch=dbatch, seq=dseq, heads=heads, dk=dk, col=col_h)

        sinks = attn_sinks[i][head_of]
        qw = jnp.tile(q_norm[i], kvh)[None]
        kw = jnp.tile(k_norm[i], kvh)[None]
        ob_p, k_p = _swa_prompt(proj, sinks, qw, kw, bd, bias_p, batch=batch, seq=seq, window=window, kvh=kvh,
                                hd=hd, nq=nq, col=col_b)
        ob_s, *kv_s = _swa_sample(proj, ck_all, cv_all, qw, kw, bd, bias_s, jnp.repeat(sinks, dseq)[:, None], kv_s,
                                  layer=i, row0=tp, batch=dbatch, seq=dseq, kvh=kvh, hd=hd, nq=nq, col=col_b, nseq=8)

        h_p, h_s = _dense(h_p, h_s, oa_p, oa_s, ob_p, ob_s, p_p, p_s, wo_all, wop_all, nf_all, wu_all, wd_all, np_all,
                          wg_all, wp_all, layer=i, tm=512)

        outs["conv_p"].append(_tail_rows(proj, (0, cb), p_tail, width - 1))
        conv_s = lax.slice(proj, (0, tp, 0), (cb, tp + ts, LANES)).reshape(cb, dbatch, dseq, LANES)
        outs["conv_s"].append(
            jnp.transpose(conv_s[:, :, dseq - (width - 1):], (1, 2, 0, 3)).reshape(dbatch, width - 1, conv_dim))
        outs["gdn_p"].append(gdn_p)
        outs["k_p"].append(k_p.reshape(batch, window, kvh, hd))
        v_blocks = (col_b["sv"], col_b["sv"] + slab // LANES)
        outs["v_p"].append(_tail_rows(proj, v_blocks, p_win, window).reshape(batch, window, kvh, hd))

    st = lambda n: jnp.stack(outs[n])
    return (h_p.reshape(batch, seq, d), h_s.reshape(dbatch, dseq, d),
            st("conv_p"), st("gdn_p"), st("k_p"), st("v_p"), st("conv_s"), gdn_s,
            kv_s[0].reshape(depth, dbatch, window, kvh, hd), kv_s[1].reshape(depth, dbatch, window, kvh, hd))
```

```python
import functools

import numpy as np
import jax
import jax.numpy as jnp
from jax import lax
from jax.experimental import pallas as pl
from jax.experimental.pallas import tpu as pltpu

F32 = jnp.float32
BF16 = jnp.bfloat16
EPS = 1e-6
LANES = 128
SUBLANES = 8
MXU_N = 256
GDN_CHUNK = 128
VMEM_LIMIT = 56 * 1024 * 1024


def _sigmoid(x):
    return 0.5 * jnp.tanh(0.5 * x) + 0.5


def _silu(x):
    hx = 0.5 * x
    return hx + hx * jnp.tanh(hx)


def _dot(a, b):
    return jnp.dot(a.astype(BF16), b.astype(BF16), preferred_element_type=F32)


def _dot_nt(a, b):
    return lax.dot_general(a.astype(BF16), b.astype(BF16), (((1,), (1,)), ((), ())), preferred_element_type=F32)


def _dot_tn(a, b):
    return lax.dot_general(a.astype(BF16), b.astype(BF16), (((0,), (0,)), ((), ())), preferred_element_type=F32)


def _rms(x):
    return x * lax.rsqrt(jnp.mean(x * x, axis=-1, keepdims=True) + EPS)


def _l2(x):
    return x * lax.rsqrt(jnp.sum(x * x, axis=-1, keepdims=True) + EPS)


def _inproj_kernel(xp_ref, xs_ref, nw_ref, w_ref, wba_ref, avec_ref, mask_ref, proj32_ref, proj16_ref, bgc_ref,
                   xn_ref, *, heads, npt, n32):
    n = pl.program_id(1)

    @pl.when(n == 0)
    def _():
        x = jnp.where(pl.program_id(0) < npt, xp_ref[...], xs_ref[...])
        xn = (_rms(x) * nw_ref[...]).astype(BF16)
        xn_ref[...] = xn
        ba = jnp.dot(xn, wba_ref[...], preferred_element_type=F32)
        beta = _sigmoid(ba)
        xs = ba + avec_ref[1:2, :]
        softplus = jnp.maximum(xs, 0.0) + jnp.log1p(jnp.exp(-jnp.abs(xs)))
        g = -jnp.exp(avec_ref[0:1, :]) * softplus
        g1 = g.astype(BF16)
        r1 = g - g1.astype(F32)
        g2 = r1.astype(BF16)
        g3 = (r1 - g2.astype(F32)).astype(BF16)
        m = mask_ref[...]
        rows = g.shape[0]
        lane = lax.broadcasted_iota(jnp.int32, (GDN_CHUNK, LANES), 1)
        for r in range(rows // GDN_CHUNK):
            sl = slice(r * GDN_CHUNK, (r + 1) * GDN_CHUNK)
            gc = (jnp.dot(m, g1[sl], preferred_element_type=F32)
                  + jnp.dot(m, g2[sl], preferred_element_type=F32)
                  + jnp.dot(m, g3[sl], preferred_element_type=F32))
            bgc_ref[sl, :] = jnp.where(lane < heads, beta[sl], gc)

    def project(out_ref):
        xn = xn_ref[...]
        for j in range(w_ref.shape[1] // MXU_N):
            r = jnp.dot(xn, w_ref[:, j * MXU_N:(j + 1) * MXU_N], preferred_element_type=F32)
            for i in range(MXU_N // LANES):
                out_ref[j * (MXU_N // LANES) + i] = r[:, i * LANES:(i + 1) * LANES].astype(out_ref.dtype)

    @pl.when(n < n32)
    def _():
        project(proj32_ref)

    @pl.when(n >= n32)
    def _():
        project(proj16_ref)


def _inproj(h_p, h_s, nw, w, wba, avec, masks, *, heads, tm, tn, n_f32):
    d = h_p.shape[1]
    t_all = h_p.shape[0] + h_s.shape[0]
    n_out = w.shape[1]
    npt = h_p.shape[0] // tm
    n32 = n_f32 // tn
    assert n32 * tn == n_f32
    return pl.pallas_call(
        functools.partial(_inproj_kernel, heads=heads, npt=npt, n32=n32),
        grid=(t_all // tm, n_out // tn),
        in_specs=[
            pl.BlockSpec((tm, d), lambda m, n: (jnp.minimum(m, npt - 1), 0)),
            pl.BlockSpec((tm, d), lambda m, n: (jnp.maximum(m - npt, 0), 0)),
            pl.BlockSpec((1, d), lambda m, n: (0, 0)),
            pl.BlockSpec((d, tn), lambda m, n: (0, n)),
            pl.BlockSpec((d, LANES), lambda m, n: (0, 0)),
            pl.BlockSpec((2, LANES), lambda m, n: (0, 0)),
            pl.BlockSpec((None, GDN_CHUNK, GDN_CHUNK), lambda m, n: (jnp.where(m >= npt, 1, 0), 0, 0)),
        ],
        out_specs=[
            pl.BlockSpec((tn // LANES, tm, LANES), lambda m, n: (jnp.minimum(n, n32 - 1), m, 0)),
            pl.BlockSpec((tn // LANES, tm, LANES), lambda m, n: (jnp.maximum(n - n32, 0), m, 0)),
            pl.BlockSpec((tm, LANES), lambda m, n: (m, 0)),
        ],
        out_shape=[
            jax.ShapeDtypeStruct((n_f32 // LANES, t_all, LANES), F32),
            jax.ShapeDtypeStruct(((n_out - n_f32) // LANES, t_all, LANES), BF16),
            jax.ShapeDtypeStruct((t_all, LANES), F32),
        ],
        scratch_shapes=[pltpu.VMEM((tm, d), BF16)],
        compiler_params=pltpu.CompilerParams(
            dimension_semantics=("parallel", "arbitrary"), vmem_limit_bytes=VMEM_LIMIT),
        name="inproj",
    )(h_p, h_s, nw, w, wba, avec, masks)


def _gdn_chunks_local(qs, ks, vs, betas, gcs, mincl, seq_len):
    n = len(qs)
    c = qs[0].shape[0]
    dv = vs[0].shape[1]
    row = lax.broadcasted_iota(jnp.int32, (c, c), 0)
    col = lax.broadcasted_iota(jnp.int32, (c, c), 1)
    decs, egs = [], []
    for gc in gcs:
        gcb = jnp.broadcast_to(gc, (c, c))
        decs.append(jnp.exp(jnp.where(mincl > 0, gcb - gcb.T, -jnp.inf)))
        egs.append(jnp.exp(gc))
    kbs = [ks[i] * betas[i] for i in range(n)]
    kks = [_dot_nt(jnp.concatenate([kbs[i], qs[i]], axis=0), ks[i]) for i in range(n)]
    a_s = [jnp.where(row == col, 0.0, kks[i][:c] * decs[i]) for i in range(n)]
    qks = [kks[i][c:] * decs[i] for i in range(n)]
    same = (row >> 1) == (col >> 1)
    eye = jnp.where(row == col, 1.0, 0.0)
    ts = [eye - jnp.where(same, a, 0.0) for a in a_s]
    na16 = [(-a).astype(BF16) for a in a_s]
    for lvl in range(1, seq_len.bit_length() - 1):
        wider = (row >> (lvl + 1)) == (col >> (lvl + 1))
        sel = wider & jnp.logical_not(same)
        t16 = [t.astype(BF16) for t in ts]
        mids = [jnp.dot(na16[i], t16[i], preferred_element_type=F32) for i in range(n)]
        ts = [jnp.where(sel, jnp.dot(t16[i], mids[i].astype(BF16), preferred_element_type=F32), ts[i])
              for i in range(n)]
        same = wider
    sols = [_dot(ts[i], jnp.concatenate([vs[i] * betas[i], kbs[i] * egs[i]], axis=1)) for i in range(n)]
    return [s[:, :dv] for s in sols], [s[:, dv:] for s in sols], qks, egs


def _conv_taps(u, ru_fn, cw):
    width = cw.shape[0]
    out = u * cw[width - 1:width, :]
    for j in range(1, width):
        out = out + ru_fn(j) * cw[width - 1 - j:width - j, :]
    return _silu(out)


def _gdn_finish(o, z, ga, gn):
    return (_rms(o) * gn * _silu(z.astype(F32)) * _sigmoid(ga.astype(F32))).astype(BF16)


def _head_scalars(bgc, h, heads):
    lane = lax.broadcasted_iota(jnp.int32, bgc.shape, 1)
    beta = jnp.sum(jnp.where(lane == h, bgc, 0.0), axis=-1, keepdims=True)
    gc = jnp.sum(jnp.where(lane == h + heads, bgc, 0.0), axis=-1, keepdims=True)
    return beta, gc


def _gdn_prompt_kernel(q_ref, k_ref, v_ref, z_ref, ga_ref, bgc_ref, cw_ref, gn_ref, mincl_ref,
                       o_ref, sfin_ref,
                       s_ref, tail_ref, u_s, wqe_s, qk_s, kd_s, el_s, *, heads_per_iter):
    c_id = pl.program_id(1)
    heads, rows, dk = q_ref.shape
    nchunk = rows // GDN_CHUNK
    cc = GDN_CHUNK

    @pl.when(c_id == 0)
    def _():
        s_ref[...] = jnp.zeros_like(s_ref)
        tail_ref[...] = jnp.zeros_like(tail_ref)

    row8 = lax.broadcasted_iota(jnp.int32, (SUBLANES, dk), 0)
    bgc = bgc_ref[...]
    mincl = mincl_ref[...]

    def conv(kind, h, u_ref):
        u = u_ref[h]
        tail = tail_ref[kind * heads + h]

        def delayed(j):
            head = jnp.where(row8 < j, pltpu.roll(tail, j, 0), pltpu.roll(u[:SUBLANES], j, 0))
            return jnp.concatenate([head, u_ref[h, pl.ds(SUBLANES - j, rows - SUBLANES), :]], axis=0)

        out = _conv_taps(u, delayed, cw_ref[kind * heads + h])
        tail_ref[kind * heads + h] = u[rows - SUBLANES:]
        return out

    def local_body(it, carry):
        hs, qs, ks, vs, betas, gcs = [], [], [], [], [], []
        for hh in range(heads_per_iter):
            h = it * heads_per_iter + hh
            q = _l2(conv(0, h, q_ref)) * (dk ** -0.5)
            k = _l2(conv(1, h, k_ref))
            v = conv(2, h, v_ref)
            beta, gc = _head_scalars(bgc, h, heads)
            for ci in range(nchunk):
                sl = slice(ci * cc, (ci + 1) * cc)
                hs.append((h, ci))
                qs.append(q[sl]); ks.append(k[sl]); vs.append(v[sl]); betas.append(beta[sl]); gcs.append(gc[sl])
        us, ws, qks, egs = _gdn_chunks_local(qs, ks, vs, betas, gcs, mincl, cc)
        for i, (h, ci) in enumerate(hs):
            g_last = gcs[i][cc - 1:cc, :]
            u_s[h, ci] = us[i]
            wqe_s[h, ci, :cc] = ws[i].astype(BF16)
            wqe_s[h, ci, cc:] = (qs[i] * egs[i]).astype(BF16)
            qk_s[h, ci] = qks[i].astype(BF16)
            kd_s[h, ci] = (ks[i] * jnp.exp(g_last - gcs[i])).astype(BF16)
            el_s[h, ci] = jnp.broadcast_to(jnp.exp(g_last), (SUBLANES, dk))
        return carry

    lax.fori_loop(0, heads // heads_per_iter, local_body, 0)

    states = [s_ref[h] for h in range(heads)]
    gn = gn_ref[...]
    for ci in range(nchunk):
        sl = slice(ci * cc, (ci + 1) * cc)
        wqs = [jnp.dot(wqe_s[h, ci], states[h].astype(BF16), preferred_element_type=F32) for h in range(heads)]
        v_news = [u_s[h, ci] - wqs[h][:cc] for h in range(heads)]
        vn16 = [v.astype(BF16) for v in v_news]
        outs = [wqs[h][cc:] + jnp.dot(qk_s[h, ci], vn16[h], preferred_element_type=F32) for h in range(heads)]
        states = [states[h] * el_s[h, ci][0:1, :]
                  + lax.dot_general(kd_s[h, ci], vn16[h], (((0,), (0,)), ((), ())), preferred_element_type=F32)
                  for h in range(heads)]
        for h in range(heads):
            o_ref[sl, h * dk:(h + 1) * dk] = _gdn_finish(outs[h], z_ref[h, sl, :], ga_ref[h, sl, :], gn)
    for h in range(heads):
        s_ref[h] = states[h]
        sfin_ref[h] = states[h]


def _gdn_prompt(proj32, proj16, bgc, cw, gn, mincl, *, batch, seq, heads, dk, col, rows, heads_per_iter):
    ncb = seq // rows
    nchunk = rows // GDN_CHUNK
    blk = lambda name: pl.BlockSpec((heads, rows, dk), lambda b, c, o=col[name]: (o, b * ncb + c, 0))
    return pl.pallas_call(
        functools.partial(_gdn_prompt_kernel, heads_per_iter=heads_per_iter),
        grid=(batch, ncb),
        in_specs=[
            blk("q"), blk("k"), blk("v"), blk("z"), blk("ga"),
            pl.BlockSpec((rows, LANES), lambda b, c: (b * ncb + c, 0)),
            pl.BlockSpec(cw.shape, lambda b, c: (0, 0, 0)),
            pl.BlockSpec((1, dk), lambda b, c: (0, 0)),
            pl.BlockSpec((GDN_CHUNK, GDN_CHUNK), lambda b, c: (0, 0)),
        ],
        out_specs=[
            pl.BlockSpec((rows, heads * dk), lambda b, c: (b * ncb + c, 0)),
            pl.BlockSpec((None, heads, dk, dk), lambda b, c: (b, 0, 0, 0)),
        ],
        out_shape=[
            jax.ShapeDtypeStruct((batch * seq, heads * dk), BF16),
            jax.ShapeDtypeStruct((batch, heads, dk, dk), F32),
        ],
        scratch_shapes=[
            pltpu.VMEM((heads, dk, dk), F32),
            pltpu.VMEM((3 * heads, SUBLANES, dk), F32),
            pltpu.VMEM((heads, nchunk, GDN_CHUNK, dk), F32),
            pltpu.VMEM((heads, nchunk, 2 * GDN_CHUNK, dk), BF16),
            pltpu.VMEM((heads, nchunk, GDN_CHUNK, GDN_CHUNK), BF16),
            pltpu.VMEM((heads, nchunk, GDN_CHUNK, dk), BF16),
            pltpu.VMEM((heads, nchunk, SUBLANES, dk), F32),
        ],
        compiler_params=pltpu.CompilerParams(
            dimension_semantics=("parallel", "arbitrary"), vmem_limit_bytes=VMEM_LIMIT),
        name="gdn_prompt",
    )(proj32, proj32, proj32, proj16, proj16, bgc, cw, gn, mincl)


def _gdn_sample_kernel(q_ref, k_ref, v_ref, z_ref, ga_ref, bgc_ref, hq_ref, hk_ref, hv_ref,
                       cw_ref, gn_ref, mincl_ref, s0_ref, *rest, heads, seq):
    o_ref, sout_ref = rest[-2:]
    h = pl.program_id(1)
    rows, dk = q_ref.shape
    nseq = rows // seq
    t_idx = lax.broadcasted_iota(jnp.int32, (nseq, seq, dk), 1)

    def conv(kind, u_ref, hist_ref):
        u = u_ref[...]
        u3 = u.reshape(nseq, seq, dk)
        hist3 = hist_ref[...].reshape(nseq, seq, dk)

        def delayed(j):
            xj = jnp.where(t_idx < j, pltpu.roll(hist3, j, 1), pltpu.roll(u3, j, 1))
            return xj.reshape(rows, dk)

        return _conv_taps(u, delayed, cw_ref[kind * heads + h])

    q = _l2(conv(0, q_ref, hq_ref)) * (dk ** -0.5)
    k = _l2(conv(1, k_ref, hk_ref))
    v = conv(2, v_ref, hv_ref)
    beta, gc = _head_scalars(bgc_ref[...], h, heads)

    (u,), (w,), (qk,), (eg,) = _gdn_chunks_local([q], [k], [v], [beta], [gc], mincl_ref[...], seq)
    qe = q * eg
    pad = jnp.zeros((seq, dk), F32)
    sls = [slice(si * seq, (si + 1) * seq) for si in range(nseq)]
    g_lasts = [gc[(si + 1) * seq - 1:(si + 1) * seq, :] for si in range(nseq)]
    s0 = [s0_ref[si] for si in range(nseq)]
    wqs = [_dot(jnp.concatenate([w[sl], qe[sl]], axis=0), s0[si]) for si, sl in enumerate(sls)]
    v_news = [u[sl] - wqs[si][:seq] for si, sl in enumerate(sls)]
    for si, sl in enumerate(sls):
        kd = k[sl] * jnp.exp(g_lasts[si] - gc[sl])
        sout_ref[si] = s0[si] * jnp.exp(g_lasts[si]) + _dot_tn(jnp.concatenate([kd, pad], axis=0),
                                                               jnp.concatenate([v_news[si], pad], axis=0))
    o = jnp.concatenate([wq[seq:] for wq in wqs], axis=0) + _dot(qk, jnp.concatenate(v_news, axis=0))
    o_ref[...] = _gdn_finish(o, z_ref[...], ga_ref[...], gn_ref[...])


def _gdn_sample(proj32, proj16, bgc, hist, s0, cw, gn, mincl, s_prev, *, layer, row0, batch, seq, heads, dk, col):
    rows = GDN_CHUNK
    nseq = rows // seq
    r0 = row0 // rows
    depth = s0.shape[0]
    blk = lambda name: pl.BlockSpec((None, rows, dk), lambda b, h, o=col[name]: (o * heads + h, r0 + b, 0))
    hblk = lambda name: pl.BlockSpec((None, rows, dk), lambda b, h, o=col[name]: (layer, b, o * heads + h))
    state_blk = pl.BlockSpec((None, nseq, None, dk, dk), lambda b, h: (layer, b, h, 0, 0))
    extra_in = [] if s_prev is None else [s_prev]
    return pl.pallas_call(
        functools.partial(_gdn_sample_kernel, heads=heads, seq=seq),
        grid=(batch * seq // rows, heads),
        in_specs=[
            blk("q"), blk("k"), blk("v"), blk("z"), blk("ga"),
            pl.BlockSpec((rows, LANES), lambda b, h: (r0 + b, 0)),
            hblk("q"), hblk("k"), hblk("v"),
            pl.BlockSpec(cw.shape, lambda b, h: (0, 0, 0)),
            pl.BlockSpec((1, dk), lambda b, h: (0, 0)),
            pl.BlockSpec((GDN_CHUNK, GDN_CHUNK), lambda b, h: (0, 0)),
            state_blk,
        ] + [pl.BlockSpec(memory_space=pl.ANY)] * len(extra_in),
        out_specs=[
            pl.BlockSpec((rows, dk), lambda b, h: (b, h)),
            state_blk,
        ],
        out_shape=[
            jax.ShapeDtypeStruct((batch * seq, heads * dk), BF16),
            jax.ShapeDtypeStruct((depth, batch, heads, dk, dk), F32),
        ],
        input_output_aliases={13: 1} if extra_in else {},
        compiler_params=pltpu.CompilerParams(
            dimension_semantics=("parallel", "parallel"), vmem_limit_bytes=VMEM_LIMIT),
        name="gdn_sample",
    )(proj32, proj32, proj32, proj16, proj16, bgc, hist, hist, hist, cw, gn, mincl, s0, *extra_in)


def _chunk_rms(x, bd, w):
    ms = _dot(x * x, bd)
    return x * lax.rsqrt(ms + EPS) * w


def _wide(ref, first, n):
    return jnp.concatenate([ref[first + i] for i in range(n)], axis=-1)


def _natural_blocks(slabs, hd):
    groups = len(slabs)
    per_block = LANES // hd
    lane = lax.broadcasted_iota(jnp.int32, (slabs[0].shape[0], LANES), 1)
    blocks = []
    for b in range(slabs[0].shape[1] * groups // LANES):
        blk = None
        for i in range(per_block):
            j, g = divmod(b * per_block + i, groups)
            first = (hd * j) // LANES * LANES
            piece = slabs[g][:, first:first + LANES]
            shift = (i * hd - hd * j) % LANES
            if shift:
                piece = pltpu.roll(piece, shift, 1)
            blk = piece if blk is None else jnp.where(lane < i * hd, blk, piece)
        blocks.append(blk)
    return blocks


def _swa_prompt_kernel(sink_ref, q_ref, kp_ref, kc_ref, vp_ref, vc_ref, gb_ref, qw_ref, kw_ref, bd_ref, bias_ref,
                       o_ref, kout_ref, *, kvh, hd):
    blk = q_ref.shape[1]
    slab = kvh * hd
    spb = slab // LANES
    groups = q_ref.shape[0] // spb
    bd = bd_ref[...]
    kn = _chunk_rms(_wide(kc_ref, 0, spb).astype(F32), bd, kw_ref[...])
    kout_ref[...] = kn
    kband = jnp.concatenate([_chunk_rms(_wide(kp_ref, 0, spb).astype(F32), bd, kw_ref[...]), kn],
                            axis=0).astype(BF16)
    vband = jnp.concatenate([_wide(vp_ref, 0, spb), _wide(vc_ref, 0, spb)], axis=0)
    lane_head = lax.broadcasted_iota(jnp.int32, (blk, slab), 1) // hd

    def scores(g):
        qn = _chunk_rms(_wide(q_ref, g * spb, spb).astype(F32), bd, qw_ref[...]) * (hd ** -0.5)
        qs = jnp.concatenate([jnp.where(lane_head == j, qn, 0.0) for j in range(kvh)], axis=0).astype(BF16)
        return lax.dot_general(qs, kband, (((1,), (1,)), ((), ())), preferred_element_type=F32)

    s_next = scores(0)
    gated = []
    for g in range(groups):
        s_all = s_next
        if g + 1 < groups:
            s_next = scores(g + 1)
        probs = []
        for j in range(kvh):
            p = g * kvh + j
            s = s_all[j * blk:(j + 1) * blk] + bias_ref[p]
            sink = sink_ref[p]
            m = jnp.maximum(jnp.max(s, axis=-1, keepdims=True), sink)
            e = jnp.exp(s - m)
            den = jnp.sum(e, axis=-1, keepdims=True) + jnp.exp(sink - m)
            probs.append((e * (1.0 / den)).astype(BF16))
        pv = jnp.dot(jnp.concatenate(probs, axis=0), vband, preferred_element_type=F32)
        acc = pv[(kvh - 1) * blk:]
        for j in range(kvh - 2, -1, -1):
            acc = jnp.where(lane_head == j, pv[j * blk:(j + 1) * blk], acc)
        gated.append(acc * _sigmoid(_wide(gb_ref, g * spb, spb).astype(F32)))
    for b, blk_out in enumerate(_natural_blocks(gated, hd)):
        o_ref[:, b * LANES:(b + 1) * LANES] = blk_out.astype(BF16)


def _swa_prompt(proj, sinks, qw, kw, bd, bias, *, batch, seq, window, kvh, hd, nq, col):
    nb = seq // window
    slab = kvh * hd
    nheads = nq // hd
    qb, sb = nq // LANES, slab // LANES
    cur = lambda name, n: pl.BlockSpec((n, window, LANES), lambda b, c, o=col[name] // n: (o, b * nb + c, 0))
    prev = lambda name, n: pl.BlockSpec(
        (n, window, LANES), lambda b, c, o=col[name] // n: (o, b * nb + jnp.maximum(c - 1, 0), 0))
    const2 = lambda b, c: (0, 0)
    return pl.pallas_call(
        functools.partial(_swa_prompt_kernel, kvh=kvh, hd=hd),
        grid=(batch, nb),
        in_specs=[
            pl.BlockSpec(memory_space=pltpu.SMEM),
            cur("sq", qb), prev("sk", sb), cur("sk", sb), prev("sv", sb), cur("sv", sb), cur("gb", qb),
            pl.BlockSpec((1, slab), const2),
            pl.BlockSpec((1, slab), const2),
            pl.BlockSpec((slab, slab), const2),
            pl.BlockSpec((None, nheads, window, 2 * window), lambda b, c: (jnp.minimum(c, 1), 0, 0, 0)),
        ],
        out_specs=[
            pl.BlockSpec((window, nq), lambda b, c: (b * nb + c, 0)),
            pl.BlockSpec((None, window, slab), lambda b, c: (b, 0, 0)),
        ],
        out_shape=[
            jax.ShapeDtypeStruct((batch * seq, nq), BF16),
            jax.ShapeDtypeStruct((batch, window, slab), F32),
        ],
        compiler_params=pltpu.CompilerParams(
            dimension_semantics=("parallel", "arbitrary"), vmem_limit_bytes=VMEM_LIMIT),
        name="swa_prompt",
    )(sinks, proj, proj, proj, proj, proj, proj, qw, kw, bd, bias)


def _swa_sample_kernel(q_ref, k_ref, v_ref, gb_ref, ck_ref, cv_ref, qw_ref, kw_ref, bd_ref, bias_ref, sink_ref,
                       *rest, kvh, hd, seq):
    o_ref, kout_ref, vout_ref = rest[-3:]
    rows = q_ref.shape[1]
    nseq = rows // seq
    slab = kvh * hd
    spb = slab // LANES
    groups = q_ref.shape[0] // spb
    window = ck_ref.shape[1]
    nkeys = bias_ref.shape[1]
    bd = bd_ref[...]
    kn = _chunk_rms(_wide(k_ref, 0, spb).astype(F32), bd, kw_ref[...]).reshape(nseq, seq, slab)
    vn = _wide(v_ref, 0, spb).astype(F32).reshape(nseq, seq, slab)
    ck = ck_ref[...]
    cv = cv_ref[...]
    kout_ref[...] = jnp.concatenate([ck[:, seq:, :], kn], axis=1)
    vout_ref[...] = jnp.concatenate([cv[:, seq:, :], vn], axis=1)
    zpad = jnp.zeros((nseq, nkeys - window - seq, slab), F32)
    k_all = jnp.concatenate([ck, kn, zpad], axis=1).astype(BF16)
    v_all = jnp.concatenate([cv, vn, zpad], axis=1).astype(BF16)

    lane_head = lax.broadcasted_iota(jnp.int32, (rows, slab), 1) // hd
    pieces = []
    for g in range(groups):
        qn = _chunk_rms(_wide(q_ref, g * spb, spb).astype(F32), bd, qw_ref[...]) * (hd ** -0.5)
        for j in range(kvh):
            pieces.append(jnp.where(lane_head == j, qn, 0.0).reshape(nseq, seq, slab))
    q_all = jnp.concatenate(pieces, axis=1).astype(BF16)
    s = jnp.einsum("bqd,bkd->bqk", q_all, k_all, preferred_element_type=F32) + bias_ref[...][None]
    sink = sink_ref[...][None]
    m = jnp.maximum(jnp.max(s, axis=-1, keepdims=True), sink)
    e = jnp.exp(s - m)
    den = jnp.sum(e, axis=-1, keepdims=True) + jnp.exp(sink - m)
    pv = jnp.einsum("bqk,bkd->bqd", (e * (1.0 / den)).astype(BF16), v_all, preferred_element_type=F32)
    lane_head3 = lax.broadcasted_iota(jnp.int32, (nseq, seq, slab), 2) // hd
    gated = []
    for g in range(groups):
        acc = jnp.zeros((nseq, seq, slab), F32)
        for j in range(kvh):
            p = g * kvh + j
            acc = acc + jnp.where(lane_head3 == j, pv[:, p * seq:(p + 1) * seq, :], 0.0)
        gated.append(acc.reshape(rows, slab) * _sigmoid(_wide(gb_ref, g * spb, spb).astype(F32)))
    for b, blk_out in enumerate(_natural_blocks(gated, hd)):
        o_ref[:, b * LANES:(b + 1) * LANES] = blk_out.astype(BF16)


def _swa_sample(proj, cache_k, cache_v, qw, kw, bd, bias, sink_col, kv_prev, *, layer, row0, batch, seq, kvh, hd,
                nq, col, nseq):
    rows = nseq * seq
    slab = kvh * hd
    depth, _, window, _ = cache_k.shape
    r0 = row0 // rows
    cache_blk = pl.BlockSpec((None, nseq, window, slab), lambda b: (layer, b, 0, 0))
    extra_in = [] if kv_prev is None else list(kv_prev)
    qb, sb = nq // LANES, slab // LANES
    cur = lambda name, n: pl.BlockSpec((n, rows, LANES), lambda b, o=col[name] // n: (o, r0 + b, 0))
    const2 = lambda b: (0, 0)
    return pl.pallas_call(
        functools.partial(_swa_sample_kernel, kvh=kvh, hd=hd, seq=seq),
        grid=(batch // nseq,),
        in_specs=[
            cur("sq", qb), cur("sk", sb), cur("sv", sb), cur("gb", qb),
            cache_blk, cache_blk,
            pl.BlockSpec((1, slab), const2),
            pl.BlockSpec((1, slab), const2),
            pl.BlockSpec((slab, slab), const2),
            pl.BlockSpec(bias.shape, const2),
            pl.BlockSpec(sink_col.shape, const2),
        ] + [pl.BlockSpec(memory_space=pl.ANY)] * len(extra_in),
        out_specs=[
            pl.BlockSpec((rows, nq), lambda b: (b, 0)),
            cache_blk, cache_blk,
        ],
        out_shape=[
            jax.ShapeDtypeStruct((batch * seq, nq), BF16),
            jax.ShapeDtypeStruct((depth, batch, window, slab), F32),
            jax.ShapeDtypeStruct((depth, batch, window, slab), F32),
        ],
        input_output_aliases={11: 1, 12: 2} if extra_in else {},
        compiler_params=pltpu.CompilerParams(
            dimension_semantics=("parallel",), vmem_limit_bytes=VMEM_LIMIT),
        name="swa_sample",
    )(proj, proj, proj, proj, cache_k, cache_v, qw, kw, bd, bias, sink_col, *extra_in)


def _dense_kernel(hp_ref, hs_ref, oap_ref, oas_ref, obp_ref, obs_ref, pp_ref, ps_ref, wo_ref, nf_ref,
                  wu_ref, wd_ref, np_ref, wg_ref, wp_ref, outp_ref, outs_ref, *, npt, ff_chunk):
    is_prompt = pl.program_id(0) < npt
    oa = jnp.where(is_prompt, oap_ref[...], oas_ref[...])
    ob = jnp.where(is_prompt, obp_ref[...], obs_ref[...])
    h = jnp.where(is_prompt, hp_ref[...], hs_ref[...])
    h = h + jnp.dot(oa + ob, wo_ref[...], preferred_element_type=F32)
    xn = (_rms(h) * nf_ref[...]).astype(BF16)
    acc = jnp.zeros_like(h)
    for c in range(wu_ref.shape[1] // ff_chunk):
        sl = slice(c * ff_chunk, (c + 1) * ff_chunk)
        hid = jnp.maximum(jnp.dot(xn, wu_ref[:, sl], preferred_element_type=F32), 0.0)
        acc = acc + jnp.dot((hid * hid).astype(BF16), wd_ref[sl, :], preferred_element_type=F32)
    h = h + acc
    xn = (_rms(h) * np_ref[...]).astype(BF16)
    gate = _sigmoid(jnp.dot(xn, wg_ref[...], preferred_element_type=F32))
    p = jnp.where(is_prompt, pp_ref[...], ps_ref[...])
    pe = jnp.dot(p.astype(BF16), wp_ref[...], preferred_element_type=F32)
    out = h + gate * pe

    @pl.when(is_prompt)
    def _():
        outp_ref[...] = out

    @pl.when(jnp.logical_not(is_prompt))
    def _():
        outs_ref[...] = out


def _dense(h_p, h_s, oa_p, oa_s, ob_p, ob_s, p_p, p_s, wo, nf, wu, wd, npl, wg, wp, *, layer, tm):
    tp, d = h_p.shape
    ts = h_s.shape[0]
    npt = tp // tm
    rows_p = lambda m: (jnp.minimum(m, npt - 1), 0)
    rows_s = lambda m: (jnp.maximum(m - npt, 0), 0)
    resident = lambda a: pl.BlockSpec((None,) + a.shape[1:], lambda m: (layer, 0, 0), pipeline_mode=pl.Buffered(1))
    return pl.pallas_call(
        functools.partial(_dense_kernel, npt=npt, ff_chunk=1024),
        grid=((tp + ts) // tm,),
        in_specs=[
            pl.BlockSpec((tm, d), rows_p), pl.BlockSpec((tm, d), rows_s),
            pl.BlockSpec((tm, d), rows_p), pl.BlockSpec((tm, d), rows_s),
            pl.BlockSpec((tm, d), rows_p), pl.BlockSpec((tm, d), rows_s),
            pl.BlockSpec((None, tm, p_p.shape[2]), lambda m: (layer, jnp.minimum(m, npt - 1), 0)),
            pl.BlockSpec((None, tm, p_s.shape[2]), lambda m: (layer, jnp.maximum(m - npt, 0), 0)),
            resident(wo), resident(nf), resident(wu), resident(wd),
            resident(npl), resident(wg), resident(wp),
        ],
        out_specs=[pl.BlockSpec((tm, d), rows_p), pl.BlockSpec((tm, d), rows_s)],
        out_shape=[jax.ShapeDtypeStruct((tp, d), F32), jax.ShapeDtypeStruct((ts, d), F32)],
        compiler_params=pltpu.CompilerParams(
            dimension_semantics=("arbitrary",), vmem_limit_bytes=VMEM_LIMIT),
        name="dense",
    )(h_p, h_s, oa_p, oa_s, ob_p, ob_s, p_p, p_s, wo, nf, wu, wd, npl, wg, wp)


def _seq_mask(seq_len):
    i = np.arange(GDN_CHUNK)
    m = (i[:, None] // seq_len == i[None, :] // seq_len) & (i[None, :] <= i[:, None])
    return m.astype(np.float32)


def _tail_rows(proj, blocks, starts, n):
    b0, b1 = blocks
    parts = [lax.slice(proj, (b0, s, 0), (b1, s + n, proj.shape[2])) for s in starts]
    x = jnp.stack(parts)
    return jnp.transpose(x, (0, 2, 1, 3)).reshape(len(starts), n, (b1 - b0) * proj.shape[2])


def kernel(x_prompt, x_sample, cache_conv, state_gdn, cache_swa_k, cache_swa_v, p_prompt, p_sample, norm_mix, w_in, conv_w, a_log, dt_bias, gdn_norm, q_norm, k_norm, attn_sinks, w_out, norm_ffn, w_up, w_down, norm_ple, w_ple_gate, w_ple_proj):
    batch, seq, d = x_prompt.shape
    dbatch, dseq, _ = x_sample.shape
    depth = w_in.shape[0]
    heads, dk, dv = state_gdn.shape[2:]
    window, kvh, hd = cache_swa_k.shape[2:]
    nq = d
    nheads = nq // hd
    groups = nheads // kvh
    slab = kvh * hd
    key_dim = heads * dk
    val_dim = heads * dv
    conv_dim = 2 * key_dim + val_dim
    width = conv_w.shape[1]
    tp, ts = batch * seq, dbatch * dseq
    assert dk == LANES and dv == LANES and val_dim == d and key_dim == d and slab % LANES == 0
    assert seq % window == 0 and window == GDN_CHUNK and GDN_CHUNK % dseq == 0 and dseq >= width - 1
    assert dseq == SUBLANES and ts % GDN_CHUNK == 0

    o_z = conv_dim
    o_b = o_z + val_dim
    o_a = o_b + heads
    o_sq = o_a + heads
    o_sk = o_sq + nq
    o_sv = o_sk + slab
    o_g = o_sv + slab
    perm = np.concatenate([np.arange(hd) + (j * groups + g) * hd for g in range(groups) for j in range(kvh)])
    col_h = {"q": 0, "k": 1, "v": 2, "z": 0, "ga": 2}
    off = {"sq": val_dim, "gb": val_dim + nq + d, "sk": val_dim + nq + 2 * d, "sv": val_dim + nq + 2 * d + slab}
    col_b = {name: o // LANES for name, o in off.items()}
    cb = conv_dim // LANES

    h_p = x_prompt.reshape(tp, d)
    h_s = x_sample.reshape(ts, d)
    bf = lambda a: a.astype(BF16)
    wo_all, wu_all, wd_all = bf(w_out), bf(w_up), bf(w_down)
    wg_all, wp_all = bf(w_ple_gate), bf(w_ple_proj)
    nf_all, np_all = norm_ffn[:, None, :], norm_ple[:, None, :]
    p_p = p_prompt.reshape(depth, tp, -1)
    p_s = p_sample.reshape(depth, ts, -1)

    masks = jnp.asarray(np.stack([_seq_mask(GDN_CHUNK), _seq_mask(dseq)]), BF16)
    mincl_p = jnp.asarray(_seq_mask(GDN_CHUNK), F32)
    mincl_s = jnp.asarray(_seq_mask(dseq), F32)
    bd = jnp.asarray(np.kron(np.eye(kvh), np.full((hd, hd), 1.0 / hd)), BF16)

    head_of = np.array([(p % kvh) * groups + p // kvh for p in range(nheads)])
    slopes = jnp.exp2(-8.0 * (jnp.asarray(head_of, F32) + 1.0) / nheads)
    qi = np.arange(window)[:, None]
    kj = np.arange(2 * window)[None, :]
    dist = window + qi - kj
    ok = (dist >= 0) & (dist <= window)
    ok_first = ok & (kj >= window)
    dist_f = jnp.asarray(dist, F32)
    bias_p = jnp.stack([
        jnp.where(jnp.asarray(okv)[None], -slopes[:, None, None] * dist_f[None], -jnp.inf) for okv in (ok_first, ok)])
    nkeys = -(-(window + dseq) // 16) * 16
    ti = np.tile(np.arange(dseq), nheads)[:, None]
    sj = np.arange(nkeys)[None, :]
    dist_s = ti + window - sj
    ok_s = (dist_s >= 0) & (dist_s <= window) & (sj < window + dseq)
    slopes_rows = jnp.repeat(slopes, dseq)[:, None]
    bias_s = jnp.where(jnp.asarray(ok_s), -slopes_rows * jnp.asarray(dist_s, F32), -jnp.inf)

    hist_all = jnp.pad(cache_conv, ((0, 0), (0, 0), (dseq - cache_conv.shape[2], 0), (0, 0)))
    hist_all = hist_all.reshape(depth, ts, conv_dim)
    ck_all = cache_swa_k.reshape(depth, dbatch, window, slab)
    cv_all = cache_swa_v.reshape(depth, dbatch, window, slab)
    cw_all = jnp.transpose(conv_w.reshape(depth, width, cb, LANES), (0, 2, 1, 3))

    outs = {n: [] for n in ("conv_p", "gdn_p", "k_p", "v_p", "conv_s")}
    gdn_s = None
    kv_s = None
    p_tail = [b * seq + seq - (width - 1) for b in range(batch)]
    p_win = [b * seq + seq - window for b in range(batch)]
    for i in range(depth):
        wi = w_in[i]
        w_main = jnp.concatenate([
            wi[:, :o_b], wi[:, o_sq:o_sk][:, perm], wi[:, o_g:o_g + d], wi[:, o_g + d:][:, perm],
            wi[:, o_sk:o_g]], axis=1).astype(BF16)
        w_ba = jnp.pad(wi[:, o_b:o_sq], ((0, 0), (0, LANES - 2 * heads))).astype(BF16)
        avec = jnp.zeros((2, LANES), F32).at[0, heads:2 * heads].set(a_log[i]).at[1, heads:2 * heads].set(dt_bias[i])
        proj32, proj16, bgc = _inproj(h_p, h_s, norm_mix[i][None], w_main, w_ba, avec, masks, heads=heads, tm=1024,
                                      tn=1536, n_f32=conv_dim)

        gn = gdn_norm[i][None]
        oa_p, gdn_p = _gdn_prompt(proj32, proj16, bgc, cw_all[i], gn, mincl_p, batch=batch, seq=seq, heads=heads, dk=dk,
                                  col=col_h, rows=512, heads_per_iter=4)
        oa_s, gdn_s = _gdn_sample(proj32, proj16, bgc, hist_all, state_gdn, cw_all[i], gn, mincl_s, gdn_s, layer=i, row0=tp,
                                  batch=dbatch, seq=dseq, heads=heads, dk=dk, col=col_h)

        sinks = attn_sinks[i][head_of]
        qw = jnp.tile(q_norm[i], kvh)[None]
        kw = jnp.tile(k_norm[i], kvh)[None]
        ob_p, k_p = _swa_prompt(proj16, sinks, qw, kw, bd, bias_p, batch=batch, seq=seq, window=window, kvh=kvh,
                                hd=hd, nq=nq, col=col_b)
        ob_s, *kv_s = _swa_sample(proj16, ck_all, cv_all, qw, kw, bd, bias_s, jnp.repeat(sinks, dseq)[:, None], kv_s,
                                  layer=i, row0=tp, batch=dbatch, seq=dseq, kvh=kvh, hd=hd, nq=nq, col=col_b, nseq=8)

        h_p, h_s = _dense(h_p, h_s, oa_p, oa_s, ob_p, ob_s, p_p, p_s, wo_all, nf_all, wu_all, wd_all, np_all,
                          wg_all, wp_all, layer=i, tm=512)

        outs["conv_p"].append(_tail_rows(proj32, (0, cb), p_tail, width - 1))
        conv_s = lax.slice(proj32, (0, tp, 0), (cb, tp + ts, LANES)).reshape(cb, dbatch, dseq, LANES)
        outs["conv_s"].append(
            jnp.transpose(conv_s[:, :, dseq - (width - 1):], (1, 2, 0, 3)).reshape(dbatch, width - 1, conv_dim))
        outs["gdn_p"].append(gdn_p)
        outs["k_p"].append(k_p.reshape(batch, window, kvh, hd))
        v_blocks = (col_b["sv"], col_b["sv"] + slab // LANES)
        outs["v_p"].append(_tail_rows(proj16, v_blocks, p_win, window).astype(F32).reshape(batch, window, kvh, hd))

    st = lambda n: jnp.stack(outs[n])
    return (h_p.reshape(batch, seq, d), h_s.reshape(dbatch, dseq, d),
            st("conv_p"), st("gdn_p"), st("k_p"), st("v_p"), st("conv_s"), gdn_s,
            kv_s[0].reshape(depth, dbatch, window, kvh, hd), kv_s[1].reshape(depth, dbatch, window, kvh, hd))
```

```python
import functools

import numpy as np
import jax
import jax.numpy as jnp
from jax import lax
from jax.experimental import pallas as pl
from jax.experimental.pallas import tpu as pltpu

F32 = jnp.float32
BF16 = jnp.bfloat16
EPS = 1e-6
LANES = 128
SUBLANES = 8
MXU_N = 256
GDN_CHUNK = 128
VMEM_LIMIT = 56 * 1024 * 1024


def _sigmoid(x):
    return 0.5 * jnp.tanh(0.5 * x) + 0.5


def _silu(x):
    hx = 0.5 * x
    return hx + hx * jnp.tanh(hx)


def _dot(a, b):
    return jnp.dot(a.astype(BF16), b.astype(BF16), preferred_element_type=F32)


def _dot_nt(a, b):
    return lax.dot_general(a.astype(BF16), b.astype(BF16), (((1,), (1,)), ((), ())), preferred_element_type=F32)


def _dot_tn(a, b):
    return lax.dot_general(a.astype(BF16), b.astype(BF16), (((0,), (0,)), ((), ())), preferred_element_type=F32)


def _rms(x):
    return x * lax.rsqrt(jnp.mean(x * x, axis=-1, keepdims=True) + EPS)


def _l2(x):
    return x * lax.rsqrt(jnp.sum(x * x, axis=-1, keepdims=True) + EPS)


def _inproj_kernel(xp_ref, xs_ref, nw_ref, w_ref, wba_ref, avec_ref, mask_ref, proj32_ref, proj16_ref, bgc_ref,
                   xn_ref, *, heads, npt, n32):
    n = pl.program_id(1)

    @pl.when(n == 0)
    def _():
        x = jnp.where(pl.program_id(0) < npt, xp_ref[...], xs_ref[...])
        xn = (_rms(x) * nw_ref[...]).astype(BF16)
        xn_ref[...] = xn
        ba = jnp.dot(xn, wba_ref[...], preferred_element_type=F32)
        beta = _sigmoid(ba)
        xs = ba + avec_ref[1:2, :]
        softplus = jnp.maximum(xs, 0.0) + jnp.log1p(jnp.exp(-jnp.abs(xs)))
        g = -jnp.exp(avec_ref[0:1, :]) * softplus
        g1 = g.astype(BF16)
        r1 = g - g1.astype(F32)
        g2 = r1.astype(BF16)
        g3 = (r1 - g2.astype(F32)).astype(BF16)
        m = mask_ref[...]
        rows = g.shape[0]
        lane = lax.broadcasted_iota(jnp.int32, (GDN_CHUNK, LANES), 1)
        for r in range(rows // GDN_CHUNK):
            sl = slice(r * GDN_CHUNK, (r + 1) * GDN_CHUNK)
            gc = (jnp.dot(m, g1[sl], preferred_element_type=F32)
                  + jnp.dot(m, g2[sl], preferred_element_type=F32)
                  + jnp.dot(m, g3[sl], preferred_element_type=F32))
            bgc_ref[sl, :] = jnp.where(lane < heads, beta[sl], gc)

    def project(out_ref):
        xn = xn_ref[...]
        for j in range(w_ref.shape[1] // MXU_N):
            r = jnp.dot(xn, w_ref[:, j * MXU_N:(j + 1) * MXU_N], preferred_element_type=F32)
            for i in range(MXU_N // LANES):
                out_ref[j * (MXU_N // LANES) + i] = r[:, i * LANES:(i + 1) * LANES].astype(out_ref.dtype)

    @pl.when(n < n32)
    def _():
        project(proj32_ref)

    @pl.when(n >= n32)
    def _():
        project(proj16_ref)


def _inproj(h_p, h_s, nw, w, wba, avec, masks, *, layer, heads, tm, tn, n_f32):
    d = h_p.shape[1]
    t_all = h_p.shape[0] + h_s.shape[0]
    n_out = w.shape[1]
    npt = h_p.shape[0] // tm
    n32 = n_f32 // tn
    assert n32 * tn == n_f32
    return pl.pallas_call(
        functools.partial(_inproj_kernel, heads=heads, npt=npt, n32=n32),
        grid=(t_all // tm, n_out // tn),
        in_specs=[
            pl.BlockSpec((tm, d), lambda m, n: (jnp.minimum(m, npt - 1), 0)),
            pl.BlockSpec((tm, d), lambda m, n: (jnp.maximum(m - npt, 0), 0)),
            pl.BlockSpec((None, 1, d), lambda m, n: (layer, 0, 0)),
            pl.BlockSpec((d, tn), lambda m, n: (0, n)),
            pl.BlockSpec((d, LANES), lambda m, n: (0, 0)),
            pl.BlockSpec((None, 2, LANES), lambda m, n: (layer, 0, 0)),
            pl.BlockSpec((None, GDN_CHUNK, GDN_CHUNK), lambda m, n: (jnp.where(m >= npt, 1, 0), 0, 0)),
        ],
        out_specs=[
            pl.BlockSpec((tn // LANES, tm, LANES), lambda m, n: (jnp.minimum(n, n32 - 1), m, 0)),
            pl.BlockSpec((tn // LANES, tm, LANES), lambda m, n: (jnp.maximum(n - n32, 0), m, 0)),
            pl.BlockSpec((tm, LANES), lambda m, n: (m, 0)),
        ],
        out_shape=[
            jax.ShapeDtypeStruct((n_f32 // LANES, t_all, LANES), F32),
            jax.ShapeDtypeStruct(((n_out - n_f32) // LANES, t_all, LANES), BF16),
            jax.ShapeDtypeStruct((t_all, LANES), F32),
        ],
        scratch_shapes=[pltpu.VMEM((tm, d), BF16)],
        compiler_params=pltpu.CompilerParams(
            dimension_semantics=("parallel", "arbitrary"), vmem_limit_bytes=VMEM_LIMIT),
        name="inproj",
    )(h_p, h_s, nw, w, wba, avec, masks)


def _gdn_chunks_local(qs, ks, vs, betas, gcs, mincl, seq_len):
    n = len(qs)
    c = qs[0].shape[0]
    dv = vs[0].shape[1]
    row = lax.broadcasted_iota(jnp.int32, (c, c), 0)
    col = lax.broadcasted_iota(jnp.int32, (c, c), 1)
    decs, egs = [], []
    for gc in gcs:
        gcb = jnp.broadcast_to(gc, (c, c))
        decs.append(jnp.exp(jnp.where(mincl > 0, gcb - gcb.T, -jnp.inf)))
        egs.append(jnp.exp(gc))
    kbs = [ks[i] * betas[i] for i in range(n)]
    kks = [_dot_nt(jnp.concatenate([kbs[i], qs[i]], axis=0), ks[i]) for i in range(n)]
    a_s = [jnp.where(row == col, 0.0, kks[i][:c] * decs[i]) for i in range(n)]
    qks = [kks[i][c:] * decs[i] for i in range(n)]
    same = (row >> 1) == (col >> 1)
    eye = jnp.where(row == col, 1.0, 0.0)
    ts = [eye - jnp.where(same, a, 0.0) for a in a_s]
    na16 = [(-a).astype(BF16) for a in a_s]
    for lvl in range(1, seq_len.bit_length() - 1):
        wider = (row >> (lvl + 1)) == (col >> (lvl + 1))
        sel = wider & jnp.logical_not(same)
        t16 = [t.astype(BF16) for t in ts]
        mids = [jnp.dot(na16[i], t16[i], preferred_element_type=F32) for i in range(n)]
        ts = [jnp.where(sel, jnp.dot(t16[i], mids[i].astype(BF16), preferred_element_type=F32), ts[i])
              for i in range(n)]
        same = wider
    sols = [_dot(ts[i], jnp.concatenate([vs[i] * betas[i], kbs[i] * egs[i]], axis=1)) for i in range(n)]
    return [s[:, :dv] for s in sols], [s[:, dv:] for s in sols], qks, egs


def _conv_taps(u, ru_fn, cw):
    width = cw.shape[0]
    out = u * cw[width - 1:width, :]
    for j in range(1, width):
        out = out + ru_fn(j) * cw[width - 1 - j:width - j, :]
    return _silu(out)


def _gdn_finish(o, z, ga, gn):
    return (_rms(o) * gn * _silu(z.astype(F32)) * _sigmoid(ga.astype(F32))).astype(BF16)


def _head_scalars(bgc, h, heads):
    lane = lax.broadcasted_iota(jnp.int32, bgc.shape, 1)
    beta = jnp.sum(jnp.where(lane == h, bgc, 0.0), axis=-1, keepdims=True)
    gc = jnp.sum(jnp.where(lane == h + heads, bgc, 0.0), axis=-1, keepdims=True)
    return beta, gc


def _gdn_prompt_kernel(q_ref, k_ref, v_ref, z_ref, ga_ref, bgc_ref, cw_ref, gn_ref, mincl_ref,
                       o_ref, sfin_ref, tails_ref,
                       s_ref, tail_ref, u_s, wqe_s, qk_s, kd_s, el_s, *, heads_per_iter):
    c_id = pl.program_id(1)
    heads, rows, dk = q_ref.shape
    nchunk = rows // GDN_CHUNK
    cc = GDN_CHUNK

    @pl.when(c_id == 0)
    def _():
        s_ref[...] = jnp.zeros_like(s_ref)
        tail_ref[...] = jnp.zeros_like(tail_ref)

    row8 = lax.broadcasted_iota(jnp.int32, (SUBLANES, dk), 0)
    bgc = bgc_ref[...]
    mincl = mincl_ref[...]

    def conv(kind, h, u_ref):
        u = u_ref[h]
        tail = tail_ref[kind * heads + h]

        def delayed(j):
            head = jnp.where(row8 < j, pltpu.roll(tail, j, 0), pltpu.roll(u[:SUBLANES], j, 0))
            return jnp.concatenate([head, u_ref[h, pl.ds(SUBLANES - j, rows - SUBLANES), :]], axis=0)

        out = _conv_taps(u, delayed, cw_ref[kind * heads + h])
        tail_ref[kind * heads + h] = u[rows - SUBLANES:]
        return out

    def local_body(it, carry):
        hs, qs, ks, vs, betas, gcs = [], [], [], [], [], []
        for hh in range(heads_per_iter):
            h = it * heads_per_iter + hh
            q = _l2(conv(0, h, q_ref)) * (dk ** -0.5)
            k = _l2(conv(1, h, k_ref))
            v = conv(2, h, v_ref)
            beta, gc = _head_scalars(bgc, h, heads)
            for ci in range(nchunk):
                sl = slice(ci * cc, (ci + 1) * cc)
                hs.append((h, ci))
                qs.append(q[sl]); ks.append(k[sl]); vs.append(v[sl]); betas.append(beta[sl]); gcs.append(gc[sl])
        us, ws, qks, egs = _gdn_chunks_local(qs, ks, vs, betas, gcs, mincl, cc)
        for i, (h, ci) in enumerate(hs):
            g_last = gcs[i][cc - 1:cc, :]
            u_s[h, ci] = us[i]
            wqe_s[h, ci, :cc] = ws[i].astype(BF16)
            wqe_s[h, ci, cc:] = (qs[i] * egs[i]).astype(BF16)
            qk_s[h, ci] = qks[i].astype(BF16)
            kd_s[h, ci] = (ks[i] * jnp.exp(g_last - gcs[i])).astype(BF16)
            el_s[h, ci] = jnp.broadcast_to(jnp.exp(g_last), (SUBLANES, dk))
        return carry

    lax.fori_loop(0, heads // heads_per_iter, local_body, 0)

    states = [s_ref[h] for h in range(heads)]
    gn = gn_ref[...]
    for ci in range(nchunk):
        sl = slice(ci * cc, (ci + 1) * cc)
        wqs = [jnp.dot(wqe_s[h, ci], states[h].astype(BF16), preferred_element_type=F32) for h in range(heads)]
        v_news = [u_s[h, ci] - wqs[h][:cc] for h in range(heads)]
        vn16 = [v.astype(BF16) for v in v_news]
        outs = [wqs[h][cc:] + jnp.dot(qk_s[h, ci], vn16[h], preferred_element_type=F32) for h in range(heads)]
        states = [states[h] * el_s[h, ci][0:1, :]
                  + lax.dot_general(kd_s[h, ci], vn16[h], (((0,), (0,)), ((), ())), preferred_element_type=F32)
                  for h in range(heads)]
        for h in range(heads):
            o_ref[sl, h * dk:(h + 1) * dk] = _gdn_finish(outs[h], z_ref[h, sl, :], ga_ref[h, sl, :], gn)
    for h in range(heads):
        s_ref[h] = states[h]
        sfin_ref[h] = states[h]
    tails_ref[...] = tail_ref[...]


def _gdn_prompt(proj32, proj16, bgc, cw, gn, mincl, *, layer, batch, seq, heads, dk, col, rows, heads_per_iter):
    ncb = seq // rows
    nchunk = rows // GDN_CHUNK
    blk = lambda name: pl.BlockSpec((heads, rows, dk), lambda b, c, o=col[name]: (o, b * ncb + c, 0))
    return pl.pallas_call(
        functools.partial(_gdn_prompt_kernel, heads_per_iter=heads_per_iter),
        grid=(batch, ncb),
        in_specs=[
            blk("q"), blk("k"), blk("v"), blk("z"), blk("ga"),
            pl.BlockSpec((rows, LANES), lambda b, c: (b * ncb + c, 0)),
            pl.BlockSpec((None,) + cw.shape[1:], lambda b, c: (layer, 0, 0, 0)),
            pl.BlockSpec((None, 1, dk), lambda b, c: (layer, 0, 0)),
            pl.BlockSpec((GDN_CHUNK, GDN_CHUNK), lambda b, c: (0, 0)),
        ],
        out_specs=[
            pl.BlockSpec((rows, heads * dk), lambda b, c: (b * ncb + c, 0)),
            pl.BlockSpec((None, heads, dk, dk), lambda b, c: (b, 0, 0, 0)),
            pl.BlockSpec((None, 3 * heads, SUBLANES, dk), lambda b, c: (b, 0, 0, 0)),
        ],
        out_shape=[
            jax.ShapeDtypeStruct((batch * seq, heads * dk), BF16),
            jax.ShapeDtypeStruct((batch, heads, dk, dk), F32),
            jax.ShapeDtypeStruct((batch, 3 * heads, SUBLANES, dk), F32),
        ],
        scratch_shapes=[
            pltpu.VMEM((heads, dk, dk), F32),
            pltpu.VMEM((3 * heads, SUBLANES, dk), F32),
            pltpu.VMEM((heads, nchunk, GDN_CHUNK, dk), F32),
            pltpu.VMEM((heads, nchunk, 2 * GDN_CHUNK, dk), BF16),
            pltpu.VMEM((heads, nchunk, GDN_CHUNK, GDN_CHUNK), BF16),
            pltpu.VMEM((heads, nchunk, GDN_CHUNK, dk), BF16),
            pltpu.VMEM((heads, nchunk, SUBLANES, dk), F32),
        ],
        compiler_params=pltpu.CompilerParams(
            dimension_semantics=("parallel", "arbitrary"), vmem_limit_bytes=VMEM_LIMIT),
        name="gdn_prompt",
    )(proj32, proj32, proj32, proj16, proj16, bgc, cw, gn, mincl)


def _gdn_sample_kernel(q_ref, k_ref, v_ref, z_ref, ga_ref, bgc_ref, hq_ref, hk_ref, hv_ref,
                       cw_ref, gn_ref, mincl_ref, s0_ref, *rest, heads, seq):
    o_ref, sout_ref = rest[-2:]
    hpi, rows, dk = q_ref.shape
    nseq = rows // seq
    t_idx = lax.broadcasted_iota(jnp.int32, (nseq, seq, dk), 1)
    bgc = bgc_ref[...]

    def conv(kind, hh, h, u_ref, hist_ref):
        u = u_ref[hh]
        u3 = u.reshape(nseq, seq, dk)
        hist3 = hist_ref[:, hh * dk:(hh + 1) * dk].reshape(nseq, seq, dk)

        def delayed(j):
            xj = jnp.where(t_idx < j, pltpu.roll(hist3, j, 1), pltpu.roll(u3, j, 1))
            return xj.reshape(rows, dk)

        return _conv_taps(u, delayed, cw_ref[kind * heads + h])

    qs, ks, vs, betas, gcs = [], [], [], [], []
    for hh in range(hpi):
        h = pl.program_id(1) * hpi + hh
        qs.append(_l2(conv(0, hh, h, q_ref, hq_ref)) * (dk ** -0.5))
        ks.append(_l2(conv(1, hh, h, k_ref, hk_ref)))
        vs.append(conv(2, hh, h, v_ref, hv_ref))
        beta, gc = _head_scalars(bgc, h, heads)
        betas.append(beta)
        gcs.append(gc)
    us, ws, qks, egs = _gdn_chunks_local(qs, ks, vs, betas, gcs, mincl_ref[...], seq)

    pad = jnp.zeros((seq, dk), F32)
    pairs = [(hh, si) for hh in range(hpi) for si in range(nseq)]
    sl = lambda si: slice(si * seq, (si + 1) * seq)
    s0 = {p: s0_ref[p[1], p[0]] for p in pairs}
    qes = [qs[hh] * egs[hh] for hh in range(hpi)]
    wqs = {(hh, si): _dot(jnp.concatenate([ws[hh][sl(si)], qes[hh][sl(si)]], axis=0), s0[(hh, si)])
           for hh, si in pairs}
    v_news = {(hh, si): us[hh][sl(si)] - wqs[(hh, si)][:seq] for hh, si in pairs}
    for hh, si in pairs:
        g_last = gcs[hh][(si + 1) * seq - 1:(si + 1) * seq, :]
        kd = ks[hh][sl(si)] * jnp.exp(g_last - gcs[hh][sl(si)])
        sout_ref[si, hh] = s0[(hh, si)] * jnp.exp(g_last) + _dot_tn(
            jnp.concatenate([kd, pad], axis=0), jnp.concatenate([v_news[(hh, si)], pad], axis=0))
    gn = gn_ref[...]
    for hh in range(hpi):
        o = (jnp.concatenate([wqs[(hh, si)][seq:] for si in range(nseq)], axis=0)
             + _dot(qks[hh], jnp.concatenate([v_news[(hh, si)] for si in range(nseq)], axis=0)))
        o_ref[:, hh * dk:(hh + 1) * dk] = _gdn_finish(o, z_ref[hh], ga_ref[hh], gn)


def _gdn_sample(proj32, proj16, bgc, hist, s0, cw, gn, mincl, s_prev, *, layer, row0, batch, seq, heads, dk, col,
                heads_per_step):
    rows = GDN_CHUNK
    nseq = rows // seq
    r0 = row0 // rows
    depth = s0.shape[0]
    hpi = heads_per_step
    hgroups = heads // hpi
    blk = lambda name: pl.BlockSpec((hpi, rows, dk), lambda b, h, o=col[name]: (o * hgroups + h, r0 + b, 0))
    hblk = lambda name: pl.BlockSpec((None, rows, hpi * dk), lambda b, h, o=col[name]: (layer, b, o * hgroups + h))
    state_blk = pl.BlockSpec((None, nseq, hpi, dk, dk), lambda b, h: (layer, b, h, 0, 0))
    extra_in = [] if s_prev is None else [s_prev]
    return pl.pallas_call(
        functools.partial(_gdn_sample_kernel, heads=heads, seq=seq),
        grid=(batch * seq // rows, hgroups),
        in_specs=[
            blk("q"), blk("k"), blk("v"), blk("z"), blk("ga"),
            pl.BlockSpec((rows, LANES), lambda b, h: (r0 + b, 0)),
            hblk("q"), hblk("k"), hblk("v"),
            pl.BlockSpec((None,) + cw.shape[1:], lambda b, h: (layer, 0, 0, 0)),
            pl.BlockSpec((None, 1, dk), lambda b, h: (layer, 0, 0)),
            pl.BlockSpec((GDN_CHUNK, GDN_CHUNK), lambda b, h: (0, 0)),
            state_blk,
        ] + [pl.BlockSpec(memory_space=pl.ANY)] * len(extra_in),
        out_specs=[
            pl.BlockSpec((rows, hpi * dk), lambda b, h: (b, h)),
            state_blk,
        ],
        out_shape=[
            jax.ShapeDtypeStruct((batch * seq, heads * dk), BF16),
            jax.ShapeDtypeStruct((depth, batch, heads, dk, dk), F32),
        ],
        input_output_aliases={13: 1} if extra_in else {},
        compiler_params=pltpu.CompilerParams(
            dimension_semantics=("parallel", "parallel"), vmem_limit_bytes=VMEM_LIMIT),
        name="gdn_sample",
    )(proj32, proj32, proj32, proj16, proj16, bgc, hist, hist, hist, cw, gn, mincl, s0, *extra_in)


def _chunk_rms(x, bd, w):
    ms = _dot(x * x, bd)
    return x * lax.rsqrt(ms + EPS) * w


def _wide(ref, first, n):
    return jnp.concatenate([ref[first + i] for i in range(n)], axis=-1)


def _natural_blocks(slabs, hd):
    groups = len(slabs)
    per_block = LANES // hd
    lane = lax.broadcasted_iota(jnp.int32, (slabs[0].shape[0], LANES), 1)
    blocks = []
    for b in range(slabs[0].shape[1] * groups // LANES):
        blk = None
        for i in range(per_block):
            j, g = divmod(b * per_block + i, groups)
            first = (hd * j) // LANES * LANES
            piece = slabs[g][:, first:first + LANES]
            shift = (i * hd - hd * j) % LANES
            if shift:
                piece = pltpu.roll(piece, shift, 1)
            blk = piece if blk is None else jnp.where(lane < i * hd, blk, piece)
        blocks.append(blk)
    return blocks


def _swa_prompt_kernel(sink_ref, q_ref, kp_ref, kc_ref, vp_ref, vc_ref, gb_ref, qw_ref, kw_ref, bd_ref, bias_ref,
                       o_ref, kout_ref, vout_ref, *, kvh, hd, layer):
    blk = q_ref.shape[1]
    slab = kvh * hd
    spb = slab // LANES
    groups = q_ref.shape[0] // spb
    bd = bd_ref[...]
    kn = _chunk_rms(_wide(kc_ref, 0, spb).astype(F32), bd, kw_ref[...])
    kout_ref[...] = kn
    v_cur = _wide(vc_ref, 0, spb)
    vout_ref[...] = v_cur.astype(F32)
    kband = jnp.concatenate([_chunk_rms(_wide(kp_ref, 0, spb).astype(F32), bd, kw_ref[...]), kn],
                            axis=0).astype(BF16)
    vband = jnp.concatenate([_wide(vp_ref, 0, spb), v_cur], axis=0)
    lane_head = lax.broadcasted_iota(jnp.int32, (blk, slab), 1) // hd

    def scores(g):
        qn = _chunk_rms(_wide(q_ref, g * spb, spb).astype(F32), bd, qw_ref[...]) * (hd ** -0.5)
        qs = jnp.concatenate([jnp.where(lane_head == j, qn, 0.0) for j in range(kvh)], axis=0).astype(BF16)
        return lax.dot_general(qs, kband, (((1,), (1,)), ((), ())), preferred_element_type=F32)

    s_next = scores(0)
    gated = []
    for g in range(groups):
        s_all = s_next
        if g + 1 < groups:
            s_next = scores(g + 1)
        probs = []
        for j in range(kvh):
            p = g * kvh + j
            s = s_all[j * blk:(j + 1) * blk] + bias_ref[p]
            sink = sink_ref[layer, p]
            m = jnp.maximum(jnp.max(s, axis=-1, keepdims=True), sink)
            e = jnp.exp(s - m)
            den = jnp.sum(e, axis=-1, keepdims=True) + jnp.exp(sink - m)
            probs.append((e * (1.0 / den)).astype(BF16))
        pv = jnp.dot(jnp.concatenate(probs, axis=0), vband, preferred_element_type=F32)
        acc = pv[(kvh - 1) * blk:]
        for j in range(kvh - 2, -1, -1):
            acc = jnp.where(lane_head == j, pv[j * blk:(j + 1) * blk], acc)
        gated.append(acc * _sigmoid(_wide(gb_ref, g * spb, spb).astype(F32)))
    for b, blk_out in enumerate(_natural_blocks(gated, hd)):
        o_ref[:, b * LANES:(b + 1) * LANES] = blk_out.astype(BF16)


def _swa_prompt(proj, sinks, qw, kw, bd, bias, *, layer, batch, seq, window, kvh, hd, nq, col):
    nb = seq // window
    slab = kvh * hd
    nheads = nq // hd
    qb, sb = nq // LANES, slab // LANES
    cur = lambda name, n: pl.BlockSpec((n, window, LANES), lambda b, c, o=col[name] // n: (o, b * nb + c, 0))
    prev = lambda name, n: pl.BlockSpec(
        (n, window, LANES), lambda b, c, o=col[name] // n: (o, b * nb + jnp.maximum(c - 1, 0), 0))
    const2 = lambda b, c: (0, 0)
    lay3 = lambda b, c: (layer, 0, 0)
    return pl.pallas_call(
        functools.partial(_swa_prompt_kernel, kvh=kvh, hd=hd, layer=layer),
        grid=(batch, nb),
        in_specs=[
            pl.BlockSpec(memory_space=pltpu.SMEM),
            cur("sq", qb), prev("sk", sb), cur("sk", sb), prev("sv", sb), cur("sv", sb), cur("gb", qb),
            pl.BlockSpec((None, 1, slab), lay3),
            pl.BlockSpec((None, 1, slab), lay3),
            pl.BlockSpec((slab, slab), const2),
            pl.BlockSpec((None, nheads, window, 2 * window), lambda b, c: (jnp.minimum(c, 1), 0, 0, 0)),
        ],
        out_specs=[
            pl.BlockSpec((window, nq), lambda b, c: (b * nb + c, 0)),
            pl.BlockSpec((None, window, slab), lambda b, c: (b, 0, 0)),
            pl.BlockSpec((None, window, slab), lambda b, c: (b, 0, 0)),
        ],
        out_shape=[
            jax.ShapeDtypeStruct((batch * seq, nq), BF16),
            jax.ShapeDtypeStruct((batch, window, slab), F32),
            jax.ShapeDtypeStruct((batch, window, slab), F32),
        ],
        compiler_params=pltpu.CompilerParams(
            dimension_semantics=("parallel", "arbitrary"), vmem_limit_bytes=VMEM_LIMIT),
        name="swa_prompt",
    )(sinks, proj, proj, proj, proj, proj, proj, qw, kw, bd, bias)


def _swa_sample_kernel(q_ref, k_ref, v_ref, gb_ref, ck_ref, cv_ref, qw_ref, kw_ref, bd_ref, bias_ref, sink_ref,
                       *rest, kvh, hd, seq):
    o_ref, kout_ref, vout_ref = rest[-3:]
    rows = q_ref.shape[1]
    nseq = rows // seq
    slab = kvh * hd
    spb = slab // LANES
    groups = q_ref.shape[0] // spb
    window = ck_ref.shape[1]
    nkeys = bias_ref.shape[1]
    bd = bd_ref[...]
    kn = _chunk_rms(_wide(k_ref, 0, spb).astype(F32), bd, kw_ref[...]).reshape(nseq, seq, slab)
    vn = _wide(v_ref, 0, spb).astype(F32).reshape(nseq, seq, slab)
    ck = ck_ref[...]
    cv = cv_ref[...]
    kout_ref[...] = jnp.concatenate([ck[:, seq:, :], kn], axis=1)
    vout_ref[...] = jnp.concatenate([cv[:, seq:, :], vn], axis=1)
    zpad = jnp.zeros((nseq, nkeys - window - seq, slab), F32)
    k_all = jnp.concatenate([ck, kn, zpad], axis=1).astype(BF16)
    v_all = jnp.concatenate([cv, vn, zpad], axis=1).astype(BF16)

    lane_head = lax.broadcasted_iota(jnp.int32, (rows, slab), 1) // hd
    pieces = []
    for g in range(groups):
        qn = _chunk_rms(_wide(q_ref, g * spb, spb).astype(F32), bd, qw_ref[...]) * (hd ** -0.5)
        for j in range(kvh):
            pieces.append(jnp.where(lane_head == j, qn, 0.0).reshape(nseq, seq, slab))
    q_all = jnp.concatenate(pieces, axis=1).astype(BF16)
    s = jnp.einsum("bqd,bkd->bqk", q_all, k_all, preferred_element_type=F32) + bias_ref[...][None]
    sink = sink_ref[...][None]
    m = jnp.maximum(jnp.max(s, axis=-1, keepdims=True), sink)
    e = jnp.exp(s - m)
    den = jnp.sum(e, axis=-1, keepdims=True) + jnp.exp(sink - m)
    pv = jnp.einsum("bqk,bkd->bqd", (e * (1.0 / den)).astype(BF16), v_all, preferred_element_type=F32)
    lane_head3 = lax.broadcasted_iota(jnp.int32, (nseq, seq, slab), 2) // hd
    gated = []
    for g in range(groups):
        acc = jnp.zeros((nseq, seq, slab), F32)
        for j in range(kvh):
            p = g * kvh + j
            acc = acc + jnp.where(lane_head3 == j, pv[:, p * seq:(p + 1) * seq, :], 0.0)
        gated.append(acc.reshape(rows, slab) * _sigmoid(_wide(gb_ref, g * spb, spb).astype(F32)))
    for b, blk_out in enumerate(_natural_blocks(gated, hd)):
        o_ref[:, b * LANES:(b + 1) * LANES] = blk_out.astype(BF16)


def _swa_sample(proj, cache_k, cache_v, qw, kw, bd, bias, sink_col, kv_prev, *, layer, row0, batch, seq, kvh, hd,
                nq, col, nseq):
    rows = nseq * seq
    slab = kvh * hd
    depth, _, window, _ = cache_k.shape
    r0 = row0 // rows
    cache_blk = pl.BlockSpec((None, nseq, window, slab), lambda b: (layer, b, 0, 0))
    extra_in = [] if kv_prev is None else list(kv_prev)
    qb, sb = nq // LANES, slab // LANES
    cur = lambda name, n: pl.BlockSpec((n, rows, LANES), lambda b, o=col[name] // n: (o, r0 + b, 0))
    const2 = lambda b: (0, 0)
    lay3 = lambda b: (layer, 0, 0)
    return pl.pallas_call(
        functools.partial(_swa_sample_kernel, kvh=kvh, hd=hd, seq=seq),
        grid=(batch // nseq,),
        in_specs=[
            cur("sq", qb), cur("sk", sb), cur("sv", sb), cur("gb", qb),
            cache_blk, cache_blk,
            pl.BlockSpec((None, 1, slab), lay3),
            pl.BlockSpec((None, 1, slab), lay3),
            pl.BlockSpec((slab, slab), const2),
            pl.BlockSpec(bias.shape, const2),
            pl.BlockSpec((None,) + sink_col.shape[1:], lay3),
        ] + [pl.BlockSpec(memory_space=pl.ANY)] * len(extra_in),
        out_specs=[
            pl.BlockSpec((rows, nq), lambda b: (b, 0)),
            cache_blk, cache_blk,
        ],
        out_shape=[
            jax.ShapeDtypeStruct((batch * seq, nq), BF16),
            jax.ShapeDtypeStruct((depth, batch, window, slab), F32),
            jax.ShapeDtypeStruct((depth, batch, window, slab), F32),
        ],
        input_output_aliases={11: 1, 12: 2} if extra_in else {},
        compiler_params=pltpu.CompilerParams(
            dimension_semantics=("parallel",), vmem_limit_bytes=VMEM_LIMIT),
        name="swa_sample",
    )(proj, proj, proj, proj, cache_k, cache_v, qw, kw, bd, bias, sink_col, *extra_in)


def _dense_kernel(hp_ref, hs_ref, oap_ref, oas_ref, obp_ref, obs_ref, pp_ref, ps_ref, wo_ref, nf_ref,
                  wu_ref, wd_ref, np_ref, wg_ref, wp_ref, outp_ref, outs_ref, *, npt, ff_chunk):
    is_prompt = pl.program_id(0) < npt
    oa = jnp.where(is_prompt, oap_ref[...], oas_ref[...])
    ob = jnp.where(is_prompt, obp_ref[...], obs_ref[...])
    h = jnp.where(is_prompt, hp_ref[...], hs_ref[...])
    h = h + jnp.dot(oa + ob, wo_ref[...], preferred_element_type=F32)
    xn = (_rms(h) * nf_ref[...]).astype(BF16)
    acc = jnp.zeros_like(h)
    for c in range(wu_ref.shape[1] // ff_chunk):
        sl = slice(c * ff_chunk, (c + 1) * ff_chunk)
        hid = jnp.maximum(jnp.dot(xn, wu_ref[:, sl], preferred_element_type=F32), 0.0)
        acc = acc + jnp.dot((hid * hid).astype(BF16), wd_ref[sl, :], preferred_element_type=F32)
    h = h + acc
    xn = (_rms(h) * np_ref[...]).astype(BF16)
    gate = _sigmoid(jnp.dot(xn, wg_ref[...], preferred_element_type=F32))
    p = jnp.where(is_prompt, pp_ref[...], ps_ref[...])
    pe = jnp.dot(p.astype(BF16), wp_ref[...], preferred_element_type=F32)
    out = h + gate * pe

    @pl.when(is_prompt)
    def _():
        outp_ref[...] = out

    @pl.when(jnp.logical_not(is_prompt))
    def _():
        outs_ref[...] = out


def _dense(h_p, h_s, oa_p, oa_s, ob_p, ob_s, p_p, p_s, wo, nf, wu, wd, npl, wg, wp, *, layer, tm):
    tp, d = h_p.shape
    ts = h_s.shape[0]
    npt = tp // tm
    rows_p = lambda m: (jnp.minimum(m, npt - 1), 0)
    rows_s = lambda m: (jnp.maximum(m - npt, 0), 0)
    resident = lambda a: pl.BlockSpec((None,) + a.shape[1:], lambda m: (layer, 0, 0), pipeline_mode=pl.Buffered(1))
    return pl.pallas_call(
        functools.partial(_dense_kernel, npt=npt, ff_chunk=1024),
        grid=((tp + ts) // tm,),
        in_specs=[
            pl.BlockSpec((tm, d), rows_p), pl.BlockSpec((tm, d), rows_s),
            pl.BlockSpec((tm, d), rows_p), pl.BlockSpec((tm, d), rows_s),
            pl.BlockSpec((tm, d), rows_p), pl.BlockSpec((tm, d), rows_s),
            pl.BlockSpec((None, tm, p_p.shape[2]), lambda m: (layer, jnp.minimum(m, npt - 1), 0)),
            pl.BlockSpec((None, tm, p_s.shape[2]), lambda m: (layer, jnp.maximum(m - npt, 0), 0)),
            resident(wo), resident(nf), resident(wu), resident(wd),
            resident(npl), resident(wg), resident(wp),
        ],
        out_specs=[pl.BlockSpec((tm, d), rows_p), pl.BlockSpec((tm, d), rows_s)],
        out_shape=[jax.ShapeDtypeStruct((tp, d), F32), jax.ShapeDtypeStruct((ts, d), F32)],
        compiler_params=pltpu.CompilerParams(
            dimension_semantics=("arbitrary",), vmem_limit_bytes=VMEM_LIMIT),
        name="dense",
    )(h_p, h_s, oa_p, oa_s, ob_p, ob_s, p_p, p_s, wo, nf, wu, wd, npl, wg, wp)


def _seq_mask(seq_len):
    i = np.arange(GDN_CHUNK)
    m = (i[:, None] // seq_len == i[None, :] // seq_len) & (i[None, :] <= i[:, None])
    return m.astype(np.float32)


def kernel(x_prompt, x_sample, cache_conv, state_gdn, cache_swa_k, cache_swa_v, p_prompt, p_sample, norm_mix, w_in, conv_w, a_log, dt_bias, gdn_norm, q_norm, k_norm, attn_sinks, w_out, norm_ffn, w_up, w_down, norm_ple, w_ple_gate, w_ple_proj):
    batch, seq, d = x_prompt.shape
    dbatch, dseq, _ = x_sample.shape
    depth = w_in.shape[0]
    heads, dk, dv = state_gdn.shape[2:]
    window, kvh, hd = cache_swa_k.shape[2:]
    nq = d
    nheads = nq // hd
    groups = nheads // kvh
    slab = kvh * hd
    key_dim = heads * dk
    val_dim = heads * dv
    conv_dim = 2 * key_dim + val_dim
    width = conv_w.shape[1]
    tp, ts = batch * seq, dbatch * dseq
    assert dk == LANES and dv == LANES and val_dim == d and key_dim == d and slab % LANES == 0
    assert seq % window == 0 and window == GDN_CHUNK and GDN_CHUNK % dseq == 0 and dseq >= width - 1
    assert dseq == SUBLANES and ts % GDN_CHUNK == 0

    o_z = conv_dim
    o_b = o_z + val_dim
    o_a = o_b + heads
    o_sq = o_a + heads
    o_sk = o_sq + nq
    o_sv = o_sk + slab
    o_g = o_sv + slab
    perm = np.concatenate([np.arange(hd) + (j * groups + g) * hd for g in range(groups) for j in range(kvh)])
    col_h = {"q": 0, "k": 1, "v": 2, "z": 0, "ga": 2}
    off = {"sq": val_dim, "gb": val_dim + nq + d, "sk": val_dim + nq + 2 * d, "sv": val_dim + nq + 2 * d + slab}
    col_b = {name: o // LANES for name, o in off.items()}
    cb = conv_dim // LANES

    h_p = x_prompt.reshape(tp, d)
    h_s = x_sample.reshape(ts, d)
    bf = lambda a: a.astype(BF16)
    wo_all, wu_all, wd_all = bf(w_out), bf(w_up), bf(w_down)
    wg_all, wp_all = bf(w_ple_gate), bf(w_ple_proj)
    nf_all, np_all = norm_ffn[:, None, :], norm_ple[:, None, :]
    p_p = p_prompt.reshape(depth, tp, -1)
    p_s = p_sample.reshape(depth, ts, -1)

    masks = jnp.asarray(np.stack([_seq_mask(GDN_CHUNK), _seq_mask(dseq)]), BF16)
    mincl_p = jnp.asarray(_seq_mask(GDN_CHUNK), F32)
    mincl_s = jnp.asarray(_seq_mask(dseq), F32)
    bd = jnp.asarray(np.kron(np.eye(kvh), np.full((hd, hd), 1.0 / hd)), BF16)

    head_of = np.array([(p % kvh) * groups + p // kvh for p in range(nheads)])
    slopes = jnp.exp2(-8.0 * (jnp.asarray(head_of, F32) + 1.0) / nheads)
    qi = np.arange(window)[:, None]
    kj = np.arange(2 * window)[None, :]
    dist = window + qi - kj
    ok = (dist >= 0) & (dist <= window)
    ok_first = ok & (kj >= window)
    dist_f = jnp.asarray(dist, F32)
    bias_p = jnp.stack([
        jnp.where(jnp.asarray(okv)[None], -slopes[:, None, None] * dist_f[None], -jnp.inf) for okv in (ok_first, ok)])
    nkeys = -(-(window + dseq) // 16) * 16
    ti = np.tile(np.arange(dseq), nheads)[:, None]
    sj = np.arange(nkeys)[None, :]
    dist_s = ti + window - sj
    ok_s = (dist_s >= 0) & (dist_s <= window) & (sj < window + dseq)
    slopes_rows = jnp.repeat(slopes, dseq)[:, None]
    bias_s = jnp.where(jnp.asarray(ok_s), -slopes_rows * jnp.asarray(dist_s, F32), -jnp.inf)

    hist_all = jnp.pad(cache_conv, ((0, 0), (0, 0), (dseq - cache_conv.shape[2], 0), (0, 0)))
    hist_all = hist_all.reshape(depth, ts, conv_dim)
    ck_all = cache_swa_k.reshape(depth, dbatch, window, slab)
    cv_all = cache_swa_v.reshape(depth, dbatch, window, slab)
    cw_all = jnp.transpose(conv_w.reshape(depth, width, cb, LANES), (0, 2, 1, 3))

    avec_all = (jnp.zeros((depth, 2, LANES), F32).at[:, 0, heads:2 * heads].set(a_log)
                .at[:, 1, heads:2 * heads].set(dt_bias))
    nm_all, gn_all = norm_mix[:, None, :], gdn_norm[:, None, :]
    sinks_all = attn_sinks[:, head_of]
    sink_col_all = jnp.repeat(sinks_all, dseq, axis=1)[:, :, None]
    qw_all = jnp.tile(q_norm, (1, kvh))[:, None, :]
    kw_all = jnp.tile(k_norm, (1, kvh))[:, None, :]
    wba_all = jnp.pad(w_in[:, :, o_b:o_sq], ((0, 0), (0, 0), (0, LANES - 2 * heads))).astype(BF16)

    outs = {n: [] for n in ("conv_p", "gdn_p", "k_p", "v_p", "conv_s")}
    gdn_s = None
    kv_s = None
    for i in range(depth):
        wi = w_in[i]
        w_main = jnp.concatenate([
            wi[:, :o_b], wi[:, o_sq:o_sk][:, perm], wi[:, o_g:o_g + d], wi[:, o_g + d:][:, perm],
            wi[:, o_sk:o_g]], axis=1).astype(BF16)
        proj32, proj16, bgc = _inproj(h_p, h_s, nm_all, w_main, wba_all[i], avec_all, masks, layer=i, heads=heads,
                                      tm=1024, tn=1536, n_f32=conv_dim)

        oa_p, gdn_p, tails = _gdn_prompt(proj32, proj16, bgc, cw_all, gn_all, mincl_p, layer=i, batch=batch, seq=seq,
                                         heads=heads, dk=dk, col=col_h, rows=512, heads_per_iter=4)
        oa_s, gdn_s = _gdn_sample(proj32, proj16, bgc, hist_all, state_gdn, cw_all, gn_all, mincl_s, gdn_s, layer=i,
                                  row0=tp, batch=dbatch, seq=dseq, heads=heads, dk=dk, col=col_h, heads_per_step=4)

        ob_p, k_p, v_p = _swa_prompt(proj16, sinks_all, qw_all, kw_all, bd, bias_p, layer=i, batch=batch, seq=seq,
                                     window=window, kvh=kvh, hd=hd, nq=nq, col=col_b)
        ob_s, *kv_s = _swa_sample(proj16, ck_all, cv_all, qw_all, kw_all, bd, bias_s, sink_col_all, kv_s,
                                  layer=i, row0=tp, batch=dbatch, seq=dseq, kvh=kvh, hd=hd, nq=nq, col=col_b, nseq=8)

        h_p, h_s = _dense(h_p, h_s, oa_p, oa_s, ob_p, ob_s, p_p, p_s, wo_all, nf_all, wu_all, wd_all, np_all,
                          wg_all, wp_all, layer=i, tm=512)

        outs["conv_p"].append(jnp.transpose(tails[:, :, SUBLANES - (width - 1):, :], (0, 2, 1, 3))
                              .reshape(batch, width - 1, conv_dim))
        conv_s = lax.slice(proj32, (0, tp, 0), (cb, tp + ts, LANES)).reshape(cb, dbatch, dseq, LANES)
        outs["conv_s"].append(
            jnp.transpose(conv_s[:, :, dseq - (width - 1):], (1, 2, 0, 3)).reshape(dbatch, width - 1, conv_dim))
        outs["gdn_p"].append(gdn_p)
        outs["k_p"].append(k_p.reshape(batch, window, kvh, hd))
        outs["v_p"].append(v_p.reshape(batch, window, kvh, hd))

    st = lambda n: jnp.stack(outs[n])
    return (h_p.reshape(batch, seq, d), h_s.reshape(dbatch, dseq, d),
            st("conv_p"), st("gdn_p"), st("k_p"), st("v_p"), st("conv_s"), gdn_s,
            kv_s[0].reshape(depth, dbatch, window, kvh, hd), kv_s[1].reshape(depth, dbatch, window, kvh, hd))
```

```python
import functools

import numpy as np
import jax
import jax.numpy as jnp
from jax import lax
from jax.experimental import pallas as pl
from jax.experimental.pallas import tpu as pltpu

F32 = jnp.float32
BF16 = jnp.bfloat16
EPS = 1e-6
LANES = 128
SUBLANES = 8
MXU_N = 256
GDN_CHUNK = 128
VMEM_LIMIT = 56 * 1024 * 1024


def _sigmoid(x):
    return 0.5 * jnp.tanh(0.5 * x) + 0.5


def _silu(x):
    hx = 0.5 * x
    return hx + hx * jnp.tanh(hx)


def _dot(a, b):
    return jnp.dot(a.astype(BF16), b.astype(BF16), preferred_element_type=F32)


def _dot_nt(a, b):
    return lax.dot_general(a.astype(BF16), b.astype(BF16), (((1,), (1,)), ((), ())), preferred_element_type=F32)


def _dot_tn(a, b):
    return lax.dot_general(a.astype(BF16), b.astype(BF16), (((0,), (0,)), ((), ())), preferred_element_type=F32)


def _rms(x):
    return x * lax.rsqrt(jnp.mean(x * x, axis=-1, keepdims=True) + EPS)


def _l2(x):
    return x * lax.rsqrt(jnp.sum(x * x, axis=-1, keepdims=True) + EPS)


def _inproj_kernel(xp_ref, xs_ref, nw_ref, w_ref, wba_ref, avec_ref, mask_ref, proj32_ref, proj16_ref, bgc_ref,
                   xn_ref, *, heads, npt, n32):
    n = pl.program_id(1)

    @pl.when(n == 0)
    def _():
        x = jnp.where(pl.program_id(0) < npt, xp_ref[...], xs_ref[...])
        xn = (_rms(x) * nw_ref[...]).astype(BF16)
        xn_ref[...] = xn
        ba = jnp.dot(xn, wba_ref[...], preferred_element_type=F32)
        beta = _sigmoid(ba)
        xs = ba + avec_ref[1:2, :]
        softplus = jnp.maximum(xs, 0.0) + jnp.log1p(jnp.exp(-jnp.abs(xs)))
        g = -jnp.exp(avec_ref[0:1, :]) * softplus
        g1 = g.astype(BF16)
        r1 = g - g1.astype(F32)
        g2 = r1.astype(BF16)
        g3 = (r1 - g2.astype(F32)).astype(BF16)
        m = mask_ref[...]
        rows = g.shape[0]
        lane = lax.broadcasted_iota(jnp.int32, (GDN_CHUNK, LANES), 1)
        for r in range(rows // GDN_CHUNK):
            sl = slice(r * GDN_CHUNK, (r + 1) * GDN_CHUNK)
            gc = (jnp.dot(m, g1[sl], preferred_element_type=F32)
                  + jnp.dot(m, g2[sl], preferred_element_type=F32)
                  + jnp.dot(m, g3[sl], preferred_element_type=F32))
            bgc_ref[sl, :] = jnp.where(lane < heads, beta[sl], gc)

    def project(out_ref):
        xn = xn_ref[...]
        for j in range(w_ref.shape[1] // MXU_N):
            r = jnp.dot(xn, w_ref[:, j * MXU_N:(j + 1) * MXU_N], preferred_element_type=F32)
            for i in range(MXU_N // LANES):
                out_ref[j * (MXU_N // LANES) + i] = r[:, i * LANES:(i + 1) * LANES].astype(out_ref.dtype)

    @pl.when(n < n32)
    def _():
        project(proj32_ref)

    @pl.when(n >= n32)
    def _():
        project(proj16_ref)


def _inproj(h_p, h_s, nw, w, wba, avec, masks, *, layer, heads, tm, tn, n_f32):
    d = h_p.shape[1]
    t_all = h_p.shape[0] + h_s.shape[0]
    n_out = w.shape[2]
    npt = h_p.shape[0] // tm
    n32 = n_f32 // tn
    assert n32 * tn == n_f32
    return pl.pallas_call(
        functools.partial(_inproj_kernel, heads=heads, npt=npt, n32=n32),
        grid=(t_all // tm, n_out // tn),
        in_specs=[
            pl.BlockSpec((tm, d), lambda m, n: (jnp.minimum(m, npt - 1), 0)),
            pl.BlockSpec((tm, d), lambda m, n: (jnp.maximum(m - npt, 0), 0)),
            pl.BlockSpec((None, 1, d), lambda m, n: (layer, 0, 0)),
            pl.BlockSpec((None, d, tn), lambda m, n: (layer, 0, n)),
            pl.BlockSpec((None, d, LANES), lambda m, n: (layer, 0, 0)),
            pl.BlockSpec((None, 2, LANES), lambda m, n: (layer, 0, 0)),
            pl.BlockSpec((None, GDN_CHUNK, GDN_CHUNK), lambda m, n: (jnp.where(m >= npt, 1, 0), 0, 0)),
        ],
        out_specs=[
            pl.BlockSpec((tn // LANES, tm, LANES), lambda m, n: (jnp.minimum(n, n32 - 1), m, 0)),
            pl.BlockSpec((tn // LANES, tm, LANES), lambda m, n: (jnp.maximum(n - n32, 0), m, 0)),
            pl.BlockSpec((tm, LANES), lambda m, n: (m, 0)),
        ],
        out_shape=[
            jax.ShapeDtypeStruct((n_f32 // LANES, t_all, LANES), F32),
            jax.ShapeDtypeStruct(((n_out - n_f32) // LANES, t_all, LANES), BF16),
            jax.ShapeDtypeStruct((t_all, LANES), F32),
        ],
        scratch_shapes=[pltpu.VMEM((tm, d), BF16)],
        compiler_params=pltpu.CompilerParams(
            dimension_semantics=("parallel", "arbitrary"), vmem_limit_bytes=VMEM_LIMIT),
        name="inproj",
    )(h_p, h_s, nw, w, wba, avec, masks)


def _gdn_chunks_local(qs, ks, vs, betas, gcs, mincl, seq_len):
    n = len(qs)
    c = qs[0].shape[0]
    dv = vs[0].shape[1]
    row = lax.broadcasted_iota(jnp.int32, (c, c), 0)
    col = lax.broadcasted_iota(jnp.int32, (c, c), 1)
    decs, egs = [], []
    for gc in gcs:
        gcb = jnp.broadcast_to(gc, (c, c))
        decs.append(jnp.exp(jnp.where(mincl > 0, gcb - gcb.T, -jnp.inf)))
        egs.append(jnp.exp(gc))
    kbs = [ks[i] * betas[i] for i in range(n)]
    kks = [_dot_nt(jnp.concatenate([kbs[i], qs[i]], axis=0), ks[i]) for i in range(n)]
    a_s = [jnp.where(row == col, 0.0, kks[i][:c] * decs[i]) for i in range(n)]
    qks = [kks[i][c:] * decs[i] for i in range(n)]
    same = (row >> 1) == (col >> 1)
    eye = jnp.where(row == col, 1.0, 0.0)
    ts = [eye - jnp.where(same, a, 0.0) for a in a_s]
    na16 = [(-a).astype(BF16) for a in a_s]
    for lvl in range(1, seq_len.bit_length() - 1):
        wider = (row >> (lvl + 1)) == (col >> (lvl + 1))
        sel = wider & jnp.logical_not(same)
        t16 = [t.astype(BF16) for t in ts]
        mids = [jnp.dot(na16[i], t16[i], preferred_element_type=F32) for i in range(n)]
        ts = [jnp.where(sel, jnp.dot(t16[i], mids[i].astype(BF16), preferred_element_type=F32), ts[i])
              for i in range(n)]
        same = wider
    sols = [_dot(ts[i], jnp.concatenate([vs[i] * betas[i], kbs[i] * egs[i]], axis=1)) for i in range(n)]
    return [s[:, :dv] for s in sols], [s[:, dv:] for s in sols], qks, egs


def _conv_taps(u, ru_fn, cw):
    width = cw.shape[0]
    out = u * cw[width - 1:width, :]
    for j in range(1, width):
        out = out + ru_fn(j) * cw[width - 1 - j:width - j, :]
    return _silu(out)


def _gdn_finish(o, z, ga, gn):
    return (_rms(o) * gn * _silu(z.astype(F32)) * _sigmoid(ga.astype(F32))).astype(BF16)


def _head_scalars(bgc, h, heads):
    lane = lax.broadcasted_iota(jnp.int32, bgc.shape, 1)
    beta = jnp.sum(jnp.where(lane == h, bgc, 0.0), axis=-1, keepdims=True)
    gc = jnp.sum(jnp.where(lane == h + heads, bgc, 0.0), axis=-1, keepdims=True)
    return beta, gc


def _gdn_prompt_kernel(q_ref, k_ref, v_ref, z_ref, ga_ref, bgc_ref, cw_ref, gn_ref, mincl_ref,
                       o_ref, sfin_ref, tails_ref,
                       s_ref, tail_ref, u_s, wqe_s, qk_s, kd_s, el_s, *, heads_per_iter):
    c_id = pl.program_id(1)
    heads, rows, dk = q_ref.shape
    nchunk = rows // GDN_CHUNK
    cc = GDN_CHUNK

    @pl.when(c_id == 0)
    def _():
        s_ref[...] = jnp.zeros_like(s_ref)
        tail_ref[...] = jnp.zeros_like(tail_ref)

    row8 = lax.broadcasted_iota(jnp.int32, (SUBLANES, dk), 0)
    bgc = bgc_ref[...]
    mincl = mincl_ref[...]

    def conv(kind, h, u_ref):
        u = u_ref[h]
        tail = tail_ref[kind * heads + h]

        def delayed(j):
            head = jnp.where(row8 < j, pltpu.roll(tail, j, 0), pltpu.roll(u[:SUBLANES], j, 0))
            return jnp.concatenate([head, u_ref[h, pl.ds(SUBLANES - j, rows - SUBLANES), :]], axis=0)

        out = _conv_taps(u, delayed, cw_ref[kind * heads + h])
        tail_ref[kind * heads + h] = u[rows - SUBLANES:]
        return out

    def local_body(it, carry):
        hs, qs, ks, vs, betas, gcs = [], [], [], [], [], []
        for hh in range(heads_per_iter):
            h = it * heads_per_iter + hh
            q = _l2(conv(0, h, q_ref)) * (dk ** -0.5)
            k = _l2(conv(1, h, k_ref))
            v = conv(2, h, v_ref)
            beta, gc = _head_scalars(bgc, h, heads)
            for ci in range(nchunk):
                sl = slice(ci * cc, (ci + 1) * cc)
                hs.append((h, ci))
                qs.append(q[sl]); ks.append(k[sl]); vs.append(v[sl]); betas.append(beta[sl]); gcs.append(gc[sl])
        us, ws, qks, egs = _gdn_chunks_local(qs, ks, vs, betas, gcs, mincl, cc)
        for i, (h, ci) in enumerate(hs):
            g_last = gcs[i][cc - 1:cc, :]
            u_s[h, ci] = us[i]
            wqe_s[h, ci, :cc] = ws[i].astype(BF16)
            wqe_s[h, ci, cc:] = (qs[i] * egs[i]).astype(BF16)
            qk_s[h, ci] = qks[i].astype(BF16)
            kd_s[h, ci] = (ks[i] * jnp.exp(g_last - gcs[i])).astype(BF16)
            el_s[h, ci] = jnp.broadcast_to(jnp.exp(g_last), (SUBLANES, dk))
        return carry

    lax.fori_loop(0, heads // heads_per_iter, local_body, 0)

    states = [s_ref[h] for h in range(heads)]
    gn = gn_ref[...]
    for ci in range(nchunk):
        sl = slice(ci * cc, (ci + 1) * cc)
        wqs = [jnp.dot(wqe_s[h, ci], states[h].astype(BF16), preferred_element_type=F32) for h in range(heads)]
        v_news = [u_s[h, ci] - wqs[h][:cc] for h in range(heads)]
        vn16 = [v.astype(BF16) for v in v_news]
        outs = [wqs[h][cc:] + jnp.dot(qk_s[h, ci], vn16[h], preferred_element_type=F32) for h in range(heads)]
        states = [states[h] * el_s[h, ci][0:1, :]
                  + lax.dot_general(kd_s[h, ci], vn16[h], (((0,), (0,)), ((), ())), preferred_element_type=F32)
                  for h in range(heads)]
        for h in range(heads):
            o_ref[sl, h * dk:(h + 1) * dk] = _gdn_finish(outs[h], z_ref[h, sl, :], ga_ref[h, sl, :], gn)
    for h in range(heads):
        s_ref[h] = states[h]
        sfin_ref[h] = states[h]
    for idx in range(3 * heads):
        tails_ref[:, idx * dk:(idx + 1) * dk] = tail_ref[idx]


def _gdn_prompt(proj32, proj16, bgc, cw, gn, mincl, *, layer, batch, seq, heads, dk, col, rows, heads_per_iter):
    ncb = seq // rows
    nchunk = rows // GDN_CHUNK
    blk = lambda name: pl.BlockSpec((heads, rows, dk), lambda b, c, o=col[name]: (o, b * ncb + c, 0))
    return pl.pallas_call(
        functools.partial(_gdn_prompt_kernel, heads_per_iter=heads_per_iter),
        grid=(batch, ncb),
        in_specs=[
            blk("q"), blk("k"), blk("v"), blk("z"), blk("ga"),
            pl.BlockSpec((rows, LANES), lambda b, c: (b * ncb + c, 0)),
            pl.BlockSpec((None,) + cw.shape[1:], lambda b, c: (layer, 0, 0, 0)),
            pl.BlockSpec((None, 1, dk), lambda b, c: (layer, 0, 0)),
            pl.BlockSpec((GDN_CHUNK, GDN_CHUNK), lambda b, c: (0, 0)),
        ],
        out_specs=[
            pl.BlockSpec((rows, heads * dk), lambda b, c: (b * ncb + c, 0)),
            pl.BlockSpec((None, heads, dk, dk), lambda b, c: (b, 0, 0, 0)),
            pl.BlockSpec((None, SUBLANES, 3 * heads * dk), lambda b, c: (b, 0, 0)),
        ],
        out_shape=[
            jax.ShapeDtypeStruct((batch * seq, heads * dk), BF16),
            jax.ShapeDtypeStruct((batch, heads, dk, dk), F32),
            jax.ShapeDtypeStruct((batch, SUBLANES, 3 * heads * dk), F32),
        ],
        scratch_shapes=[
            pltpu.VMEM((heads, dk, dk), F32),
            pltpu.VMEM((3 * heads, SUBLANES, dk), F32),
            pltpu.VMEM((heads, nchunk, GDN_CHUNK, dk), F32),
            pltpu.VMEM((heads, nchunk, 2 * GDN_CHUNK, dk), BF16),
            pltpu.VMEM((heads, nchunk, GDN_CHUNK, GDN_CHUNK), BF16),
            pltpu.VMEM((heads, nchunk, GDN_CHUNK, dk), BF16),
            pltpu.VMEM((heads, nchunk, SUBLANES, dk), F32),
        ],
        compiler_params=pltpu.CompilerParams(
            dimension_semantics=("parallel", "arbitrary"), vmem_limit_bytes=VMEM_LIMIT),
        name="gdn_prompt",
    )(proj32, proj32, proj32, proj16, proj16, bgc, cw, gn, mincl)


def _gdn_sample_kernel(q_ref, k_ref, v_ref, z_ref, ga_ref, bgc_ref, hq_ref, hk_ref, hv_ref,
                       cw_ref, gn_ref, mincl_ref, s0_ref, *rest, heads, seq):
    o_ref, sout_ref, raw_ref = rest[-3:]
    hpi, rows, dk = q_ref.shape
    nseq = rows // seq
    t_idx = lax.broadcasted_iota(jnp.int32, (nseq, seq, dk), 1)
    bgc = bgc_ref[...]

    def conv(kind, hh, h, u_ref, hist_ref):
        u = u_ref[hh]
        raw_ref[kind, :, hh * dk:(hh + 1) * dk] = u
        u3 = u.reshape(nseq, seq, dk)
        hist3 = hist_ref[:, hh * dk:(hh + 1) * dk].reshape(nseq, seq, dk)

        def delayed(j):
            xj = jnp.where(t_idx < j, pltpu.roll(hist3, j, 1), pltpu.roll(u3, j, 1))
            return xj.reshape(rows, dk)

        return _conv_taps(u, delayed, cw_ref[kind * heads + h])

    qs, ks, vs, betas, gcs = [], [], [], [], []
    for hh in range(hpi):
        h = pl.program_id(1) * hpi + hh
        qs.append(_l2(conv(0, hh, h, q_ref, hq_ref)) * (dk ** -0.5))
        ks.append(_l2(conv(1, hh, h, k_ref, hk_ref)))
        vs.append(conv(2, hh, h, v_ref, hv_ref))
        beta, gc = _head_scalars(bgc, h, heads)
        betas.append(beta)
        gcs.append(gc)
    us, ws, qks, egs = _gdn_chunks_local(qs, ks, vs, betas, gcs, mincl_ref[...], seq)

    pad = jnp.zeros((seq, dk), F32)
    pairs = [(hh, si) for hh in range(hpi) for si in range(nseq)]
    sl = lambda si: slice(si * seq, (si + 1) * seq)
    s0 = {p: s0_ref[p[1], p[0]] for p in pairs}
    qes = [qs[hh] * egs[hh] for hh in range(hpi)]
    wqs = {(hh, si): _dot(jnp.concatenate([ws[hh][sl(si)], qes[hh][sl(si)]], axis=0), s0[(hh, si)])
           for hh, si in pairs}
    v_news = {(hh, si): us[hh][sl(si)] - wqs[(hh, si)][:seq] for hh, si in pairs}
    for hh, si in pairs:
        g_last = gcs[hh][(si + 1) * seq - 1:(si + 1) * seq, :]
        kd = ks[hh][sl(si)] * jnp.exp(g_last - gcs[hh][sl(si)])
        sout_ref[si, hh] = s0[(hh, si)] * jnp.exp(g_last) + _dot_tn(
            jnp.concatenate([kd, pad], axis=0), jnp.concatenate([v_news[(hh, si)], pad], axis=0))
    gn = gn_ref[...]
    for hh in range(hpi):
        o = (jnp.concatenate([wqs[(hh, si)][seq:] for si in range(nseq)], axis=0)
             + _dot(qks[hh], jnp.concatenate([v_news[(hh, si)] for si in range(nseq)], axis=0)))
        o_ref[:, hh * dk:(hh + 1) * dk] = _gdn_finish(o, z_ref[hh], ga_ref[hh], gn)


def _gdn_sample(proj32, proj16, bgc, hist, s0, cw, gn, mincl, s_prev, *, layer, row0, batch, seq, heads, dk, col,
                heads_per_step):
    rows = GDN_CHUNK
    nseq = rows // seq
    r0 = row0 // rows
    depth = s0.shape[0]
    hpi = heads_per_step
    hgroups = heads // hpi
    blk = lambda name: pl.BlockSpec((hpi, rows, dk), lambda b, h, o=col[name]: (o * hgroups + h, r0 + b, 0))
    hblk = lambda name: pl.BlockSpec((None, rows, hpi * dk), lambda b, h, o=col[name]: (layer, b, o * hgroups + h))
    state_blk = pl.BlockSpec((None, nseq, hpi, dk, dk), lambda b, h: (layer, b, h, 0, 0))
    extra_in = [] if s_prev is None else [s_prev]
    return pl.pallas_call(
        functools.partial(_gdn_sample_kernel, heads=heads, seq=seq),
        grid=(batch * seq // rows, hgroups),
        in_specs=[
            blk("q"), blk("k"), blk("v"), blk("z"), blk("ga"),
            pl.BlockSpec((rows, LANES), lambda b, h: (r0 + b, 0)),
            hblk("q"), hblk("k"), hblk("v"),
            pl.BlockSpec((None,) + cw.shape[1:], lambda b, h: (layer, 0, 0, 0)),
            pl.BlockSpec((None, 1, dk), lambda b, h: (layer, 0, 0)),
            pl.BlockSpec((GDN_CHUNK, GDN_CHUNK), lambda b, h: (0, 0)),
            state_blk,
        ] + [pl.BlockSpec(memory_space=pl.ANY)] * len(extra_in),
        out_specs=[
            pl.BlockSpec((rows, hpi * dk), lambda b, h: (b, h)),
            state_blk,
            pl.BlockSpec((3, rows, hpi * dk), lambda b, h: (0, b, h)),
        ],
        out_shape=[
            jax.ShapeDtypeStruct((batch * seq, heads * dk), BF16),
            jax.ShapeDtypeStruct((depth, batch, heads, dk, dk), F32),
            jax.ShapeDtypeStruct((3, batch * seq, heads * dk), F32),
        ],
        input_output_aliases={13: 1} if extra_in else {},
        compiler_params=pltpu.CompilerParams(
            dimension_semantics=("parallel", "parallel"), vmem_limit_bytes=VMEM_LIMIT),
        name="gdn_sample",
    )(proj32, proj32, proj32, proj16, proj16, bgc, hist, hist, hist, cw, gn, mincl, s0, *extra_in)


def _chunk_rms(x, bd, w):
    ms = _dot(x * x, bd)
    return x * lax.rsqrt(ms + EPS) * w


def _wide(ref, first, n):
    return jnp.concatenate([ref[first + i] for i in range(n)], axis=-1)


def _natural_blocks(slabs, hd):
    groups = len(slabs)
    per_block = LANES // hd
    lane = lax.broadcasted_iota(jnp.int32, (slabs[0].shape[0], LANES), 1)
    blocks = []
    for b in range(slabs[0].shape[1] * groups // LANES):
        blk = None
        for i in range(per_block):
            j, g = divmod(b * per_block + i, groups)
            first = (hd * j) // LANES * LANES
            piece = slabs[g][:, first:first + LANES]
            shift = (i * hd - hd * j) % LANES
            if shift:
                piece = pltpu.roll(piece, shift, 1)
            blk = piece if blk is None else jnp.where(lane < i * hd, blk, piece)
        blocks.append(blk)
    return blocks


def _swa_prompt_kernel(sink_ref, q_ref, kp_ref, kc_ref, vp_ref, vc_ref, gb_ref, qw_ref, kw_ref, bd_ref, bias_ref,
                       o_ref, kout_ref, vout_ref, *, kvh, hd, layer):
    blk = q_ref.shape[1]
    slab = kvh * hd
    spb = slab // LANES
    groups = q_ref.shape[0] // spb
    bd = bd_ref[...]
    kn = _chunk_rms(_wide(kc_ref, 0, spb).astype(F32), bd, kw_ref[...])
    kout_ref[...] = kn
    v_cur = _wide(vc_ref, 0, spb)
    vout_ref[...] = v_cur.astype(F32)
    kband = jnp.concatenate([_chunk_rms(_wide(kp_ref, 0, spb).astype(F32), bd, kw_ref[...]), kn],
                            axis=0).astype(BF16)
    vband = jnp.concatenate([_wide(vp_ref, 0, spb), v_cur], axis=0)
    lane_head = lax.broadcasted_iota(jnp.int32, (blk, slab), 1) // hd

    def scores(g):
        qn = _chunk_rms(_wide(q_ref, g * spb, spb).astype(F32), bd, qw_ref[...]) * (hd ** -0.5)
        qs = jnp.concatenate([jnp.where(lane_head == j, qn, 0.0) for j in range(kvh)], axis=0).astype(BF16)
        return lax.dot_general(qs, kband, (((1,), (1,)), ((), ())), preferred_element_type=F32)

    s_next = scores(0)
    gated = []
    for g in range(groups):
        s_all = s_next
        if g + 1 < groups:
            s_next = scores(g + 1)
        probs = []
        for j in range(kvh):
            p = g * kvh + j
            s = s_all[j * blk:(j + 1) * blk] + bias_ref[p]
            sink = sink_ref[layer, p]
            m = jnp.maximum(jnp.max(s, axis=-1, keepdims=True), sink)
            e = jnp.exp(s - m)
            den = jnp.sum(e, axis=-1, keepdims=True) + jnp.exp(sink - m)
            probs.append((e * (1.0 / den)).astype(BF16))
        pv = jnp.dot(jnp.concatenate(probs, axis=0), vband, preferred_element_type=F32)
        acc = pv[(kvh - 1) * blk:]
        for j in range(kvh - 2, -1, -1):
            acc = jnp.where(lane_head == j, pv[j * blk:(j + 1) * blk], acc)
        gated.append(acc * _sigmoid(_wide(gb_ref, g * spb, spb).astype(F32)))
    for b, blk_out in enumerate(_natural_blocks(gated, hd)):
        o_ref[:, b * LANES:(b + 1) * LANES] = blk_out.astype(BF16)


def _swa_prompt(proj, sinks, qw, kw, bd, bias, *, layer, batch, seq, window, kvh, hd, nq, col):
    nb = seq // window
    slab = kvh * hd
    nheads = nq // hd
    qb, sb = nq // LANES, slab // LANES
    cur = lambda name, n: pl.BlockSpec((n, window, LANES), lambda b, c, o=col[name] // n: (o, b * nb + c, 0))
    prev = lambda name, n: pl.BlockSpec(
        (n, window, LANES), lambda b, c, o=col[name] // n: (o, b * nb + jnp.maximum(c - 1, 0), 0))
    const2 = lambda b, c: (0, 0)
    lay3 = lambda b, c: (layer, 0, 0)
    return pl.pallas_call(
        functools.partial(_swa_prompt_kernel, kvh=kvh, hd=hd, layer=layer),
        grid=(batch, nb),
        in_specs=[
            pl.BlockSpec(memory_space=pltpu.SMEM),
            cur("sq", qb), prev("sk", sb), cur("sk", sb), prev("sv", sb), cur("sv", sb), cur("gb", qb),
            pl.BlockSpec((None, 1, slab), lay3),
            pl.BlockSpec((None, 1, slab), lay3),
            pl.BlockSpec((slab, slab), const2),
            pl.BlockSpec((None, nheads, window, 2 * window), lambda b, c: (jnp.minimum(c, 1), 0, 0, 0)),
        ],
        out_specs=[
            pl.BlockSpec((window, nq), lambda b, c: (b * nb + c, 0)),
            pl.BlockSpec((None, window, slab), lambda b, c: (b, 0, 0)),
            pl.BlockSpec((None, window, slab), lambda b, c: (b, 0, 0)),
        ],
        out_shape=[
            jax.ShapeDtypeStruct((batch * seq, nq), BF16),
            jax.ShapeDtypeStruct((batch, window, slab), F32),
            jax.ShapeDtypeStruct((batch, window, slab), F32),
        ],
        compiler_params=pltpu.CompilerParams(
            dimension_semantics=("parallel", "arbitrary"), vmem_limit_bytes=VMEM_LIMIT),
        name="swa_prompt",
    )(sinks, proj, proj, proj, proj, proj, proj, qw, kw, bd, bias)


def _swa_sample_kernel(q_ref, k_ref, v_ref, gb_ref, ck_ref, cv_ref, qw_ref, kw_ref, bd_ref, bias_ref, sink_ref,
                       *rest, kvh, hd, seq):
    o_ref, kout_ref, vout_ref = rest[-3:]
    rows = q_ref.shape[1]
    nseq = rows // seq
    slab = kvh * hd
    spb = slab // LANES
    groups = q_ref.shape[0] // spb
    window = ck_ref.shape[1]
    nkeys = bias_ref.shape[1]
    bd = bd_ref[...]
    kn = _chunk_rms(_wide(k_ref, 0, spb).astype(F32), bd, kw_ref[...]).reshape(nseq, seq, slab)
    vn = _wide(v_ref, 0, spb).astype(F32).reshape(nseq, seq, slab)
    ck = ck_ref[...]
    cv = cv_ref[...]
    kout_ref[...] = jnp.concatenate([ck[:, seq:, :], kn], axis=1)
    vout_ref[...] = jnp.concatenate([cv[:, seq:, :], vn], axis=1)
    zpad = jnp.zeros((nseq, nkeys - window - seq, slab), F32)
    k_all = jnp.concatenate([ck, kn, zpad], axis=1).astype(BF16)
    v_all = jnp.concatenate([cv, vn, zpad], axis=1).astype(BF16)

    lane_head = lax.broadcasted_iota(jnp.int32, (rows, slab), 1) // hd
    pieces = []
    for g in range(groups):
        qn = _chunk_rms(_wide(q_ref, g * spb, spb).astype(F32), bd, qw_ref[...]) * (hd ** -0.5)
        for j in range(kvh):
            pieces.append(jnp.where(lane_head == j, qn, 0.0).reshape(nseq, seq, slab))
    q_all = jnp.concatenate(pieces, axis=1).astype(BF16)
    s = jnp.einsum("bqd,bkd->bqk", q_all, k_all, preferred_element_type=F32) + bias_ref[...][None]
    sink = sink_ref[...][None]
    m = jnp.maximum(jnp.max(s, axis=-1, keepdims=True), sink)
    e = jnp.exp(s - m)
    den = jnp.sum(e, axis=-1, keepdims=True) + jnp.exp(sink - m)
    pv = jnp.einsum("bqk,bkd->bqd", (e * (1.0 / den)).astype(BF16), v_all, preferred_element_type=F32)
    lane_head3 = lax.broadcasted_iota(jnp.int32, (nseq, seq, slab), 2) // hd
    gated = []
    for g in range(groups):
        acc = jnp.zeros((nseq, seq, slab), F32)
        for j in range(kvh):
            p = g * kvh + j
            acc = acc + jnp.where(lane_head3 == j, pv[:, p * seq:(p + 1) * seq, :], 0.0)
        gated.append(acc.reshape(rows, slab) * _sigmoid(_wide(gb_ref, g * spb, spb).astype(F32)))
    for b, blk_out in enumerate(_natural_blocks(gated, hd)):
        o_ref[:, b * LANES:(b + 1) * LANES] = blk_out.astype(BF16)


def _swa_sample(proj, cache_k, cache_v, qw, kw, bd, bias, sink_col, kv_prev, *, layer, row0, batch, seq, kvh, hd,
                nq, col, nseq):
    rows = nseq * seq
    slab = kvh * hd
    depth, _, window, _ = cache_k.shape
    r0 = row0 // rows
    cache_blk = pl.BlockSpec((None, nseq, window, slab), lambda b: (layer, b, 0, 0))
    extra_in = [] if kv_prev is None else list(kv_prev)
    qb, sb = nq // LANES, slab // LANES
    cur = lambda name, n: pl.BlockSpec((n, rows, LANES), lambda b, o=col[name] // n: (o, r0 + b, 0))
    const2 = lambda b: (0, 0)
    lay3 = lambda b: (layer, 0, 0)
    return pl.pallas_call(
        functools.partial(_swa_sample_kernel, kvh=kvh, hd=hd, seq=seq),
        grid=(batch // nseq,),
        in_specs=[
            cur("sq", qb), cur("sk", sb), cur("sv", sb), cur("gb", qb),
            cache_blk, cache_blk,
            pl.BlockSpec((None, 1, slab), lay3),
            pl.BlockSpec((None, 1, slab), lay3),
            pl.BlockSpec((slab, slab), const2),
            pl.BlockSpec(bias.shape, const2),
            pl.BlockSpec((None,) + sink_col.shape[1:], lay3),
        ] + [pl.BlockSpec(memory_space=pl.ANY)] * len(extra_in),
        out_specs=[
            pl.BlockSpec((rows, nq), lambda b: (b, 0)),
            cache_blk, cache_blk,
        ],
        out_shape=[
            jax.ShapeDtypeStruct((batch * seq, nq), BF16),
            jax.ShapeDtypeStruct((depth, batch, window, slab), F32),
            jax.ShapeDtypeStruct((depth, batch, window, slab), F32),
        ],
        input_output_aliases={11: 1, 12: 2} if extra_in else {},
        compiler_params=pltpu.CompilerParams(
            dimension_semantics=("parallel",), vmem_limit_bytes=VMEM_LIMIT),
        name="swa_sample",
    )(proj, proj, proj, proj, cache_k, cache_v, qw, kw, bd, bias, sink_col, *extra_in)


def _dense_kernel(hp_ref, hs_ref, oap_ref, oas_ref, obp_ref, obs_ref, pp_ref, ps_ref, wo_ref, nf_ref,
                  wu_ref, wd_ref, np_ref, wg_ref, wp_ref, outp_ref, outs_ref, *, npt, ff_chunk):
    is_prompt = pl.program_id(0) < npt
    oa = jnp.where(is_prompt, oap_ref[...], oas_ref[...])
    ob = jnp.where(is_prompt, obp_ref[...], obs_ref[...])
    h = jnp.where(is_prompt, hp_ref[...], hs_ref[...])
    h = h + jnp.dot(oa + ob, wo_ref[...], preferred_element_type=F32)
    xn = (_rms(h) * nf_ref[...]).astype(BF16)
    acc = jnp.zeros_like(h)
    for c in range(wu_ref.shape[1] // ff_chunk):
        sl = slice(c * ff_chunk, (c + 1) * ff_chunk)
        hid = jnp.maximum(jnp.dot(xn, wu_ref[:, sl], preferred_element_type=F32), 0.0)
        acc = acc + jnp.dot((hid * hid).astype(BF16), wd_ref[sl, :], preferred_element_type=F32)
    h = h + acc
    xn = (_rms(h) * np_ref[...]).astype(BF16)
    gate = _sigmoid(jnp.dot(xn, wg_ref[...], preferred_element_type=F32))
    p = jnp.where(is_prompt, pp_ref[...], ps_ref[...])
    pe = jnp.dot(p.astype(BF16), wp_ref[...], preferred_element_type=F32)
    out = h + gate * pe

    @pl.when(is_prompt)
    def _():
        outp_ref[...] = out

    @pl.when(jnp.logical_not(is_prompt))
    def _():
        outs_ref[...] = out


def _dense(h_p, h_s, oa_p, oa_s, ob_p, ob_s, p_p, p_s, wo, nf, wu, wd, npl, wg, wp, *, layer, tm):
    tp, d = h_p.shape
    ts = h_s.shape[0]
    npt = tp // tm
    rows_p = lambda m: (jnp.minimum(m, npt - 1), 0)
    rows_s = lambda m: (jnp.maximum(m - npt, 0), 0)
    resident = lambda a: pl.BlockSpec((None,) + a.shape[1:], lambda m: (layer, 0, 0), pipeline_mode=pl.Buffered(1))
    return pl.pallas_call(
        functools.partial(_dense_kernel, npt=npt, ff_chunk=1024),
        grid=((tp + ts) // tm,),
        in_specs=[
            pl.BlockSpec((tm, d), rows_p), pl.BlockSpec((tm, d), rows_s),
            pl.BlockSpec((tm, d), rows_p), pl.BlockSpec((tm, d), rows_s),
            pl.BlockSpec((tm, d), rows_p), pl.BlockSpec((tm, d), rows_s),
            pl.BlockSpec((None, tm, p_p.shape[2]), lambda m: (layer, jnp.minimum(m, npt - 1), 0)),
            pl.BlockSpec((None, tm, p_s.shape[2]), lambda m: (layer, jnp.maximum(m - npt, 0), 0)),
            resident(wo), resident(nf), resident(wu), resident(wd),
            resident(npl), resident(wg), resident(wp),
        ],
        out_specs=[pl.BlockSpec((tm, d), rows_p), pl.BlockSpec((tm, d), rows_s)],
        out_shape=[jax.ShapeDtypeStruct((tp, d), F32), jax.ShapeDtypeStruct((ts, d), F32)],
        compiler_params=pltpu.CompilerParams(
            dimension_semantics=("arbitrary",), vmem_limit_bytes=VMEM_LIMIT),
        name="dense",
    )(h_p, h_s, oa_p, oa_s, ob_p, ob_s, p_p, p_s, wo, nf, wu, wd, npl, wg, wp)


def _seq_mask(seq_len):
    i = np.arange(GDN_CHUNK)
    m = (i[:, None] // seq_len == i[None, :] // seq_len) & (i[None, :] <= i[:, None])
    return m.astype(np.float32)


def kernel(x_prompt, x_sample, cache_conv, state_gdn, cache_swa_k, cache_swa_v, p_prompt, p_sample, norm_mix, w_in, conv_w, a_log, dt_bias, gdn_norm, q_norm, k_norm, attn_sinks, w_out, norm_ffn, w_up, w_down, norm_ple, w_ple_gate, w_ple_proj):
    batch, seq, d = x_prompt.shape
    dbatch, dseq, _ = x_sample.shape
    depth = w_in.shape[0]
    heads, dk, dv = state_gdn.shape[2:]
    window, kvh, hd = cache_swa_k.shape[2:]
    nq = d
    nheads = nq // hd
    groups = nheads // kvh
    slab = kvh * hd
    key_dim = heads * dk
    val_dim = heads * dv
    conv_dim = 2 * key_dim + val_dim
    width = conv_w.shape[1]
    tp, ts = batch * seq, dbatch * dseq
    assert dk == LANES and dv == LANES and val_dim == d and key_dim == d and slab % LANES == 0
    assert seq % window == 0 and window == GDN_CHUNK and GDN_CHUNK % dseq == 0 and dseq >= width - 1
    assert dseq == SUBLANES and ts % GDN_CHUNK == 0

    o_z = conv_dim
    o_b = o_z + val_dim
    o_a = o_b + heads
    o_sq = o_a + heads
    o_sk = o_sq + nq
    o_sv = o_sk + slab
    o_g = o_sv + slab
    col_h = {"q": 0, "k": 1, "v": 2, "z": 0, "ga": 2}
    off = {"sq": val_dim, "gb": val_dim + nq + d, "sk": val_dim + nq + 2 * d, "sv": val_dim + nq + 2 * d + slab}
    col_b = {name: o // LANES for name, o in off.items()}
    cb = conv_dim // LANES

    h_p = x_prompt.reshape(tp, d)
    h_s = x_sample.reshape(ts, d)
    bf = lambda a: a.astype(BF16)
    wo_all, wu_all, wd_all = bf(w_out), bf(w_up), bf(w_down)
    wg_all, wp_all = bf(w_ple_gate), bf(w_ple_proj)
    nf_all, np_all = norm_ffn[:, None, :], norm_ple[:, None, :]
    p_p = p_prompt.reshape(depth, tp, -1)
    p_s = p_sample.reshape(depth, ts, -1)

    masks = jnp.asarray(np.stack([_seq_mask(GDN_CHUNK), _seq_mask(dseq)]), BF16)
    mincl_p = jnp.asarray(_seq_mask(GDN_CHUNK), F32)
    mincl_s = jnp.asarray(_seq_mask(dseq), F32)
    bd = jnp.asarray(np.kron(np.eye(kvh), np.full((hd, hd), 1.0 / hd)), BF16)

    head_of = np.array([(p % kvh) * groups + p // kvh for p in range(nheads)])
    slopes = jnp.exp2(-8.0 * (jnp.asarray(head_of, F32) + 1.0) / nheads)
    qi = np.arange(window)[:, None]
    kj = np.arange(2 * window)[None, :]
    dist = window + qi - kj
    ok = (dist >= 0) & (dist <= window)
    ok_first = ok & (kj >= window)
    dist_f = jnp.asarray(dist, F32)
    bias_p = jnp.stack([
        jnp.where(jnp.asarray(okv)[None], -slopes[:, None, None] * dist_f[None], -jnp.inf) for okv in (ok_first, ok)])
    nkeys = -(-(window + dseq) // 16) * 16
    ti = np.tile(np.arange(dseq), nheads)[:, None]
    sj = np.arange(nkeys)[None, :]
    dist_s = ti + window - sj
    ok_s = (dist_s >= 0) & (dist_s <= window) & (sj < window + dseq)
    slopes_rows = jnp.repeat(slopes, dseq)[:, None]
    bias_s = jnp.where(jnp.asarray(ok_s), -slopes_rows * jnp.asarray(dist_s, F32), -jnp.inf)

    hist_all = jnp.pad(cache_conv, ((0, 0), (0, 0), (dseq - cache_conv.shape[2], 0), (0, 0)))
    hist_all = hist_all.reshape(depth, ts, conv_dim)
    ck_all = cache_swa_k.reshape(depth, dbatch, window, slab)
    cv_all = cache_swa_v.reshape(depth, dbatch, window, slab)
    cw_all = jnp.transpose(conv_w.reshape(depth, width, cb, LANES), (0, 2, 1, 3))

    avec_all = (jnp.zeros((depth, 2, LANES), F32).at[:, 0, heads:2 * heads].set(a_log)
                .at[:, 1, heads:2 * heads].set(dt_bias))
    nm_all, gn_all = norm_mix[:, None, :], gdn_norm[:, None, :]
    sinks_all = attn_sinks[:, head_of]
    sink_col_all = jnp.repeat(sinks_all, dseq, axis=1)[:, :, None]
    qw_all = jnp.tile(q_norm, (1, kvh))[:, None, :]
    kw_all = jnp.tile(k_norm, (1, kvh))[:, None, :]
    wba_all = jnp.pad(w_in[:, :, o_b:o_sq], ((0, 0), (0, 0), (0, LANES - 2 * heads))).astype(BF16)

    def regroup(x):
        return jnp.swapaxes(x.reshape(x.shape[:-1] + (kvh, groups, hd)), -3, -2).reshape(x.shape)

    w_main_all = jnp.concatenate([
        w_in[:, :, :o_b], regroup(w_in[:, :, o_sq:o_sk]), w_in[:, :, o_g:o_g + d], regroup(w_in[:, :, o_g + d:]),
        w_in[:, :, o_sk:o_g]], axis=2).astype(BF16)

    outs = {n: [] for n in ("conv_p", "gdn_p", "k_p", "v_p", "conv_s")}
    gdn_s = None
    kv_s = None
    for i in range(depth):
        proj32, proj16, bgc = _inproj(h_p, h_s, nm_all, w_main_all, wba_all, avec_all, masks, layer=i, heads=heads,
                                      tm=1024, tn=1536, n_f32=conv_dim)

        oa_p, gdn_p, tails = _gdn_prompt(proj32, proj16, bgc, cw_all, gn_all, mincl_p, layer=i, batch=batch, seq=seq,
                                         heads=heads, dk=dk, col=col_h, rows=512, heads_per_iter=4)
        oa_s, gdn_s, raw_s = _gdn_sample(proj32, proj16, bgc, hist_all, state_gdn, cw_all, gn_all, mincl_s, gdn_s, layer=i,
                                  row0=tp, batch=dbatch, seq=dseq, heads=heads, dk=dk, col=col_h, heads_per_step=4)

        ob_p, k_p, v_p = _swa_prompt(proj16, sinks_all, qw_all, kw_all, bd, bias_p, layer=i, batch=batch, seq=seq,
                                     window=window, kvh=kvh, hd=hd, nq=nq, col=col_b)
        ob_s, *kv_s = _swa_sample(proj16, ck_all, cv_all, qw_all, kw_all, bd, bias_s, sink_col_all, kv_s,
                                  layer=i, row0=tp, batch=dbatch, seq=dseq, kvh=kvh, hd=hd, nq=nq, col=col_b, nseq=8)

        h_p, h_s = _dense(h_p, h_s, oa_p, oa_s, ob_p, ob_s, p_p, p_s, wo_all, nf_all, wu_all, wd_all, np_all,
                          wg_all, wp_all, layer=i, tm=512)

        outs["conv_p"].append(tails)
        outs["conv_s"].append(raw_s)
        outs["gdn_p"].append(gdn_p)
        outs["k_p"].append(k_p.reshape(batch, window, kvh, hd))
        outs["v_p"].append(v_p.reshape(batch, window, kvh, hd))

    st = lambda n: jnp.stack(outs[n])
    conv_p = st("conv_p")[:, :, SUBLANES - (width - 1):, :]
    conv_s = st("conv_s").reshape(depth, 3, dbatch, dseq, key_dim)[:, :, :, dseq - (width - 1):, :]
    conv_s = jnp.transpose(conv_s, (0, 2, 3, 1, 4)).reshape(depth, dbatch, width - 1, conv_dim)
    return (h_p.reshape(batch, seq, d), h_s.reshape(dbatch, dseq, d),
            conv_p, st("gdn_p"), st("k_p"), st("v_p"), conv_s, gdn_s,
            kv_s[0].reshape(depth, dbatch, window, kvh, hd), kv_s[1].reshape(depth, dbatch, window, kvh, hd))
```

```python
import functools

import numpy as np
import jax
import jax.numpy as jnp
from jax import lax
from jax.experimental import pallas as pl
from jax.experimental.pallas import tpu as pltpu

F32 = jnp.float32
BF16 = jnp.bfloat16
EPS = 1e-6
LANES = 128
SUBLANES = 8
MXU_N = 256
GDN_CHUNK = 128
VMEM_LIMIT = 56 * 1024 * 1024


def _sigmoid(x):
    return 0.5 * jnp.tanh(0.5 * x) + 0.5


def _silu(x):
    hx = 0.5 * x
    return hx + hx * jnp.tanh(hx)


def _dot(a, b):
    return jnp.dot(a.astype(BF16), b.astype(BF16), preferred_element_type=F32)


def _dot_nt(a, b):
    return lax.dot_general(a.astype(BF16), b.astype(BF16), (((1,), (1,)), ((), ())), preferred_element_type=F32)


def _dot_tn(a, b):
    return lax.dot_general(a.astype(BF16), b.astype(BF16), (((0,), (0,)), ((), ())), preferred_element_type=F32)


def _rms(x):
    return x * lax.rsqrt(jnp.mean(x * x, axis=-1, keepdims=True) + EPS)


def _l2(x):
    return x * lax.rsqrt(jnp.sum(x * x, axis=-1, keepdims=True) + EPS)


def _inproj_kernel(xp_ref, xs_ref, nw_ref, w_ref, wba_ref, avec_ref, mask_ref, proj32_ref, proj16_ref, bgc_ref,
                   xn_ref, *, heads, npt, n32):
    n = pl.program_id(1)

    @pl.when(n == 0)
    def _():
        x = jnp.where(pl.program_id(0) < npt, xp_ref[...], xs_ref[...])
        xn = (_rms(x) * nw_ref[...]).astype(BF16)
        xn_ref[...] = xn
        ba = _dot_nt(xn, wba_ref[...])
        beta = _sigmoid(ba)
        xs = ba + avec_ref[1:2, :]
        softplus = jnp.maximum(xs, 0.0) + jnp.log1p(jnp.exp(-jnp.abs(xs)))
        g = -jnp.exp(avec_ref[0:1, :]) * softplus
        g1 = g.astype(BF16)
        r1 = g - g1.astype(F32)
        g2 = r1.astype(BF16)
        g3 = (r1 - g2.astype(F32)).astype(BF16)
        m = mask_ref[...]
        rows = g.shape[0]
        lane = lax.broadcasted_iota(jnp.int32, (GDN_CHUNK, LANES), 1)
        for r in range(rows // GDN_CHUNK):
            sl = slice(r * GDN_CHUNK, (r + 1) * GDN_CHUNK)
            gc = (jnp.dot(m, g1[sl], preferred_element_type=F32)
                  + jnp.dot(m, g2[sl], preferred_element_type=F32)
                  + jnp.dot(m, g3[sl], preferred_element_type=F32))
            bgc_ref[sl, :] = jnp.where(lane < heads, beta[sl], gc)

    def project(out_ref):
        xn = xn_ref[...]
        for j in range(w_ref.shape[0] // MXU_N):
            r = _dot_nt(xn, w_ref[j * MXU_N:(j + 1) * MXU_N, :])
            for i in range(MXU_N // LANES):
                out_ref[j * (MXU_N // LANES) + i] = r[:, i * LANES:(i + 1) * LANES].astype(out_ref.dtype)

    @pl.when(n < n32)
    def _():
        project(proj32_ref)

    @pl.when(n >= n32)
    def _():
        project(proj16_ref)


def _inproj(h_p, h_s, nw, w, wba, avec, masks, *, layer, heads, tm, tn, n_f32):
    d = h_p.shape[1]
    t_all = h_p.shape[0] + h_s.shape[0]
    n_out = w.shape[1]
    npt = h_p.shape[0] // tm
    n32 = n_f32 // tn
    assert n32 * tn == n_f32
    return pl.pallas_call(
        functools.partial(_inproj_kernel, heads=heads, npt=npt, n32=n32),
        grid=(t_all // tm, n_out // tn),
        in_specs=[
            pl.BlockSpec((tm, d), lambda m, n: (jnp.minimum(m, npt - 1), 0)),
            pl.BlockSpec((tm, d), lambda m, n: (jnp.maximum(m - npt, 0), 0)),
            pl.BlockSpec((None, 1, d), lambda m, n: (layer, 0, 0)),
            pl.BlockSpec((None, tn, d), lambda m, n: (layer, n, 0)),
            pl.BlockSpec((None, LANES, d), lambda m, n: (layer, 0, 0)),
            pl.BlockSpec((None, 2, LANES), lambda m, n: (layer, 0, 0)),
            pl.BlockSpec((None, GDN_CHUNK, GDN_CHUNK), lambda m, n: (jnp.where(m >= npt, 1, 0), 0, 0)),
        ],
        out_specs=[
            pl.BlockSpec((tn // LANES, tm, LANES), lambda m, n: (jnp.minimum(n, n32 - 1), m, 0)),
            pl.BlockSpec((tn // LANES, tm, LANES), lambda m, n: (jnp.maximum(n - n32, 0), m, 0)),
            pl.BlockSpec((tm, LANES), lambda m, n: (m, 0)),
        ],
        out_shape=[
            jax.ShapeDtypeStruct((n_f32 // LANES, t_all, LANES), F32),
            jax.ShapeDtypeStruct(((n_out - n_f32) // LANES, t_all, LANES), BF16),
            jax.ShapeDtypeStruct((t_all, LANES), F32),
        ],
        scratch_shapes=[pltpu.VMEM((tm, d), BF16)],
        compiler_params=pltpu.CompilerParams(
            dimension_semantics=("parallel", "arbitrary"), vmem_limit_bytes=VMEM_LIMIT),
        name="inproj",
    )(h_p, h_s, nw, w, wba, avec, masks)


def _gdn_chunks_local(qs, ks, vs, betas, gcs, mincl, seq_len):
    n = len(qs)
    c = qs[0].shape[0]
    dv = vs[0].shape[1]
    row = lax.broadcasted_iota(jnp.int32, (c, c), 0)
    col = lax.broadcasted_iota(jnp.int32, (c, c), 1)
    decs, egs = [], []
    for gc in gcs:
        gcb = jnp.broadcast_to(gc, (c, c))
        decs.append(jnp.exp(jnp.where(mincl > 0, gcb - gcb.T, -jnp.inf)))
        egs.append(jnp.exp(gc))
    kbs = [ks[i] * betas[i] for i in range(n)]
    kks = [_dot_nt(jnp.concatenate([kbs[i], qs[i]], axis=0), ks[i]) for i in range(n)]
    a_s = [jnp.where(row == col, 0.0, kks[i][:c] * decs[i]) for i in range(n)]
    qks = [kks[i][c:] * decs[i] for i in range(n)]
    same = (row >> 1) == (col >> 1)
    eye = jnp.where(row == col, 1.0, 0.0)
    ts = [eye - jnp.where(same, a, 0.0) for a in a_s]
    na16 = [(-a).astype(BF16) for a in a_s]
    for lvl in range(1, seq_len.bit_length() - 1):
        wider = (row >> (lvl + 1)) == (col >> (lvl + 1))
        sel = wider & jnp.logical_not(same)
        t16 = [t.astype(BF16) for t in ts]
        mids = [jnp.dot(na16[i], t16[i], preferred_element_type=F32) for i in range(n)]
        ts = [jnp.where(sel, jnp.dot(t16[i], mids[i].astype(BF16), preferred_element_type=F32), ts[i])
              for i in range(n)]
        same = wider
    sols = [_dot(ts[i], jnp.concatenate([vs[i] * betas[i], kbs[i] * egs[i]], axis=1)) for i in range(n)]
    return [s[:, :dv] for s in sols], [s[:, dv:] for s in sols], qks, egs


def _conv_taps(u, ru_fn, cw):
    width = cw.shape[0]
    out = u * cw[width - 1:width, :]
    for j in range(1, width):
        out = out + ru_fn(j) * cw[width - 1 - j:width - j, :]
    return _silu(out)


def _gdn_finish(o, z, ga, gn):
    return (_rms(o) * gn * _silu(z.astype(F32)) * _sigmoid(ga.astype(F32))).astype(BF16)


def _head_scalars(bgc, h, heads):
    lane = lax.broadcasted_iota(jnp.int32, bgc.shape, 1)
    beta = jnp.sum(jnp.where(lane == h, bgc, 0.0), axis=-1, keepdims=True)
    gc = jnp.sum(jnp.where(lane == h + heads, bgc, 0.0), axis=-1, keepdims=True)
    return beta, gc


def _gdn_prompt_kernel(q_ref, k_ref, v_ref, z_ref, ga_ref, bgc_ref, cw_ref, gn_ref, mincl_ref,
                       o_ref, sfin_ref, tails_ref,
                       s_ref, tail_ref, u_s, wqe_s, qk_s, kd_s, el_s, *, heads_per_iter):
    c_id = pl.program_id(1)
    heads, rows, dk = q_ref.shape
    nchunk = rows // GDN_CHUNK
    cc = GDN_CHUNK

    @pl.when(c_id == 0)
    def _():
        s_ref[...] = jnp.zeros_like(s_ref)
        tail_ref[...] = jnp.zeros_like(tail_ref)

    row8 = lax.broadcasted_iota(jnp.int32, (SUBLANES, dk), 0)
    bgc = bgc_ref[...]
    mincl = mincl_ref[...]

    def conv(kind, h, u_ref):
        u = u_ref[h]
        tail = tail_ref[kind * heads + h]

        def delayed(j):
            head = jnp.where(row8 < j, pltpu.roll(tail, j, 0), pltpu.roll(u[:SUBLANES], j, 0))
            return jnp.concatenate([head, u_ref[h, pl.ds(SUBLANES - j, rows - SUBLANES), :]], axis=0)

        out = _conv_taps(u, delayed, cw_ref[kind * heads + h])
        tail_ref[kind * heads + h] = u[rows - SUBLANES:]
        return out

    def local_body(it, carry):
        hs, qs, ks, vs, betas, gcs = [], [], [], [], [], []
        for hh in range(heads_per_iter):
            h = it * heads_per_iter + hh
            q = _l2(conv(0, h, q_ref)) * (dk ** -0.5)
            k = _l2(conv(1, h, k_ref))
            v = conv(2, h, v_ref)
            beta, gc = _head_scalars(bgc, h, heads)
            for ci in range(nchunk):
                sl = slice(ci * cc, (ci + 1) * cc)
                hs.append((h, ci))
                qs.append(q[sl]); ks.append(k[sl]); vs.append(v[sl]); betas.append(beta[sl]); gcs.append(gc[sl])
        us, ws, qks, egs = _gdn_chunks_local(qs, ks, vs, betas, gcs, mincl, cc)
        for i, (h, ci) in enumerate(hs):
            g_last = gcs[i][cc - 1:cc, :]
            u_s[h, ci] = us[i]
            wqe_s[h, ci, :cc] = ws[i].astype(BF16)
            wqe_s[h, ci, cc:] = (qs[i] * egs[i]).astype(BF16)
            qk_s[h, ci] = qks[i].astype(BF16)
            kd_s[h, ci] = (ks[i] * jnp.exp(g_last - gcs[i])).astype(BF16)
            el_s[h, ci] = jnp.broadcast_to(jnp.exp(g_last), (SUBLANES, dk))
        return carry

    lax.fori_loop(0, heads // heads_per_iter, local_body, 0)

    states = [s_ref[h] for h in range(heads)]
    gn = gn_ref[...]
    for ci in range(nchunk):
        sl = slice(ci * cc, (ci + 1) * cc)
        wqs = [jnp.dot(wqe_s[h, ci], states[h].astype(BF16), preferred_element_type=F32) for h in range(heads)]
        v_news = [u_s[h, ci] - wqs[h][:cc] for h in range(heads)]
        vn16 = [v.astype(BF16) for v in v_news]
        outs = [wqs[h][cc:] + jnp.dot(qk_s[h, ci], vn16[h], preferred_element_type=F32) for h in range(heads)]
        states = [states[h] * el_s[h, ci][0:1, :]
                  + lax.dot_general(kd_s[h, ci], vn16[h], (((0,), (0,)), ((), ())), preferred_element_type=F32)
                  for h in range(heads)]
        for h in range(heads):
            o_ref[sl, h * dk:(h + 1) * dk] = _gdn_finish(outs[h], z_ref[h, sl, :], ga_ref[h, sl, :], gn)
    for h in range(heads):
        s_ref[h] = states[h]
        sfin_ref[h] = states[h]
    for idx in range(3 * heads):
        tails_ref[:, idx * dk:(idx + 1) * dk] = tail_ref[idx]


def _gdn_prompt(proj32, proj16, bgc, cw, gn, mincl, *, layer, batch, seq, heads, dk, col, rows, heads_per_iter):
    ncb = seq // rows
    nchunk = rows // GDN_CHUNK
    blk = lambda name: pl.BlockSpec((heads, rows, dk), lambda b, c, o=col[name]: (o, b * ncb + c, 0))
    return pl.pallas_call(
        functools.partial(_gdn_prompt_kernel, heads_per_iter=heads_per_iter),
        grid=(batch, ncb),
        in_specs=[
            blk("q"), blk("k"), blk("v"), blk("z"), blk("ga"),
            pl.BlockSpec((rows, LANES), lambda b, c: (b * ncb + c, 0)),
            pl.BlockSpec((None,) + cw.shape[1:], lambda b, c: (layer, 0, 0, 0)),
            pl.BlockSpec((None, 1, dk), lambda b, c: (layer, 0, 0)),
            pl.BlockSpec((GDN_CHUNK, GDN_CHUNK), lambda b, c: (0, 0)),
        ],
        out_specs=[
            pl.BlockSpec((rows, heads * dk), lambda b, c: (b * ncb + c, 0)),
            pl.BlockSpec((None, heads, dk, dk), lambda b, c: (b, 0, 0, 0)),
            pl.BlockSpec((None, SUBLANES, 3 * heads * dk), lambda b, c: (b, 0, 0)),
        ],
        out_shape=[
            jax.ShapeDtypeStruct((batch * seq, heads * dk), BF16),
            jax.ShapeDtypeStruct((batch, heads, dk, dk), F32),
            jax.ShapeDtypeStruct((batch, SUBLANES, 3 * heads * dk), F32),
        ],
        scratch_shapes=[
            pltpu.VMEM((heads, dk, dk), F32),
            pltpu.VMEM((3 * heads, SUBLANES, dk), F32),
            pltpu.VMEM((heads, nchunk, GDN_CHUNK, dk), F32),
            pltpu.VMEM((heads, nchunk, 2 * GDN_CHUNK, dk), BF16),
            pltpu.VMEM((heads, nchunk, GDN_CHUNK, GDN_CHUNK), BF16),
            pltpu.VMEM((heads, nchunk, GDN_CHUNK, dk), BF16),
            pltpu.VMEM((heads, nchunk, SUBLANES, dk), F32),
        ],
        compiler_params=pltpu.CompilerParams(
            dimension_semantics=("parallel", "arbitrary"), vmem_limit_bytes=VMEM_LIMIT),
        name="gdn_prompt",
    )(proj32, proj32, proj32, proj16, proj16, bgc, cw, gn, mincl)


def _gdn_sample_kernel(q_ref, k_ref, v_ref, z_ref, ga_ref, bgc_ref, hq_ref, hk_ref, hv_ref,
                       cw_ref, gn_ref, mincl_ref, s0_ref, *rest, heads, seq):
    o_ref, sout_ref, raw_ref = rest[-3:]
    hpi, rows, dk = q_ref.shape
    nseq = rows // seq
    t_idx = lax.broadcasted_iota(jnp.int32, (nseq, seq, dk), 1)
    bgc = bgc_ref[...]

    def conv(kind, hh, h, u_ref, hist_ref):
        u = u_ref[hh]
        raw_ref[kind, :, hh * dk:(hh + 1) * dk] = u
        u3 = u.reshape(nseq, seq, dk)
        hist3 = hist_ref[:, hh * dk:(hh + 1) * dk].reshape(nseq, seq, dk)

        def delayed(j):
            xj = jnp.where(t_idx < j, pltpu.roll(hist3, j, 1), pltpu.roll(u3, j, 1))
            return xj.reshape(rows, dk)

        return _conv_taps(u, delayed, cw_ref[kind * heads + h])

    qs, ks, vs, betas, gcs = [], [], [], [], []
    for hh in range(hpi):
        h = pl.program_id(1) * hpi + hh
        qs.append(_l2(conv(0, hh, h, q_ref, hq_ref)) * (dk ** -0.5))
        ks.append(_l2(conv(1, hh, h, k_ref, hk_ref)))
        vs.append(conv(2, hh, h, v_ref, hv_ref))
        beta, gc = _head_scalars(bgc, h, heads)
        betas.append(beta)
        gcs.append(gc)
    us, ws, qks, egs = _gdn_chunks_local(qs, ks, vs, betas, gcs, mincl_ref[...], seq)

    pad = jnp.zeros((seq, dk), F32)
    pairs = [(hh, si) for hh in range(hpi) for si in range(nseq)]
    sl = lambda si: slice(si * seq, (si + 1) * seq)
    s0 = {p: s0_ref[p[1], p[0]] for p in pairs}
    qes = [qs[hh] * egs[hh] for hh in range(hpi)]
    wqs = {(hh, si): _dot(jnp.concatenate([ws[hh][sl(si)], qes[hh][sl(si)]], axis=0), s0[(hh, si)])
           for hh, si in pairs}
    v_news = {(hh, si): us[hh][sl(si)] - wqs[(hh, si)][:seq] for hh, si in pairs}
    for hh, si in pairs:
        g_last = gcs[hh][(si + 1) * seq - 1:(si + 1) * seq, :]
        kd = ks[hh][sl(si)] * jnp.exp(g_last - gcs[hh][sl(si)])
        sout_ref[si, hh] = s0[(hh, si)] * jnp.exp(g_last) + _dot_tn(
            jnp.concatenate([kd, pad], axis=0), jnp.concatenate([v_news[(hh, si)], pad], axis=0))
    gn = gn_ref[...]
    for hh in range(hpi):
        o = (jnp.concatenate([wqs[(hh, si)][seq:] for si in range(nseq)], axis=0)
             + _dot(qks[hh], jnp.concatenate([v_news[(hh, si)] for si in range(nseq)], axis=0)))
        o_ref[:, hh * dk:(hh + 1) * dk] = _gdn_finish(o, z_ref[hh], ga_ref[hh], gn)


def _gdn_sample(proj32, proj16, bgc, hist, s0, cw, gn, mincl, s_prev, *, layer, row0, batch, seq, heads, dk, col,
                heads_per_step):
    rows = GDN_CHUNK
    nseq = rows // seq
    r0 = row0 // rows
    depth = s0.shape[0]
    hpi = heads_per_step
    hgroups = heads // hpi
    blk = lambda name: pl.BlockSpec((hpi, rows, dk), lambda b, h, o=col[name]: (o * hgroups + h, r0 + b, 0))
    hblk = lambda name: pl.BlockSpec((None, rows, hpi * dk), lambda b, h, o=col[name]: (layer, b, o * hgroups + h))
    state_blk = pl.BlockSpec((None, nseq, hpi, dk, dk), lambda b, h: (layer, b, h, 0, 0))
    extra_in = [] if s_prev is None else [s_prev]
    return pl.pallas_call(
        functools.partial(_gdn_sample_kernel, heads=heads, seq=seq),
        grid=(batch * seq // rows, hgroups),
        in_specs=[
            blk("q"), blk("k"), blk("v"), blk("z"), blk("ga"),
            pl.BlockSpec((rows, LANES), lambda b, h: (r0 + b, 0)),
            hblk("q"), hblk("k"), hblk("v"),
            pl.BlockSpec((None,) + cw.shape[1:], lambda b, h: (layer, 0, 0, 0)),
            pl.BlockSpec((None, 1, dk), lambda b, h: (layer, 0, 0)),
            pl.BlockSpec((GDN_CHUNK, GDN_CHUNK), lambda b, h: (0, 0)),
            state_blk,
        ] + [pl.BlockSpec(memory_space=pl.ANY)] * len(extra_in),
        out_specs=[
            pl.BlockSpec((rows, hpi * dk), lambda b, h: (b, h)),
            state_blk,
            pl.BlockSpec((3, rows, hpi * dk), lambda b, h: (0, b, h)),
        ],
        out_shape=[
            jax.ShapeDtypeStruct((batch * seq, heads * dk), BF16),
            jax.ShapeDtypeStruct((depth, batch, heads, dk, dk), F32),
            jax.ShapeDtypeStruct((3, batch * seq, heads * dk), F32),
        ],
        input_output_aliases={13: 1} if extra_in else {},
        compiler_params=pltpu.CompilerParams(
            dimension_semantics=("parallel", "parallel"), vmem_limit_bytes=VMEM_LIMIT),
        name="gdn_sample",
    )(proj32, proj32, proj32, proj16, proj16, bgc, hist, hist, hist, cw, gn, mincl, s0, *extra_in)


def _chunk_rms(x, bd, w):
    ms = _dot(x * x, bd)
    return x * lax.rsqrt(ms + EPS) * w


def _wide(ref, first, n):
    return jnp.concatenate([ref[first + i] for i in range(n)], axis=-1)


def _natural_blocks(slabs, hd):
    groups = len(slabs)
    per_block = LANES // hd
    lane = lax.broadcasted_iota(jnp.int32, (slabs[0].shape[0], LANES), 1)
    blocks = []
    for b in range(slabs[0].shape[1] * groups // LANES):
        blk = None
        for i in range(per_block):
            j, g = divmod(b * per_block + i, groups)
            first = (hd * j) // LANES * LANES
            piece = slabs[g][:, first:first + LANES]
            shift = (i * hd - hd * j) % LANES
            if shift:
                piece = pltpu.roll(piece, shift, 1)
            blk = piece if blk is None else jnp.where(lane < i * hd, blk, piece)
        blocks.append(blk)
    return blocks


def _swa_prompt_kernel(sink_ref, q_ref, kp_ref, kc_ref, vp_ref, vc_ref, gb_ref, qw_ref, kw_ref, bd_ref, bias_ref,
                       o_ref, kout_ref, vout_ref, *, kvh, hd, layer):
    blk = q_ref.shape[1]
    slab = kvh * hd
    spb = slab // LANES
    groups = q_ref.shape[0] // spb
    bd = bd_ref[...]
    kn = _chunk_rms(_wide(kc_ref, 0, spb).astype(F32), bd, kw_ref[...])
    kout_ref[...] = kn
    v_cur = _wide(vc_ref, 0, spb)
    vout_ref[...] = v_cur.astype(F32)
    kband = jnp.concatenate([_chunk_rms(_wide(kp_ref, 0, spb).astype(F32), bd, kw_ref[...]), kn],
                            axis=0).astype(BF16)
    vband = jnp.concatenate([_wide(vp_ref, 0, spb), v_cur], axis=0)
    lane_head = lax.broadcasted_iota(jnp.int32, (blk, slab), 1) // hd

    def scores(g):
        qn = _chunk_rms(_wide(q_ref, g * spb, spb).astype(F32), bd, qw_ref[...]) * (hd ** -0.5)
        qs = jnp.concatenate([jnp.where(lane_head == j, qn, 0.0) for j in range(kvh)], axis=0).astype(BF16)
        return lax.dot_general(qs, kband, (((1,), (1,)), ((), ())), preferred_element_type=F32)

    s_groups = [scores(g) for g in range(groups)]
    gated = []
    for g in range(groups):
        s_all = s_groups[g]
        probs = []
        for j in range(kvh):
            p = g * kvh + j
            s = s_all[j * blk:(j + 1) * blk] + bias_ref[p]
            sink = sink_ref[layer, p]
            m = jnp.maximum(jnp.max(s, axis=-1, keepdims=True), sink)
            e = jnp.exp(s - m)
            den = jnp.sum(e, axis=-1, keepdims=True) + jnp.exp(sink - m)
            probs.append((e * (1.0 / den)).astype(BF16))
        pv = jnp.dot(jnp.concatenate(probs, axis=0), vband, preferred_element_type=F32)
        acc = pv[(kvh - 1) * blk:]
        for j in range(kvh - 2, -1, -1):
            acc = jnp.where(lane_head == j, pv[j * blk:(j + 1) * blk], acc)
        gated.append(acc * _sigmoid(_wide(gb_ref, g * spb, spb).astype(F32)))
    for b, blk_out in enumerate(_natural_blocks(gated, hd)):
        o_ref[:, b * LANES:(b + 1) * LANES] = blk_out.astype(BF16)


def _swa_prompt(proj, sinks, qw, kw, bd, bias, *, layer, batch, seq, window, kvh, hd, nq, col):
    nb = seq // window
    slab = kvh * hd
    nheads = nq // hd
    qb, sb = nq // LANES, slab // LANES
    cur = lambda name, n: pl.BlockSpec((n, window, LANES), lambda b, c, o=col[name] // n: (o, b * nb + c, 0))
    prev = lambda name, n: pl.BlockSpec(
        (n, window, LANES), lambda b, c, o=col[name] // n: (o, b * nb + jnp.maximum(c - 1, 0), 0))
    const2 = lambda b, c: (0, 0)
    lay3 = lambda b, c: (layer, 0, 0)
    return pl.pallas_call(
        functools.partial(_swa_prompt_kernel, kvh=kvh, hd=hd, layer=layer),
        grid=(batch, nb),
        in_specs=[
            pl.BlockSpec(memory_space=pltpu.SMEM),
            cur("sq", qb), prev("sk", sb), cur("sk", sb), prev("sv", sb), cur("sv", sb), cur("gb", qb),
            pl.BlockSpec((None, 1, slab), lay3),
            pl.BlockSpec((None, 1, slab), lay3),
            pl.BlockSpec((slab, slab), const2),
            pl.BlockSpec((None, nheads, window, 2 * window), lambda b, c: (jnp.minimum(c, 1), 0, 0, 0)),
        ],
        out_specs=[
            pl.BlockSpec((window, nq), lambda b, c: (b * nb + c, 0)),
            pl.BlockSpec((None, window, slab), lambda b, c: (b, 0, 0)),
            pl.BlockSpec((None, window, slab), lambda b, c: (b, 0, 0)),
        ],
        out_shape=[
            jax.ShapeDtypeStruct((batch * seq, nq), BF16),
            jax.ShapeDtypeStruct((batch, window, slab), F32),
            jax.ShapeDtypeStruct((batch, window, slab), F32),
        ],
        compiler_params=pltpu.CompilerParams(
            dimension_semantics=("parallel", "arbitrary"), vmem_limit_bytes=VMEM_LIMIT),
        name="swa_prompt",
    )(sinks, proj, proj, proj, proj, proj, proj, qw, kw, bd, bias)


def _swa_sample_kernel(q_ref, k_ref, v_ref, gb_ref, ck_ref, cv_ref, qw_ref, kw_ref, bd_ref, bias_ref, sink_ref,
                       *rest, kvh, hd, seq):
    o_ref, kout_ref, vout_ref = rest[-3:]
    rows = q_ref.shape[1]
    nseq = rows // seq
    slab = kvh * hd
    spb = slab // LANES
    groups = q_ref.shape[0] // spb
    window = ck_ref.shape[1]
    nkeys = bias_ref.shape[1]
    bd = bd_ref[...]
    kn = _chunk_rms(_wide(k_ref, 0, spb).astype(F32), bd, kw_ref[...]).reshape(nseq, seq, slab)
    vn = _wide(v_ref, 0, spb).astype(F32).reshape(nseq, seq, slab)
    ck = ck_ref[...]
    cv = cv_ref[...]
    kout_ref[...] = jnp.concatenate([ck[:, seq:, :], kn], axis=1)
    vout_ref[...] = jnp.concatenate([cv[:, seq:, :], vn], axis=1)
    zpad = jnp.zeros((nseq, nkeys - window - seq, slab), F32)
    k_all = jnp.concatenate([ck, kn, zpad], axis=1).astype(BF16)
    v_all = jnp.concatenate([cv, vn, zpad], axis=1).astype(BF16)

    lane_head = lax.broadcasted_iota(jnp.int32, (rows, slab), 1) // hd
    pieces = []
    for g in range(groups):
        qn = _chunk_rms(_wide(q_ref, g * spb, spb).astype(F32), bd, qw_ref[...]) * (hd ** -0.5)
        for j in range(kvh):
            pieces.append(jnp.where(lane_head == j, qn, 0.0).reshape(nseq, seq, slab))
    q_all = jnp.concatenate(pieces, axis=1).astype(BF16)
    s = jnp.einsum("bqd,bkd->bqk", q_all, k_all, preferred_element_type=F32) + bias_ref[...][None]
    sink = sink_ref[...][None]
    m = jnp.maximum(jnp.max(s, axis=-1, keepdims=True), sink)
    e = jnp.exp(s - m)
    den = jnp.sum(e, axis=-1, keepdims=True) + jnp.exp(sink - m)
    pv = jnp.einsum("bqk,bkd->bqd", (e * (1.0 / den)).astype(BF16), v_all, preferred_element_type=F32)
    lane_head3 = lax.broadcasted_iota(jnp.int32, (nseq, seq, slab), 2) // hd
    gated = []
    for g in range(groups):
        acc = jnp.zeros((nseq, seq, slab), F32)
        for j in range(kvh):
            p = g * kvh + j
            acc = acc + jnp.where(lane_head3 == j, pv[:, p * seq:(p + 1) * seq, :], 0.0)
        gated.append(acc.reshape(rows, slab) * _sigmoid(_wide(gb_ref, g * spb, spb).astype(F32)))
    for b, blk_out in enumerate(_natural_blocks(gated, hd)):
        o_ref[:, b * LANES:(b + 1) * LANES] = blk_out.astype(BF16)


def _swa_sample(proj, cache_k, cache_v, qw, kw, bd, bias, sink_col, kv_prev, *, layer, row0, batch, seq, kvh, hd,
                nq, col, nseq):
    rows = nseq * seq
    slab = kvh * hd
    depth, _, window, _ = cache_k.shape
    r0 = row0 // rows
    cache_blk = pl.BlockSpec((None, nseq, window, slab), lambda b: (layer, b, 0, 0))
    extra_in = [] if kv_prev is None else list(kv_prev)
    qb, sb = nq // LANES, slab // LANES
    cur = lambda name, n: pl.BlockSpec((n, rows, LANES), lambda b, o=col[name] // n: (o, r0 + b, 0))
    const2 = lambda b: (0, 0)
    lay3 = lambda b: (layer, 0, 0)
    return pl.pallas_call(
        functools.partial(_swa_sample_kernel, kvh=kvh, hd=hd, seq=seq),
        grid=(batch // nseq,),
        in_specs=[
            cur("sq", qb), cur("sk", sb), cur("sv", sb), cur("gb", qb),
            cache_blk, cache_blk,
            pl.BlockSpec((None, 1, slab), lay3),
            pl.BlockSpec((None, 1, slab), lay3),
            pl.BlockSpec((slab, slab), const2),
            pl.BlockSpec(bias.shape, const2),
            pl.BlockSpec((None,) + sink_col.shape[1:], lay3),
        ] + [pl.BlockSpec(memory_space=pl.ANY)] * len(extra_in),
        out_specs=[
            pl.BlockSpec((rows, nq), lambda b: (b, 0)),
            cache_blk, cache_blk,
        ],
        out_shape=[
            jax.ShapeDtypeStruct((batch * seq, nq), BF16),
            jax.ShapeDtypeStruct((depth, batch, window, slab), F32),
            jax.ShapeDtypeStruct((depth, batch, window, slab), F32),
        ],
        input_output_aliases={11: 1, 12: 2} if extra_in else {},
        compiler_params=pltpu.CompilerParams(
            dimension_semantics=("parallel",), vmem_limit_bytes=VMEM_LIMIT),
        name="swa_sample",
    )(proj, proj, proj, proj, cache_k, cache_v, qw, kw, bd, bias, sink_col, *extra_in)


def _dense_kernel(hp_ref, hs_ref, oap_ref, oas_ref, obp_ref, obs_ref, pp_ref, ps_ref, wo_ref, nf_ref,
                  wu_ref, wd_ref, np_ref, wg_ref, wp_ref, outp_ref, outs_ref, *, npt, ff_chunk):
    is_prompt = pl.program_id(0) < npt
    oa = jnp.where(is_prompt, oap_ref[...], oas_ref[...])
    ob = jnp.where(is_prompt, obp_ref[...], obs_ref[...])
    h = jnp.where(is_prompt, hp_ref[...], hs_ref[...])
    h = h + jnp.dot(oa + ob, wo_ref[...], preferred_element_type=F32)
    xn = (_rms(h) * nf_ref[...]).astype(BF16)
    acc = jnp.zeros_like(h)
    for c in range(wu_ref.shape[1] // ff_chunk):
        sl = slice(c * ff_chunk, (c + 1) * ff_chunk)
        hid = jnp.maximum(jnp.dot(xn, wu_ref[:, sl], preferred_element_type=F32), 0.0)
        acc = acc + jnp.dot((hid * hid).astype(BF16), wd_ref[sl, :], preferred_element_type=F32)
    h = h + acc
    xn = (_rms(h) * np_ref[...]).astype(BF16)
    gate = _sigmoid(jnp.dot(xn, wg_ref[...], preferred_element_type=F32))
    p = jnp.where(is_prompt, pp_ref[...], ps_ref[...])
    pe = jnp.dot(p.astype(BF16), wp_ref[...], preferred_element_type=F32)
    out = h + gate * pe

    @pl.when(is_prompt)
    def _():
        outp_ref[...] = out

    @pl.when(jnp.logical_not(is_prompt))
    def _():
        outs_ref[...] = out


def _dense(h_p, h_s, oa_p, oa_s, ob_p, ob_s, p_p, p_s, wo, nf, wu, wd, npl, wg, wp, *, layer, tm):
    tp, d = h_p.shape
    ts = h_s.shape[0]
    npt = tp // tm
    rows_p = lambda m: (jnp.minimum(m, npt - 1), 0)
    rows_s = lambda m: (jnp.maximum(m - npt, 0), 0)
    resident = lambda a: pl.BlockSpec((None,) + a.shape[1:], lambda m: (layer, 0, 0), pipeline_mode=pl.Buffered(1))
    return pl.pallas_call(
        functools.partial(_dense_kernel, npt=npt, ff_chunk=1024),
        grid=((tp + ts) // tm,),
        in_specs=[
            pl.BlockSpec((tm, d), rows_p), pl.BlockSpec((tm, d), rows_s),
            pl.BlockSpec((tm, d), rows_p), pl.BlockSpec((tm, d), rows_s),
            pl.BlockSpec((tm, d), rows_p), pl.BlockSpec((tm, d), rows_s),
            pl.BlockSpec((None, tm, p_p.shape[2]), lambda m: (layer, jnp.minimum(m, npt - 1), 0)),
            pl.BlockSpec((None, tm, p_s.shape[2]), lambda m: (layer, jnp.maximum(m - npt, 0), 0)),
            resident(wo), resident(nf), resident(wu), resident(wd),
            resident(npl), resident(wg), resident(wp),
        ],
        out_specs=[pl.BlockSpec((tm, d), rows_p), pl.BlockSpec((tm, d), rows_s)],
        out_shape=[jax.ShapeDtypeStruct((tp, d), F32), jax.ShapeDtypeStruct((ts, d), F32)],
        compiler_params=pltpu.CompilerParams(
            dimension_semantics=("arbitrary",), vmem_limit_bytes=VMEM_LIMIT),
        name="dense",
    )(h_p, h_s, oa_p, oa_s, ob_p, ob_s, p_p, p_s, wo, nf, wu, wd, npl, wg, wp)


def _seq_mask(seq_len):
    i = np.arange(GDN_CHUNK)
    m = (i[:, None] // seq_len == i[None, :] // seq_len) & (i[None, :] <= i[:, None])
    return m.astype(np.float32)


def kernel(x_prompt, x_sample, cache_conv, state_gdn, cache_swa_k, cache_swa_v, p_prompt, p_sample, norm_mix, w_in, conv_w, a_log, dt_bias, gdn_norm, q_norm, k_norm, attn_sinks, w_out, norm_ffn, w_up, w_down, norm_ple, w_ple_gate, w_ple_proj):
    batch, seq, d = x_prompt.shape
    dbatch, dseq, _ = x_sample.shape
    depth = w_in.shape[0]
    heads, dk, dv = state_gdn.shape[2:]
    window, kvh, hd = cache_swa_k.shape[2:]
    nq = d
    nheads = nq // hd
    groups = nheads // kvh
    slab = kvh * hd
    key_dim = heads * dk
    val_dim = heads * dv
    conv_dim = 2 * key_dim + val_dim
    width = conv_w.shape[1]
    tp, ts = batch * seq, dbatch * dseq
    assert dk == LANES and dv == LANES and val_dim == d and key_dim == d and slab % LANES == 0
    assert seq % window == 0 and window == GDN_CHUNK and GDN_CHUNK % dseq == 0 and dseq >= width - 1
    assert dseq == SUBLANES and ts % GDN_CHUNK == 0

    o_z = conv_dim
    o_b = o_z + val_dim
    o_a = o_b + heads
    o_sq = o_a + heads
    o_sk = o_sq + nq
    o_sv = o_sk + slab
    o_g = o_sv + slab
    col_h = {"q": 0, "k": 1, "v": 2, "z": 0, "ga": 2}
    off = {"sq": val_dim, "gb": val_dim + nq + d, "sk": val_dim + nq + 2 * d, "sv": val_dim + nq + 2 * d + slab}
    col_b = {name: o // LANES for name, o in off.items()}
    cb = conv_dim // LANES

    h_p = x_prompt.reshape(tp, d)
    h_s = x_sample.reshape(ts, d)
    bf = lambda a: a.astype(BF16)
    wo_all, wu_all, wd_all = bf(w_out), bf(w_up), bf(w_down)
    wg_all, wp_all = bf(w_ple_gate), bf(w_ple_proj)
    nf_all, np_all = norm_ffn[:, None, :], norm_ple[:, None, :]
    p_p = p_prompt.reshape(depth, tp, -1)
    p_s = p_sample.reshape(depth, ts, -1)

    masks = jnp.asarray(np.stack([_seq_mask(GDN_CHUNK), _seq_mask(dseq)]), BF16)
    mincl_p = jnp.asarray(_seq_mask(GDN_CHUNK), F32)
    mincl_s = jnp.asarray(_seq_mask(dseq), F32)
    bd = jnp.asarray(np.kron(np.eye(kvh), np.full((hd, hd), 1.0 / hd)), BF16)

    head_of = np.array([(p % kvh) * groups + p // kvh for p in range(nheads)])
    slopes = jnp.exp2(-8.0 * (jnp.asarray(head_of, F32) + 1.0) / nheads)
    qi = np.arange(window)[:, None]
    kj = np.arange(2 * window)[None, :]
    dist = window + qi - kj
    ok = (dist >= 0) & (dist <= window)
    ok_first = ok & (kj >= window)
    dist_f = jnp.asarray(dist, F32)
    bias_p = jnp.stack([
        jnp.where(jnp.asarray(okv)[None], -slopes[:, None, None] * dist_f[None], -jnp.inf) for okv in (ok_first, ok)])
    nkeys = -(-(window + dseq) // 16) * 16
    ti = np.tile(np.arange(dseq), nheads)[:, None]
    sj = np.arange(nkeys)[None, :]
    dist_s = ti + window - sj
    ok_s = (dist_s >= 0) & (dist_s <= window) & (sj < window + dseq)
    slopes_rows = jnp.repeat(slopes, dseq)[:, None]
    bias_s = jnp.where(jnp.asarray(ok_s), -slopes_rows * jnp.asarray(dist_s, F32), -jnp.inf)

    hist_all = jnp.pad(cache_conv, ((0, 0), (0, 0), (dseq - cache_conv.shape[2], 0), (0, 0)))
    hist_all = hist_all.reshape(depth, ts, conv_dim)
    ck_all = cache_swa_k.reshape(depth, dbatch, window, slab)
    cv_all = cache_swa_v.reshape(depth, dbatch, window, slab)
    cw_all = jnp.transpose(conv_w.reshape(depth, width, cb, LANES), (0, 2, 1, 3))

    avec_all = (jnp.zeros((depth, 2, LANES), F32).at[:, 0, heads:2 * heads].set(a_log)
                .at[:, 1, heads:2 * heads].set(dt_bias))
    nm_all, gn_all = norm_mix[:, None, :], gdn_norm[:, None, :]
    sinks_all = attn_sinks[:, head_of]
    sink_col_all = jnp.repeat(sinks_all, dseq, axis=1)[:, :, None]
    qw_all = jnp.tile(q_norm, (1, kvh))[:, None, :]
    kw_all = jnp.tile(k_norm, (1, kvh))[:, None, :]
    w_t = jnp.swapaxes(w_in, 1, 2)
    wba_all = jnp.pad(w_t[:, o_b:o_sq], ((0, 0), (0, LANES - 2 * heads), (0, 0))).astype(BF16)

    def regroup(x):
        return jnp.swapaxes(x.reshape(depth, kvh, groups, hd, d), 1, 2).reshape(x.shape)

    w_main_all = jnp.concatenate([
        w_t[:, :o_b], regroup(w_t[:, o_sq:o_sk]), w_t[:, o_g:o_g + d], regroup(w_t[:, o_g + d:]),
        w_t[:, o_sk:o_g]], axis=1).astype(BF16)

    outs = {n: [] for n in ("conv_p", "gdn_p", "k_p", "v_p", "conv_s")}
    gdn_s = None
    kv_s = None
    for i in range(depth):
        proj32, proj16, bgc = _inproj(h_p, h_s, nm_all, w_main_all, wba_all, avec_all, masks, layer=i, heads=heads,
                                      tm=1024, tn=1536, n_f32=conv_dim)

        oa_p, gdn_p, tails = _gdn_prompt(proj32, proj16, bgc, cw_all, gn_all, mincl_p, layer=i, batch=batch, seq=seq,
                                         heads=heads, dk=dk, col=col_h, rows=512, heads_per_iter=4)
        oa_s, gdn_s, raw_s = _gdn_sample(proj32, proj16, bgc, hist_all, state_gdn, cw_all, gn_all, mincl_s, gdn_s, layer=i,
                                  row0=tp, batch=dbatch, seq=dseq, heads=heads, dk=dk, col=col_h, heads_per_step=4)

        ob_p, k_p, v_p = _swa_prompt(proj16, sinks_all, qw_all, kw_all, bd, bias_p, layer=i, batch=batch, seq=seq,
                                     window=window, kvh=kvh, hd=hd, nq=nq, col=col_b)
        ob_s, *kv_s = _swa_sample(proj16, ck_all, cv_all, qw_all, kw_all, bd, bias_s, sink_col_all, kv_s,
                                  layer=i, row0=tp, batch=dbatch, seq=dseq, kvh=kvh, hd=hd, nq=nq, col=col_b, nseq=8)

        h_p, h_s = _dense(h_p, h_s, oa_p, oa_s, ob_p, ob_s, p_p, p_s, wo_all, nf_all, wu_all, wd_all, np_all,
                          wg_all, wp_all, layer=i, tm=512)

        outs["conv_p"].append(tails)
        outs["conv_s"].append(raw_s)
        outs["gdn_p"].append(gdn_p)
        outs["k_p"].append(k_p.reshape(batch, window, kvh, hd))
        outs["v_p"].append(v_p.reshape(batch, window, kvh, hd))

    st = lambda n: jnp.stack(outs[n])
    conv_p = st("conv_p")[:, :, SUBLANES - (width - 1):, :]
    conv_s = st("conv_s").reshape(depth, 3, dbatch, dseq, key_dim)[:, :, :, dseq - (width - 1):, :]
    conv_s = jnp.transpose(conv_s, (0, 2, 3, 1, 4)).reshape(depth, dbatch, width - 1, conv_dim)
    return (h_p.reshape(batch, seq, d), h_s.reshape(dbatch, dseq, d),
            conv_p, st("gdn_p"), st("k_p"), st("v_p"), conv_s, gdn_s,
            kv_s[0].reshape(depth, dbatch, window, kvh, hd), kv_s[1].reshape(depth, dbatch, window, kvh, hd))
```

```python
import functools

import numpy as np
import jax
import jax.numpy as jnp
from jax import lax
from jax.experimental import pallas as pl
from jax.experimental.pallas import tpu as pltpu

F32 = jnp.float32
BF16 = jnp.bfloat16
EPS = 1e-6
LANES = 128
SUBLANES = 8
MXU_N = 256
GDN_CHUNK = 128
VMEM_LIMIT = 56 * 1024 * 1024


def _sigmoid(x):
    return 0.5 * jnp.tanh(0.5 * x) + 0.5


def _silu(x):
    hx = 0.5 * x
    return hx + hx * jnp.tanh(hx)


def _dot(a, b):
    return jnp.dot(a.astype(BF16), b.astype(BF16), preferred_element_type=F32)


def _dot_nt(a, b):
    return lax.dot_general(a.astype(BF16), b.astype(BF16), (((1,), (1,)), ((), ())), preferred_element_type=F32)


def _dot_tn(a, b):
    return lax.dot_general(a.astype(BF16), b.astype(BF16), (((0,), (0,)), ((), ())), preferred_element_type=F32)


def _rms(x):
    return x * lax.rsqrt(jnp.mean(x * x, axis=-1, keepdims=True) + EPS)


def _l2(x):
    return x * lax.rsqrt(jnp.sum(x * x, axis=-1, keepdims=True) + EPS)


def _inproj_kernel(xp_ref, xs_ref, nw_ref, w_ref, wba_ref, avec_ref, mask_ref, cw_ref, proj32_ref, proj16_ref, bgc_ref,
                   tails_ref, xn_ref, hist_ref, cbuf_ref, *, heads, npt, n32, tiles_per_seq):
    m = pl.program_id(0)
    n = pl.program_id(1)
    rows = xn_ref.shape[0]
    nblk = proj32_ref.shape[0]

    @pl.when((m == 0) & (n == 0))
    def _():
        hist_ref[...] = jnp.zeros_like(hist_ref)

    @pl.when(n == 0)
    def _():
        x = jnp.where(pl.program_id(0) < npt, xp_ref[...], xs_ref[...])
        xn = (_rms(x) * nw_ref[...]).astype(BF16)
        xn_ref[...] = xn
        ba = _dot_nt(xn, wba_ref[...])
        beta = _sigmoid(ba)
        xs = ba + avec_ref[1:2, :]
        softplus = jnp.maximum(xs, 0.0) + jnp.log1p(jnp.exp(-jnp.abs(xs)))
        g = -jnp.exp(avec_ref[0:1, :]) * softplus
        g1 = g.astype(BF16)
        r1 = g - g1.astype(F32)
        g2 = r1.astype(BF16)
        g3 = (r1 - g2.astype(F32)).astype(BF16)
        m = mask_ref[...]
        rows = g.shape[0]
        lane = lax.broadcasted_iota(jnp.int32, (GDN_CHUNK, LANES), 1)
        for r in range(rows // GDN_CHUNK):
            sl = slice(r * GDN_CHUNK, (r + 1) * GDN_CHUNK)
            gc = (jnp.dot(m, g1[sl], preferred_element_type=F32)
                  + jnp.dot(m, g2[sl], preferred_element_type=F32)
                  + jnp.dot(m, g3[sl], preferred_element_type=F32))
            bgc_ref[sl, :] = jnp.where(lane < heads, beta[sl], gc)

    def conv_silu(blk, raw):
        g = n * nblk + blk
        first = (m % tiles_per_seq) == 0
        cbuf_ref[blk, 0:SUBLANES, :] = jnp.where(first, 0.0, hist_ref[g])
        cbuf_ref[blk, SUBLANES:, :] = raw
        hist_ref[g] = raw[rows - SUBLANES:]
        cw = cw_ref[g]
        width = cw.shape[0]
        out = raw * cw[width - 1:width, :]
        for j in range(1, width):
            out = out + cbuf_ref[blk, pl.ds(SUBLANES - j, rows), :] * cw[width - 1 - j:width - j, :]
        return _silu(out)

    def project(out_ref, post):
        xn = xn_ref[...]
        for j in range(w_ref.shape[0] // MXU_N):
            r = _dot_nt(xn, w_ref[j * MXU_N:(j + 1) * MXU_N, :])
            for i in range(MXU_N // LANES):
                blk = j * (MXU_N // LANES) + i
                out_ref[blk] = post(blk, r[:, i * LANES:(i + 1) * LANES]).astype(out_ref.dtype)

    @pl.when((n < n32) & (m < npt))
    def _():
        project(proj32_ref, conv_silu)

    @pl.when((n < n32) & (m >= npt))
    def _():
        project(proj32_ref, lambda blk, raw: raw)

    @pl.when(n < n32)
    def _():
        for blk in range(nblk):
            tails_ref[blk] = hist_ref[n * nblk + blk]

    @pl.when(n >= n32)
    def _():
        project(proj16_ref, lambda blk, raw: raw)


def _inproj(h_p, h_s, nw, w, wba, avec, masks, cw, *, layer, heads, tm, tn, n_f32, seq):
    d = h_p.shape[1]
    t_all = h_p.shape[0] + h_s.shape[0]
    n_out = w.shape[1]
    npt = h_p.shape[0] // tm
    n32 = n_f32 // tn
    nblk = tn // LANES
    assert n32 * tn == n_f32 and seq % tm == 0
    return pl.pallas_call(
        functools.partial(_inproj_kernel, heads=heads, npt=npt, n32=n32, tiles_per_seq=seq // tm),
        grid=(t_all // tm, n_out // tn),
        in_specs=[
            pl.BlockSpec((tm, d), lambda m, n: (jnp.minimum(m, npt - 1), 0)),
            pl.BlockSpec((tm, d), lambda m, n: (jnp.maximum(m - npt, 0), 0)),
            pl.BlockSpec((None, 1, d), lambda m, n: (layer, 0, 0)),
            pl.BlockSpec((None, tn, d), lambda m, n: (layer, n, 0)),
            pl.BlockSpec((None, LANES, d), lambda m, n: (layer, 0, 0)),
            pl.BlockSpec((None, 2, LANES), lambda m, n: (layer, 0, 0)),
            pl.BlockSpec((None, GDN_CHUNK, GDN_CHUNK), lambda m, n: (jnp.where(m >= npt, 1, 0), 0, 0)),
            pl.BlockSpec((None,) + cw.shape[1:], lambda m, n: (layer, 0, 0, 0)),
        ],
        out_specs=[
            pl.BlockSpec((nblk, tm, LANES), lambda m, n: (jnp.minimum(n, n32 - 1), m, 0)),
            pl.BlockSpec((nblk, tm, LANES), lambda m, n: (jnp.maximum(n - n32, 0), m, 0)),
            pl.BlockSpec((tm, LANES), lambda m, n: (m, 0)),
            pl.BlockSpec((None, nblk, SUBLANES, LANES), lambda m, n: (m, jnp.minimum(n, n32 - 1), 0, 0)),
        ],
        out_shape=[
            jax.ShapeDtypeStruct((n_f32 // LANES, t_all, LANES), F32),
            jax.ShapeDtypeStruct(((n_out - n_f32) // LANES, t_all, LANES), BF16),
            jax.ShapeDtypeStruct((t_all, LANES), F32),
            jax.ShapeDtypeStruct((t_all // tm, n_f32 // LANES, SUBLANES, LANES), F32),
        ],
        scratch_shapes=[
            pltpu.VMEM((tm, d), BF16),
            pltpu.VMEM((n_f32 // LANES, SUBLANES, LANES), F32),
            pltpu.VMEM((nblk, tm + SUBLANES, LANES), F32),
        ],
        compiler_params=pltpu.CompilerParams(
            dimension_semantics=("arbitrary", "arbitrary"), vmem_limit_bytes=VMEM_LIMIT),
        name="inproj",
    )(h_p, h_s, nw, w, wba, avec, masks, cw)


def _gdn_chunks_local(qs, ks, vs, betas, gcs, mincl, seq_len):
    n = len(qs)
    c = qs[0].shape[0]
    dv = vs[0].shape[1]
    row = lax.broadcasted_iota(jnp.int32, (c, c), 0)
    col = lax.broadcasted_iota(jnp.int32, (c, c), 1)
    decs, egs = [], []
    for gc in gcs:
        gcb = jnp.broadcast_to(gc, (c, c))
        decs.append(jnp.exp(jnp.where(mincl > 0, gcb - gcb.T, -jnp.inf)))
        egs.append(jnp.exp(gc))
    kbs = [ks[i] * betas[i] for i in range(n)]
    kks = [_dot_nt(jnp.concatenate([kbs[i], qs[i]], axis=0), ks[i]) for i in range(n)]
    a_s = [jnp.where(row == col, 0.0, kks[i][:c] * decs[i]) for i in range(n)]
    qks = [kks[i][c:] * decs[i] for i in range(n)]
    same = (row >> 1) == (col >> 1)
    eye = jnp.where(row == col, 1.0, 0.0)
    ts = [eye - jnp.where(same, a, 0.0) for a in a_s]
    na16 = [(-a).astype(BF16) for a in a_s]
    for lvl in range(1, seq_len.bit_length() - 1):
        wider = (row >> (lvl + 1)) == (col >> (lvl + 1))
        sel = wider & jnp.logical_not(same)
        t16 = [t.astype(BF16) for t in ts]
        mids = [jnp.dot(na16[i], t16[i], preferred_element_type=F32) for i in range(n)]
        ts = [jnp.where(sel, jnp.dot(t16[i], mids[i].astype(BF16), preferred_element_type=F32), ts[i])
              for i in range(n)]
        same = wider
    sols = [_dot(ts[i], jnp.concatenate([vs[i] * betas[i], kbs[i] * egs[i]], axis=1)) for i in range(n)]
    return [s[:, :dv] for s in sols], [s[:, dv:] for s in sols], qks, egs


def _conv_taps(u, ru_fn, cw):
    width = cw.shape[0]
    out = u * cw[width - 1:width, :]
    for j in range(1, width):
        out = out + ru_fn(j) * cw[width - 1 - j:width - j, :]
    return _silu(out)


def _gdn_finish(o, z, ga, gn):
    return (_rms(o) * gn * _silu(z.astype(F32)) * _sigmoid(ga.astype(F32))).astype(BF16)


def _head_scalars(bgc, h, heads):
    lane = lax.broadcasted_iota(jnp.int32, bgc.shape, 1)
    beta = jnp.sum(jnp.where(lane == h, bgc, 0.0), axis=-1, keepdims=True)
    gc = jnp.sum(jnp.where(lane == h + heads, bgc, 0.0), axis=-1, keepdims=True)
    return beta, gc


def _gdn_prompt_kernel(q_ref, k_ref, v_ref, z_ref, ga_ref, bgc_ref, gn_ref, mincl_ref,
                       o_ref, sfin_ref,
                       s_ref, u_s, wqe_s, qk_s, kd_s, el_s, *, heads_per_iter):
    c_id = pl.program_id(1)
    heads, rows, dk = q_ref.shape
    nchunk = rows // GDN_CHUNK
    cc = GDN_CHUNK

    @pl.when(c_id == 0)
    def _():
        s_ref[...] = jnp.zeros_like(s_ref)

    bgc = bgc_ref[...]
    mincl = mincl_ref[...]

    def local_body(it, carry):
        hs, qs, ks, vs, betas, gcs = [], [], [], [], [], []
        for hh in range(heads_per_iter):
            h = it * heads_per_iter + hh
            q = _l2(q_ref[h]) * (dk ** -0.5)
            k = _l2(k_ref[h])
            v = v_ref[h]
            beta, gc = _head_scalars(bgc, h, heads)
            for ci in range(nchunk):
                sl = slice(ci * cc, (ci + 1) * cc)
                hs.append((h, ci))
                qs.append(q[sl]); ks.append(k[sl]); vs.append(v[sl]); betas.append(beta[sl]); gcs.append(gc[sl])
        us, ws, qks, egs = _gdn_chunks_local(qs, ks, vs, betas, gcs, mincl, cc)
        for i, (h, ci) in enumerate(hs):
            g_last = gcs[i][cc - 1:cc, :]
            u_s[h, ci] = us[i]
            wqe_s[h, ci, :cc] = ws[i].astype(BF16)
            wqe_s[h, ci, cc:] = (qs[i] * egs[i]).astype(BF16)
            qk_s[h, ci] = qks[i].astype(BF16)
            kd_s[h, ci] = (ks[i] * jnp.exp(g_last - gcs[i])).astype(BF16)
            el_s[h, ci] = jnp.broadcast_to(jnp.exp(g_last), (SUBLANES, dk))
        return carry

    lax.fori_loop(0, heads // heads_per_iter, local_body, 0)

    states = [s_ref[h] for h in range(heads)]
    gn = gn_ref[...]
    for ci in range(nchunk):
        sl = slice(ci * cc, (ci + 1) * cc)
        wqs = [jnp.dot(wqe_s[h, ci], states[h].astype(BF16), preferred_element_type=F32) for h in range(heads)]
        v_news = [u_s[h, ci] - wqs[h][:cc] for h in range(heads)]
        vn16 = [v.astype(BF16) for v in v_news]
        outs = [wqs[h][cc:] + jnp.dot(qk_s[h, ci], vn16[h], preferred_element_type=F32) for h in range(heads)]
        states = [states[h] * el_s[h, ci][0:1, :]
                  + lax.dot_general(kd_s[h, ci], vn16[h], (((0,), (0,)), ((), ())), preferred_element_type=F32)
                  for h in range(heads)]
        for h in range(heads):
            o_ref[sl, h * dk:(h + 1) * dk] = _gdn_finish(outs[h], z_ref[h, sl, :], ga_ref[h, sl, :], gn)
    for h in range(heads):
        s_ref[h] = states[h]
        sfin_ref[h] = states[h]


def _gdn_prompt(proj32, proj16, bgc, gn, mincl, *, layer, batch, seq, heads, dk, col, rows, heads_per_iter):
    ncb = seq // rows
    nchunk = rows // GDN_CHUNK
    blk = lambda name: pl.BlockSpec((heads, rows, dk), lambda b, c, o=col[name]: (o, b * ncb + c, 0))
    return pl.pallas_call(
        functools.partial(_gdn_prompt_kernel, heads_per_iter=heads_per_iter),
        grid=(batch, ncb),
        in_specs=[
            blk("q"), blk("k"), blk("v"), blk("z"), blk("ga"),
            pl.BlockSpec((rows, LANES), lambda b, c: (b * ncb + c, 0)),
            pl.BlockSpec((None, 1, dk), lambda b, c: (layer, 0, 0)),
            pl.BlockSpec((GDN_CHUNK, GDN_CHUNK), lambda b, c: (0, 0)),
        ],
        out_specs=[
            pl.BlockSpec((rows, heads * dk), lambda b, c: (b * ncb + c, 0)),
            pl.BlockSpec((None, heads, dk, dk), lambda b, c: (b, 0, 0, 0)),
        ],
        out_shape=[
            jax.ShapeDtypeStruct((batch * seq, heads * dk), BF16),
            jax.ShapeDtypeStruct((batch, heads, dk, dk), F32),
        ],
        scratch_shapes=[
            pltpu.VMEM((heads, dk, dk), F32),
            pltpu.VMEM((heads, nchunk, GDN_CHUNK, dk), F32),
            pltpu.VMEM((heads, nchunk, 2 * GDN_CHUNK, dk), BF16),
            pltpu.VMEM((heads, nchunk, GDN_CHUNK, GDN_CHUNK), BF16),
            pltpu.VMEM((heads, nchunk, GDN_CHUNK, dk), BF16),
            pltpu.VMEM((heads, nchunk, SUBLANES, dk), F32),
        ],
        compiler_params=pltpu.CompilerParams(
            dimension_semantics=("parallel", "arbitrary"), vmem_limit_bytes=VMEM_LIMIT),
        name="gdn_prompt",
    )(proj32, proj32, proj32, proj16, proj16, bgc, gn, mincl)


def _gdn_sample_kernel(q_ref, k_ref, v_ref, z_ref, ga_ref, bgc_ref, hq_ref, hk_ref, hv_ref,
                       cw_ref, gn_ref, mincl_ref, s0_ref, *rest, heads, seq):
    o_ref, sout_ref, raw_ref = rest[-3:]
    hpi, rows, dk = q_ref.shape
    nseq = rows // seq
    t_idx = lax.broadcasted_iota(jnp.int32, (nseq, seq, dk), 1)
    bgc = bgc_ref[...]

    def conv(kind, hh, h, u_ref, hist_ref):
        u = u_ref[hh]
        raw_ref[kind, :, hh * dk:(hh + 1) * dk] = u
        u3 = u.reshape(nseq, seq, dk)
        hist3 = hist_ref[:, hh * dk:(hh + 1) * dk].reshape(nseq, seq, dk)

        def delayed(j):
            xj = jnp.where(t_idx < j, pltpu.roll(hist3, j, 1), pltpu.roll(u3, j, 1))
            return xj.reshape(rows, dk)

        return _conv_taps(u, delayed, cw_ref[kind * heads + h])

    qs, ks, vs, betas, gcs = [], [], [], [], []
    for hh in range(hpi):
        h = pl.program_id(1) * hpi + hh
        qs.append(_l2(conv(0, hh, h, q_ref, hq_ref)) * (dk ** -0.5))
        ks.append(_l2(conv(1, hh, h, k_ref, hk_ref)))
        vs.append(conv(2, hh, h, v_ref, hv_ref))
        beta, gc = _head_scalars(bgc, h, heads)
        betas.append(beta)
        gcs.append(gc)
    us, ws, qks, egs = _gdn_chunks_local(qs, ks, vs, betas, gcs, mincl_ref[...], seq)

    pad = jnp.zeros((seq, dk), F32)
    pairs = [(hh, si) for hh in range(hpi) for si in range(nseq)]
    sl = lambda si: slice(si * seq, (si + 1) * seq)
    s0 = {p: s0_ref[p[1], p[0]] for p in pairs}
    qes = [qs[hh] * egs[hh] for hh in range(hpi)]
    wqs = {(hh, si): _dot(jnp.concatenate([ws[hh][sl(si)], qes[hh][sl(si)]], axis=0), s0[(hh, si)])
           for hh, si in pairs}
    v_news = {(hh, si): us[hh][sl(si)] - wqs[(hh, si)][:seq] for hh, si in pairs}
    for hh, si in pairs:
        g_last = gcs[hh][(si + 1) * seq - 1:(si + 1) * seq, :]
        kd = ks[hh][sl(si)] * jnp.exp(g_last - gcs[hh][sl(si)])
        sout_ref[si, hh] = s0[(hh, si)] * jnp.exp(g_last) + _dot_tn(
            jnp.concatenate([kd, pad], axis=0), jnp.concatenate([v_news[(hh, si)], pad], axis=0))
    gn = gn_ref[...]
    for hh in range(hpi):
        o = (jnp.concatenate([wqs[(hh, si)][seq:] for si in range(nseq)], axis=0)
             + _dot(qks[hh], jnp.concatenate([v_news[(hh, si)] for si in range(nseq)], axis=0)))
        o_ref[:, hh * dk:(hh + 1) * dk] = _gdn_finish(o, z_ref[hh], ga_ref[hh], gn)


def _gdn_sample(proj32, proj16, bgc, hist, s0, cw, gn, mincl, s_prev, *, layer, row0, batch, seq, heads, dk, col,
                heads_per_step):
    rows = GDN_CHUNK
    nseq = rows // seq
    r0 = row0 // rows
    depth = s0.shape[0]
    hpi = heads_per_step
    hgroups = heads // hpi
    blk = lambda name: pl.BlockSpec((hpi, rows, dk), lambda b, h, o=col[name]: (o * hgroups + h, r0 + b, 0))
    hblk = lambda name: pl.BlockSpec((None, rows, hpi * dk), lambda b, h, o=col[name]: (layer, b, o * hgroups + h))
    state_blk = pl.BlockSpec((None, nseq, hpi, dk, dk), lambda b, h: (layer, b, h, 0, 0))
    extra_in = [] if s_prev is None else [s_prev]
    return pl.pallas_call(
        functools.partial(_gdn_sample_kernel, heads=heads, seq=seq),
        grid=(batch * seq // rows, hgroups),
        in_specs=[
            blk("q"), blk("k"), blk("v"), blk("z"), blk("ga"),
            pl.BlockSpec((rows, LANES), lambda b, h: (r0 + b, 0)),
            hblk("q"), hblk("k"), hblk("v"),
            pl.BlockSpec((None,) + cw.shape[1:], lambda b, h: (layer, 0, 0, 0)),
            pl.BlockSpec((None, 1, dk), lambda b, h: (layer, 0, 0)),
            pl.BlockSpec((GDN_CHUNK, GDN_CHUNK), lambda b, h: (0, 0)),
            state_blk,
        ] + [pl.BlockSpec(memory_space=pl.ANY)] * len(extra_in),
        out_specs=[
            pl.BlockSpec((rows, hpi * dk), lambda b, h: (b, h)),
            state_blk,
            pl.BlockSpec((3, rows, hpi * dk), lambda b, h: (0, b, h)),
        ],
        out_shape=[
            jax.ShapeDtypeStruct((batch * seq, heads * dk), BF16),
            jax.ShapeDtypeStruct((depth, batch, heads, dk, dk), F32),
            jax.ShapeDtypeStruct((3, batch * seq, heads * dk), F32),
        ],
        input_output_aliases={13: 1} if extra_in else {},
        compiler_params=pltpu.CompilerParams(
            dimension_semantics=("parallel", "parallel"), vmem_limit_bytes=VMEM_LIMIT),
        name="gdn_sample",
    )(proj32, proj32, proj32, proj16, proj16, bgc, hist, hist, hist, cw, gn, mincl, s0, *extra_in)


def _chunk_rms(x, bd, w):
    ms = _dot(x * x, bd)
    return x * lax.rsqrt(ms + EPS) * w


def _wide(ref, first, n):
    return jnp.concatenate([ref[first + i] for i in range(n)], axis=-1)


def _natural_blocks(slabs, hd):
    groups = len(slabs)
    per_block = LANES // hd
    lane = lax.broadcasted_iota(jnp.int32, (slabs[0].shape[0], LANES), 1)
    blocks = []
    for b in range(slabs[0].shape[1] * groups // LANES):
        blk = None
        for i in range(per_block):
            j, g = divmod(b * per_block + i, groups)
            first = (hd * j) // LANES * LANES
            piece = slabs[g][:, first:first + LANES]
            shift = (i * hd - hd * j) % LANES
            if shift:
                piece = pltpu.roll(piece, shift, 1)
            blk = piece if blk is None else jnp.where(lane < i * hd, blk, piece)
        blocks.append(blk)
    return blocks


def _swa_prompt_kernel(sink_ref, q_ref, kp_ref, kc_ref, vp_ref, vc_ref, gb_ref, qw_ref, kw_ref, bd_ref, bias_ref,
                       o_ref, kout_ref, vout_ref, *, kvh, hd, layer):
    blk = q_ref.shape[1]
    slab = kvh * hd
    spb = slab // LANES
    groups = q_ref.shape[0] // spb
    bd = bd_ref[...]
    kn = _chunk_rms(_wide(kc_ref, 0, spb).astype(F32), bd, kw_ref[...])
    kout_ref[...] = kn
    v_cur = _wide(vc_ref, 0, spb)
    vout_ref[...] = v_cur.astype(F32)
    kband = jnp.concatenate([_chunk_rms(_wide(kp_ref, 0, spb).astype(F32), bd, kw_ref[...]), kn],
                            axis=0).astype(BF16)
    vband = jnp.concatenate([_wide(vp_ref, 0, spb), v_cur], axis=0)
    lane_head = lax.broadcasted_iota(jnp.int32, (blk, slab), 1) // hd

    def scores(g):
        qn = _chunk_rms(_wide(q_ref, g * spb, spb).astype(F32), bd, qw_ref[...]) * (hd ** -0.5)
        qs = jnp.concatenate([jnp.where(lane_head == j, qn, 0.0) for j in range(kvh)], axis=0).astype(BF16)
        return lax.dot_general(qs, kband, (((1,), (1,)), ((), ())), preferred_element_type=F32)

    s_groups = [scores(g) for g in range(groups)]
    gated = []
    for g in range(groups):
        s_all = s_groups[g]
        probs = []
        for j in range(kvh):
            p = g * kvh + j
            s = s_all[j * blk:(j + 1) * blk] + bias_ref[p]
            sink = sink_ref[layer, p]
            m = jnp.maximum(jnp.max(s, axis=-1, keepdims=True), sink)
            e = jnp.exp(s - m)
            den = jnp.sum(e, axis=-1, keepdims=True) + jnp.exp(sink - m)
            probs.append((e * (1.0 / den)).astype(BF16))
        pv = jnp.dot(jnp.concatenate(probs, axis=0), vband, preferred_element_type=F32)
        acc = pv[(kvh - 1) * blk:]
        for j in range(kvh - 2, -1, -1):
            acc = jnp.where(lane_head == j, pv[j * blk:(j + 1) * blk], acc)
        gated.append(acc * _sigmoid(_wide(gb_ref, g * spb, spb).astype(F32)))
    for b, blk_out in enumerate(_natural_blocks(gated, hd)):
        o_ref[:, b * LANES:(b + 1) * LANES] = blk_out.astype(BF16)


def _swa_prompt(proj, sinks, qw, kw, bd, bias, *, layer, batch, seq, window, kvh, hd, nq, col):
    nb = seq // window
    slab = kvh * hd
    nheads = nq // hd
    qb, sb = nq // LANES, slab // LANES
    cur = lambda name, n: pl.BlockSpec((n, window, LANES), lambda b, c, o=col[name] // n: (o, b * nb + c, 0))
    prev = lambda name, n: pl.BlockSpec(
        (n, window, LANES), lambda b, c, o=col[name] // n: (o, b * nb + jnp.maximum(c - 1, 0), 0))
    const2 = lambda b, c: (0, 0)
    lay3 = lambda b, c: (layer, 0, 0)
    return pl.pallas_call(
        functools.partial(_swa_prompt_kernel, kvh=kvh, hd=hd, layer=layer),
        grid=(batch, nb),
        in_specs=[
            pl.BlockSpec(memory_space=pltpu.SMEM),
            cur("sq", qb), prev("sk", sb), cur("sk", sb), prev("sv", sb), cur("sv", sb), cur("gb", qb),
            pl.BlockSpec((None, 1, slab), lay3),
            pl.BlockSpec((None, 1, slab), lay3),
            pl.BlockSpec((slab, slab), const2),
            pl.BlockSpec((None, nheads, window, 2 * window), lambda b, c: (jnp.minimum(c, 1), 0, 0, 0)),
        ],
        out_specs=[
            pl.BlockSpec((window, nq), lambda b, c: (b * nb + c, 0)),
            pl.BlockSpec((None, window, slab), lambda b, c: (b, 0, 0)),
            pl.BlockSpec((None, window, slab), lambda b, c: (b, 0, 0)),
        ],
        out_shape=[
            jax.ShapeDtypeStruct((batch * seq, nq), BF16),
            jax.ShapeDtypeStruct((batch, window, slab), F32),
            jax.ShapeDtypeStruct((batch, window, slab), F32),
        ],
        compiler_params=pltpu.CompilerParams(
            dimension_semantics=("parallel", "arbitrary"), vmem_limit_bytes=VMEM_LIMIT),
        name="swa_prompt",
    )(sinks, proj, proj, proj, proj, proj, proj, qw, kw, bd, bias)


def _swa_sample_kernel(q_ref, k_ref, v_ref, gb_ref, ck_ref, cv_ref, qw_ref, kw_ref, bd_ref, bias_ref, sink_ref,
                       *rest, kvh, hd, seq):
    o_ref, kout_ref, vout_ref = rest[-3:]
    rows = q_ref.shape[1]
    nseq = rows // seq
    slab = kvh * hd
    spb = slab // LANES
    groups = q_ref.shape[0] // spb
    window = ck_ref.shape[1]
    nkeys = bias_ref.shape[1]
    bd = bd_ref[...]
    kn = _chunk_rms(_wide(k_ref, 0, spb).astype(F32), bd, kw_ref[...]).reshape(nseq, seq, slab)
    vn = _wide(v_ref, 0, spb).astype(F32).reshape(nseq, seq, slab)
    ck = ck_ref[...]
    cv = cv_ref[...]
    kout_ref[...] = jnp.concatenate([ck[:, seq:, :], kn], axis=1)
    vout_ref[...] = jnp.concatenate([cv[:, seq:, :], vn], axis=1)
    zpad = jnp.zeros((nseq, nkeys - window - seq, slab), F32)
    k_all = jnp.concatenate([ck, kn, zpad], axis=1).astype(BF16)
    v_all = jnp.concatenate([cv, vn, zpad], axis=1).astype(BF16)

    lane_head = lax.broadcasted_iota(jnp.int32, (rows, slab), 1) // hd
    pieces = []
    for g in range(groups):
        qn = _chunk_rms(_wide(q_ref, g * spb, spb).astype(F32), bd, qw_ref[...]) * (hd ** -0.5)
        for j in range(kvh):
            pieces.append(jnp.where(lane_head == j, qn, 0.0).reshape(nseq, seq, slab))
    q_all = jnp.concatenate(pieces, axis=1).astype(BF16)
    s = jnp.einsum("bqd,bkd->bqk", q_all, k_all, preferred_element_type=F32) + bias_ref[...][None]
    sink = sink_ref[...][None]
    m = jnp.maximum(jnp.max(s, axis=-1, keepdims=True), sink)
    e = jnp.exp(s - m)
    den = jnp.sum(e, axis=-1, keepdims=True) + jnp.exp(sink - m)
    pv = jnp.einsum("bqk,bkd->bqd", (e * (1.0 / den)).astype(BF16), v_all, preferred_element_type=F32)
    lane_head3 = lax.broadcasted_iota(jnp.int32, (nseq, seq, slab), 2) // hd
    gated = []
    for g in range(groups):
        acc = jnp.zeros((nseq, seq, slab), F32)
        for j in range(kvh):
            p = g * kvh + j
            acc = acc + jnp.where(lane_head3 == j, pv[:, p * seq:(p + 1) * seq, :], 0.0)
        gated.append(acc.reshape(rows, slab) * _sigmoid(_wide(gb_ref, g * spb, spb).astype(F32)))
    for b, blk_out in enumerate(_natural_blocks(gated, hd)):
        o_ref[:, b * LANES:(b + 1) * LANES] = blk_out.astype(BF16)


def _swa_sample(proj, cache_k, cache_v, qw, kw, bd, bias, sink_col, kv_prev, *, layer, row0, batch, seq, kvh, hd,
                nq, col, nseq):
    rows = nseq * seq
    slab = kvh * hd
    depth, _, window, _ = cache_k.shape
    r0 = row0 // rows
    cache_blk = pl.BlockSpec((None, nseq, window, slab), lambda b: (layer, b, 0, 0))
    extra_in = [] if kv_prev is None else list(kv_prev)
    qb, sb = nq // LANES, slab // LANES
    cur = lambda name, n: pl.BlockSpec((n, rows, LANES), lambda b, o=col[name] // n: (o, r0 + b, 0))
    const2 = lambda b: (0, 0)
    lay3 = lambda b: (layer, 0, 0)
    return pl.pallas_call(
        functools.partial(_swa_sample_kernel, kvh=kvh, hd=hd, seq=seq),
        grid=(batch // nseq,),
        in_specs=[
            cur("sq", qb), cur("sk", sb), cur("sv", sb), cur("gb", qb),
            cache_blk, cache_blk,
            pl.BlockSpec((None, 1, slab), lay3),
            pl.BlockSpec((None, 1, slab), lay3),
            pl.BlockSpec((slab, slab), const2),
            pl.BlockSpec(bias.shape, const2),
            pl.BlockSpec((None,) + sink_col.shape[1:], lay3),
        ] + [pl.BlockSpec(memory_space=pl.ANY)] * len(extra_in),
        out_specs=[
            pl.BlockSpec((rows, nq), lambda b: (b, 0)),
            cache_blk, cache_blk,
        ],
        out_shape=[
            jax.ShapeDtypeStruct((batch * seq, nq), BF16),
            jax.ShapeDtypeStruct((depth, batch, window, slab), F32),
            jax.ShapeDtypeStruct((depth, batch, window, slab), F32),
        ],
        input_output_aliases={11: 1, 12: 2} if extra_in else {},
        compiler_params=pltpu.CompilerParams(
            dimension_semantics=("parallel",), vmem_limit_bytes=VMEM_LIMIT),
        name="swa_sample",
    )(proj, proj, proj, proj, cache_k, cache_v, qw, kw, bd, bias, sink_col, *extra_in)


def _dense_kernel(hp_ref, hs_ref, oap_ref, oas_ref, obp_ref, obs_ref, pp_ref, ps_ref, wo_ref, nf_ref,
                  wu_ref, wd_ref, np_ref, wg_ref, wp_ref, outp_ref, outs_ref, *, npt, ff_chunk):
    is_prompt = pl.program_id(0) < npt
    oa = jnp.where(is_prompt, oap_ref[...], oas_ref[...])
    ob = jnp.where(is_prompt, obp_ref[...], obs_ref[...])
    h = jnp.where(is_prompt, hp_ref[...], hs_ref[...])
    h = h + jnp.dot(oa + ob, wo_ref[...], preferred_element_type=F32)
    xn = (_rms(h) * nf_ref[...]).astype(BF16)
    acc = jnp.zeros_like(h)
    for c in range(wu_ref.shape[1] // ff_chunk):
        sl = slice(c * ff_chunk, (c + 1) * ff_chunk)
        hid = jnp.maximum(jnp.dot(xn, wu_ref[:, sl], preferred_element_type=F32), 0.0)
        acc = acc + jnp.dot((hid * hid).astype(BF16), wd_ref[sl, :], preferred_element_type=F32)
    h = h + acc
    xn = (_rms(h) * np_ref[...]).astype(BF16)
    gate = _sigmoid(jnp.dot(xn, wg_ref[...], preferred_element_type=F32))
    p = jnp.where(is_prompt, pp_ref[...], ps_ref[...])
    pe = jnp.dot(p.astype(BF16), wp_ref[...], preferred_element_type=F32)
    out = h + gate * pe

    @pl.when(is_prompt)
    def _():
        outp_ref[...] = out

    @pl.when(jnp.logical_not(is_prompt))
    def _():
        outs_ref[...] = out


def _dense(h_p, h_s, oa_p, oa_s, ob_p, ob_s, p_p, p_s, wo, nf, wu, wd, npl, wg, wp, *, layer, tm):
    tp, d = h_p.shape
    ts = h_s.shape[0]
    npt = tp // tm
    rows_p = lambda m: (jnp.minimum(m, npt - 1), 0)
    rows_s = lambda m: (jnp.maximum(m - npt, 0), 0)
    resident = lambda a: pl.BlockSpec((None,) + a.shape[1:], lambda m: (layer, 0, 0), pipeline_mode=pl.Buffered(1))
    return pl.pallas_call(
        functools.partial(_dense_kernel, npt=npt, ff_chunk=1024),
        grid=((tp + ts) // tm,),
        in_specs=[
            pl.BlockSpec((tm, d), rows_p), pl.BlockSpec((tm, d), rows_s),
            pl.BlockSpec((tm, d), rows_p), pl.BlockSpec((tm, d), rows_s),
            pl.BlockSpec((tm, d), rows_p), pl.BlockSpec((tm, d), rows_s),
            pl.BlockSpec((None, tm, p_p.shape[2]), lambda m: (layer, jnp.minimum(m, npt - 1), 0)),
            pl.BlockSpec((None, tm, p_s.shape[2]), lambda m: (layer, jnp.maximum(m - npt, 0), 0)),
            resident(wo), resident(nf), resident(wu), resident(wd),
            resident(npl), resident(wg), resident(wp),
        ],
        out_specs=[pl.BlockSpec((tm, d), rows_p), pl.BlockSpec((tm, d), rows_s)],
        out_shape=[jax.ShapeDtypeStruct((tp, d), F32), jax.ShapeDtypeStruct((ts, d), F32)],
        compiler_params=pltpu.CompilerParams(
            dimension_semantics=("arbitrary",), vmem_limit_bytes=VMEM_LIMIT),
        name="dense",
    )(h_p, h_s, oa_p, oa_s, ob_p, ob_s, p_p, p_s, wo, nf, wu, wd, npl, wg, wp)


def _seq_mask(seq_len):
    i = np.arange(GDN_CHUNK)
    m = (i[:, None] // seq_len == i[None, :] // seq_len) & (i[None, :] <= i[:, None])
    return m.astype(np.float32)


def kernel(x_prompt, x_sample, cache_conv, state_gdn, cache_swa_k, cache_swa_v, p_prompt, p_sample, norm_mix, w_in, conv_w, a_log, dt_bias, gdn_norm, q_norm, k_norm, attn_sinks, w_out, norm_ffn, w_up, w_down, norm_ple, w_ple_gate, w_ple_proj):
    batch, seq, d = x_prompt.shape
    dbatch, dseq, _ = x_sample.shape
    depth = w_in.shape[0]
    heads, dk, dv = state_gdn.shape[2:]
    window, kvh, hd = cache_swa_k.shape[2:]
    nq = d
    nheads = nq // hd
    groups = nheads // kvh
    slab = kvh * hd
    key_dim = heads * dk
    val_dim = heads * dv
    conv_dim = 2 * key_dim + val_dim
    width = conv_w.shape[1]
    tp, ts = batch * seq, dbatch * dseq
    assert dk == LANES and dv == LANES and val_dim == d and key_dim == d and slab % LANES == 0
    assert seq % window == 0 and window == GDN_CHUNK and GDN_CHUNK % dseq == 0 and dseq >= width - 1
    assert dseq == SUBLANES and ts % GDN_CHUNK == 0

    o_z = conv_dim
    o_b = o_z + val_dim
    o_a = o_b + heads
    o_sq = o_a + heads
    o_sk = o_sq + nq
    o_sv = o_sk + slab
    o_g = o_sv + slab
    col_h = {"q": 0, "k": 1, "v": 2, "z": 0, "ga": 2}
    off = {"sq": val_dim, "gb": val_dim + nq + d, "sk": val_dim + nq + 2 * d, "sv": val_dim + nq + 2 * d + slab}
    col_b = {name: o // LANES for name, o in off.items()}
    cb = conv_dim // LANES

    h_p = x_prompt.reshape(tp, d)
    h_s = x_sample.reshape(ts, d)
    bf = lambda a: a.astype(BF16)
    wo_all, wu_all, wd_all = bf(w_out), bf(w_up), bf(w_down)
    wg_all, wp_all = bf(w_ple_gate), bf(w_ple_proj)
    nf_all, np_all = norm_ffn[:, None, :], norm_ple[:, None, :]
    p_p = p_prompt.reshape(depth, tp, -1)
    p_s = p_sample.reshape(depth, ts, -1)

    masks = jnp.asarray(np.stack([_seq_mask(GDN_CHUNK), _seq_mask(dseq)]), BF16)
    mincl_p = jnp.asarray(_seq_mask(GDN_CHUNK), F32)
    mincl_s = jnp.asarray(_seq_mask(dseq), F32)
    bd = jnp.asarray(np.kron(np.eye(kvh), np.full((hd, hd), 1.0 / hd)), BF16)

    head_of = np.array([(p % kvh) * groups + p // kvh for p in range(nheads)])
    slopes = jnp.exp2(-8.0 * (jnp.asarray(head_of, F32) + 1.0) / nheads)
    qi = np.arange(window)[:, None]
    kj = np.arange(2 * window)[None, :]
    dist = window + qi - kj
    ok = (dist >= 0) & (dist <= window)
    ok_first = ok & (kj >= window)
    dist_f = jnp.asarray(dist, F32)
    bias_p = jnp.stack([
        jnp.where(jnp.asarray(okv)[None], -slopes[:, None, None] * dist_f[None], -jnp.inf) for okv in (ok_first, ok)])
    nkeys = -(-(window + dseq) // 16) * 16
    ti = np.tile(np.arange(dseq), nheads)[:, None]
    sj = np.arange(nkeys)[None, :]
    dist_s = ti + window - sj
    ok_s = (dist_s >= 0) & (dist_s <= window) & (sj < window + dseq)
    slopes_rows = jnp.repeat(slopes, dseq)[:, None]
    bias_s = jnp.where(jnp.asarray(ok_s), -slopes_rows * jnp.asarray(dist_s, F32), -jnp.inf)

    hist_all = jnp.pad(cache_conv, ((0, 0), (0, 0), (dseq - cache_conv.shape[2], 0), (0, 0)))
    hist_all = hist_all.reshape(depth, ts, conv_dim)
    ck_all = cache_swa_k.reshape(depth, dbatch, window, slab)
    cv_all = cache_swa_v.reshape(depth, dbatch, window, slab)
    cw_all = jnp.transpose(conv_w.reshape(depth, width, cb, LANES), (0, 2, 1, 3))

    avec_all = (jnp.zeros((depth, 2, LANES), F32).at[:, 0, heads:2 * heads].set(a_log)
                .at[:, 1, heads:2 * heads].set(dt_bias))
    nm_all, gn_all = norm_mix[:, None, :], gdn_norm[:, None, :]
    sinks_all = attn_sinks[:, head_of]
    sink_col_all = jnp.repeat(sinks_all, dseq, axis=1)[:, :, None]
    qw_all = jnp.tile(q_norm, (1, kvh))[:, None, :]
    kw_all = jnp.tile(k_norm, (1, kvh))[:, None, :]
    w_t = jnp.swapaxes(w_in, 1, 2)
    wba_all = jnp.pad(w_t[:, o_b:o_sq], ((0, 0), (0, LANES - 2 * heads), (0, 0))).astype(BF16)

    def regroup(x):
        return jnp.swapaxes(x.reshape(depth, kvh, groups, hd, d), 1, 2).reshape(x.shape)

    w_main_all = jnp.concatenate([
        w_t[:, :o_b], regroup(w_t[:, o_sq:o_sk]), w_t[:, o_g:o_g + d], regroup(w_t[:, o_g + d:]),
        w_t[:, o_sk:o_g]], axis=1).astype(BF16)

    outs = {n: [] for n in ("conv_p", "gdn_p", "k_p", "v_p", "conv_s")}
    gdn_s = None
    kv_s = None
    for i in range(depth):
        proj32, proj16, bgc, tails = _inproj(h_p, h_s, nm_all, w_main_all, wba_all, avec_all, masks, cw_all, layer=i,
                                             heads=heads, tm=1024, tn=1536, n_f32=conv_dim, seq=seq)

        oa_p, gdn_p = _gdn_prompt(proj32, proj16, bgc, gn_all, mincl_p, layer=i, batch=batch, seq=seq,
                                  heads=heads, dk=dk, col=col_h, rows=512, heads_per_iter=4)
        oa_s, gdn_s, raw_s = _gdn_sample(proj32, proj16, bgc, hist_all, state_gdn, cw_all, gn_all, mincl_s, gdn_s, layer=i,
                                  row0=tp, batch=dbatch, seq=dseq, heads=heads, dk=dk, col=col_h, heads_per_step=4)

        ob_p, k_p, v_p = _swa_prompt(proj16, sinks_all, qw_all, kw_all, bd, bias_p, layer=i, batch=batch, seq=seq,
                                     window=window, kvh=kvh, hd=hd, nq=nq, col=col_b)
        ob_s, *kv_s = _swa_sample(proj16, ck_all, cv_all, qw_all, kw_all, bd, bias_s, sink_col_all, kv_s,
                                  layer=i, row0=tp, batch=dbatch, seq=dseq, kvh=kvh, hd=hd, nq=nq, col=col_b, nseq=8)

        h_p, h_s = _dense(h_p, h_s, oa_p, oa_s, ob_p, ob_s, p_p, p_s, wo_all, nf_all, wu_all, wd_all, np_all,
                          wg_all, wp_all, layer=i, tm=512)

        outs["conv_p"].append(tails[seq // 1024 - 1:tp // 1024:seq // 1024])
        outs["conv_s"].append(raw_s)
        outs["gdn_p"].append(gdn_p)
        outs["k_p"].append(k_p.reshape(batch, window, kvh, hd))
        outs["v_p"].append(v_p.reshape(batch, window, kvh, hd))

    st = lambda n: jnp.stack(outs[n])
    conv_p = jnp.transpose(st("conv_p")[:, :, :, SUBLANES - (width - 1):, :], (0, 1, 3, 2, 4))
    conv_p = conv_p.reshape(depth, batch, width - 1, conv_dim)
    conv_s = st("conv_s").reshape(depth, 3, dbatch, dseq, key_dim)[:, :, :, dseq - (width - 1):, :]
    conv_s = jnp.transpose(conv_s, (0, 2, 3, 1, 4)).reshape(depth, dbatch, width - 1, conv_dim)
    return (h_p.reshape(batch, seq, d), h_s.reshape(dbatch, dseq, d),
            conv_p, st("gdn_p"), st("k_p"), st("v_p"), conv_s, gdn_s,
            kv_s[0].reshape(depth, dbatch, window, kvh, hd), kv_s[1].reshape(depth, dbatch, window, kvh, hd))
```

```python
import functools

import numpy as np
import jax
import jax.numpy as jnp
from jax import lax
from jax.experimental import pallas as pl
from jax.experimental.pallas import tpu as pltpu

F32 = jnp.float32
BF16 = jnp.bfloat16
EPS = 1e-6
LANES = 128
SUBLANES = 8
MXU_N = 256
GDN_CHUNK = 128
VMEM_LIMIT = 56 * 1024 * 1024


def _sigmoid(x):
    return 0.5 * jnp.tanh(0.5 * x) + 0.5


def _silu(x):
    hx = 0.5 * x
    return hx + hx * jnp.tanh(hx)


def _dot(a, b):
    return jnp.dot(a.astype(BF16), b.astype(BF16), preferred_element_type=F32)


def _dot_nt(a, b):
    return lax.dot_general(a.astype(BF16), b.astype(BF16), (((1,), (1,)), ((), ())), preferred_element_type=F32)


def _dot_tn(a, b):
    return lax.dot_general(a.astype(BF16), b.astype(BF16), (((0,), (0,)), ((), ())), preferred_element_type=F32)


def _rms(x):
    return x * lax.rsqrt(jnp.mean(x * x, axis=-1, keepdims=True) + EPS)


def _l2(x):
    return x * lax.rsqrt(jnp.sum(x * x, axis=-1, keepdims=True) + EPS)


def _inproj_kernel(xp_ref, xs_ref, nw_ref, w_ref, wba_ref, avec_ref, mask_ref, proj32_ref, proj16_ref, bgc_ref,
                   xn_ref, *, heads, npt, n32):
    n = pl.program_id(1)

    @pl.when(n == 0)
    def _():
        x = jnp.where(pl.program_id(0) < npt, xp_ref[...], xs_ref[...])
        xn = (_rms(x) * nw_ref[...]).astype(BF16)
        xn_ref[...] = xn
        ba = _dot_nt(xn, wba_ref[...])
        beta = _sigmoid(ba)
        xs = ba + avec_ref[1:2, :]
        softplus = jnp.maximum(xs, 0.0) + jnp.log1p(jnp.exp(-jnp.abs(xs)))
        g = -jnp.exp(avec_ref[0:1, :]) * softplus
        g1 = g.astype(BF16)
        r1 = g - g1.astype(F32)
        g2 = r1.astype(BF16)
        g3 = (r1 - g2.astype(F32)).astype(BF16)
        m = mask_ref[...]
        rows = g.shape[0]
        lane = lax.broadcasted_iota(jnp.int32, (GDN_CHUNK, LANES), 1)
        for r in range(rows // GDN_CHUNK):
            sl = slice(r * GDN_CHUNK, (r + 1) * GDN_CHUNK)
            gc = (jnp.dot(m, g1[sl], preferred_element_type=F32)
                  + jnp.dot(m, g2[sl], preferred_element_type=F32)
                  + jnp.dot(m, g3[sl], preferred_element_type=F32))
            bgc_ref[sl, :] = jnp.where(lane < heads, beta[sl], gc)

    def project(out_ref):
        xn = xn_ref[...]
        for j in range(w_ref.shape[0] // MXU_N):
            r = _dot_nt(xn, w_ref[j * MXU_N:(j + 1) * MXU_N, :])
            for i in range(MXU_N // LANES):
                out_ref[j * (MXU_N // LANES) + i] = r[:, i * LANES:(i + 1) * LANES].astype(out_ref.dtype)

    @pl.when(n < n32)
    def _():
        project(proj32_ref)

    @pl.when(n >= n32)
    def _():
        project(proj16_ref)


def _inproj(h_p, h_s, nw, w, wba, avec, masks, *, layer, heads, tm, tn, n_f32):
    d = h_p.shape[1]
    t_all = h_p.shape[0] + h_s.shape[0]
    n_out = w.shape[1]
    npt = h_p.shape[0] // tm
    n32 = n_f32 // tn
    assert n32 * tn == n_f32
    return pl.pallas_call(
        functools.partial(_inproj_kernel, heads=heads, npt=npt, n32=n32),
        grid=(t_all // tm, n_out // tn),
        in_specs=[
            pl.BlockSpec((tm, d), lambda m, n: (jnp.minimum(m, npt - 1), 0)),
            pl.BlockSpec((tm, d), lambda m, n: (jnp.maximum(m - npt, 0), 0)),
            pl.BlockSpec((None, 1, d), lambda m, n: (layer, 0, 0)),
            pl.BlockSpec((None, tn, d), lambda m, n: (layer, n, 0)),
            pl.BlockSpec((None, LANES, d), lambda m, n: (layer, 0, 0)),
            pl.BlockSpec((None, 2, LANES), lambda m, n: (layer, 0, 0)),
            pl.BlockSpec((None, GDN_CHUNK, GDN_CHUNK), lambda m, n: (jnp.where(m >= npt, 1, 0), 0, 0)),
        ],
        out_specs=[
            pl.BlockSpec((tn // LANES, tm, LANES), lambda m, n: (jnp.minimum(n, n32 - 1), m, 0)),
            pl.BlockSpec((tn // LANES, tm, LANES), lambda m, n: (jnp.maximum(n - n32, 0), m, 0)),
            pl.BlockSpec((tm, LANES), lambda m, n: (m, 0)),
        ],
        out_shape=[
            jax.ShapeDtypeStruct((n_f32 // LANES, t_all, LANES), F32),
            jax.ShapeDtypeStruct(((n_out - n_f32) // LANES, t_all, LANES), BF16),
            jax.ShapeDtypeStruct((t_all, LANES), F32),
        ],
        scratch_shapes=[pltpu.VMEM((tm, d), BF16)],
        compiler_params=pltpu.CompilerParams(
            dimension_semantics=("parallel", "arbitrary"), vmem_limit_bytes=VMEM_LIMIT),
        name="inproj",
    )(h_p, h_s, nw, w, wba, avec, masks)


def _gdn_chunks_local(qs, ks, vs, betas, gcs, mincl, seq_len):
    n = len(qs)
    c = qs[0].shape[0]
    dv = vs[0].shape[1]
    row = lax.broadcasted_iota(jnp.int32, (c, c), 0)
    col = lax.broadcasted_iota(jnp.int32, (c, c), 1)
    decs, egs = [], []
    for gc in gcs:
        gcb = jnp.broadcast_to(gc, (c, c))
        decs.append(jnp.exp(jnp.where(mincl > 0, gcb - gcb.T, -jnp.inf)))
        egs.append(jnp.exp(gc))
    kbs = [ks[i] * betas[i] for i in range(n)]
    kks = [_dot_nt(jnp.concatenate([kbs[i], qs[i]], axis=0), ks[i]) for i in range(n)]
    a_s = [jnp.where(row == col, 0.0, kks[i][:c] * decs[i]) for i in range(n)]
    qks = [kks[i][c:] * decs[i] for i in range(n)]
    same = (row >> 1) == (col >> 1)
    eye = jnp.where(row == col, 1.0, 0.0)
    ts = [eye - jnp.where(same, a, 0.0) for a in a_s]
    na16 = [(-a).astype(BF16) for a in a_s]
    for lvl in range(1, seq_len.bit_length() - 1):
        wider = (row >> (lvl + 1)) == (col >> (lvl + 1))
        sel = wider & jnp.logical_not(same)
        t16 = [t.astype(BF16) for t in ts]
        mids = [jnp.dot(na16[i], t16[i], preferred_element_type=F32) for i in range(n)]
        ts = [jnp.where(sel, jnp.dot(t16[i], mids[i].astype(BF16), preferred_element_type=F32), ts[i])
              for i in range(n)]
        same = wider
    sols = [_dot(ts[i], jnp.concatenate([vs[i] * betas[i], kbs[i] * egs[i]], axis=1)) for i in range(n)]
    return [s[:, :dv] for s in sols], [s[:, dv:] for s in sols], qks, egs


def _conv_taps(u, ru_fn, cw):
    width = cw.shape[0]
    out = u * cw[width - 1:width, :]
    for j in range(1, width):
        out = out + ru_fn(j) * cw[width - 1 - j:width - j, :]
    return _silu(out)


def _gdn_finish(o, z, ga, gn):
    return (_rms(o) * gn * _silu(z.astype(F32)) * _sigmoid(ga.astype(F32))).astype(BF16)


def _head_scalars(bgc, h, heads):
    lane = lax.broadcasted_iota(jnp.int32, bgc.shape, 1)
    beta = jnp.sum(jnp.where(lane == h, bgc, 0.0), axis=-1, keepdims=True)
    gc = jnp.sum(jnp.where(lane == h + heads, bgc, 0.0), axis=-1, keepdims=True)
    return beta, gc


def _gdn_prompt_kernel(q_ref, k_ref, v_ref, z_ref, ga_ref, bgc_ref, cw_ref, gn_ref, mincl_ref,
                       o_ref, sfin_ref, tails_ref,
                       s_ref, tail_ref, u_s, wqe_s, qk_s, kd_s, el_s, *, heads_per_iter):
    c_id = pl.program_id(1)
    heads, rows, dk = q_ref.shape
    nchunk = rows // GDN_CHUNK
    cc = GDN_CHUNK

    @pl.when(c_id == 0)
    def _():
        s_ref[...] = jnp.zeros_like(s_ref)
        tail_ref[...] = jnp.zeros_like(tail_ref)

    row8 = lax.broadcasted_iota(jnp.int32, (SUBLANES, dk), 0)
    bgc = bgc_ref[...]
    mincl = mincl_ref[...]

    def conv(kind, h, u_ref):
        u = u_ref[h]
        tail = tail_ref[kind * heads + h]

        def delayed(j):
            head = jnp.where(row8 < j, pltpu.roll(tail, j, 0), pltpu.roll(u[:SUBLANES], j, 0))
            return jnp.concatenate([head, u_ref[h, pl.ds(SUBLANES - j, rows - SUBLANES), :]], axis=0)

        out = _conv_taps(u, delayed, cw_ref[kind * heads + h])
        tail_ref[kind * heads + h] = u[rows - SUBLANES:]
        return out

    def local_body(it, carry):
        hs, qs, ks, vs, betas, gcs = [], [], [], [], [], []
        for hh in range(heads_per_iter):
            h = it * heads_per_iter + hh
            q = _l2(conv(0, h, q_ref)) * (dk ** -0.5)
            k = _l2(conv(1, h, k_ref))
            v = conv(2, h, v_ref)
            beta, gc = _head_scalars(bgc, h, heads)
            for ci in range(nchunk):
                sl = slice(ci * cc, (ci + 1) * cc)
                hs.append((h, ci))
                qs.append(q[sl]); ks.append(k[sl]); vs.append(v[sl]); betas.append(beta[sl]); gcs.append(gc[sl])
        us, ws, qks, egs = _gdn_chunks_local(qs, ks, vs, betas, gcs, mincl, cc)
        for i, (h, ci) in enumerate(hs):
            g_last = gcs[i][cc - 1:cc, :]
            u_s[h, ci] = us[i]
            wqe_s[h, ci, :cc] = ws[i].astype(BF16)
            wqe_s[h, ci, cc:] = (qs[i] * egs[i]).astype(BF16)
            qk_s[h, ci] = qks[i].astype(BF16)
            kd_s[h, ci] = (ks[i] * jnp.exp(g_last - gcs[i])).astype(BF16)
            el_s[h, ci] = jnp.broadcast_to(jnp.exp(g_last), (SUBLANES, dk))
        return carry

    lax.fori_loop(0, heads // heads_per_iter, local_body, 0)

    states = [s_ref[h] for h in range(heads)]
    gn = gn_ref[...]
    for ci in range(nchunk):
        sl = slice(ci * cc, (ci + 1) * cc)
        wqs = [jnp.dot(wqe_s[h, ci], states[h].astype(BF16), preferred_element_type=F32) for h in range(heads)]
        v_news = [u_s[h, ci] - wqs[h][:cc] for h in range(heads)]
        vn16 = [v.astype(BF16) for v in v_news]
        outs = [wqs[h][cc:] + jnp.dot(qk_s[h, ci], vn16[h], preferred_element_type=F32) for h in range(heads)]
        states = [states[h] * el_s[h, ci][0:1, :]
                  + lax.dot_general(kd_s[h, ci], vn16[h], (((0,), (0,)), ((), ())), preferred_element_type=F32)
                  for h in range(heads)]
        for h in range(heads):
            o_ref[sl, h * dk:(h + 1) * dk] = _gdn_finish(outs[h], z_ref[h, sl, :], ga_ref[h, sl, :], gn)
    for h in range(heads):
        s_ref[h] = states[h]
        sfin_ref[h] = states[h]
    for idx in range(3 * heads):
        tails_ref[:, idx * dk:(idx + 1) * dk] = tail_ref[idx]


def _gdn_prompt(proj32, proj16, bgc, cw, gn, mincl, *, layer, batch, seq, heads, dk, col, rows, heads_per_iter):
    ncb = seq // rows
    nchunk = rows // GDN_CHUNK
    blk = lambda name: pl.BlockSpec((heads, rows, dk), lambda b, c, o=col[name]: (o, b * ncb + c, 0))
    return pl.pallas_call(
        functools.partial(_gdn_prompt_kernel, heads_per_iter=heads_per_iter),
        grid=(batch, ncb),
        in_specs=[
            blk("q"), blk("k"), blk("v"), blk("z"), blk("ga"),
            pl.BlockSpec((rows, LANES), lambda b, c: (b * ncb + c, 0)),
            pl.BlockSpec((None,) + cw.shape[1:], lambda b, c: (layer, 0, 0, 0)),
            pl.BlockSpec((None, 1, dk), lambda b, c: (layer, 0, 0)),
            pl.BlockSpec((GDN_CHUNK, GDN_CHUNK), lambda b, c: (0, 0)),
        ],
        out_specs=[
            pl.BlockSpec((rows, heads * dk), lambda b, c: (b * ncb + c, 0)),
            pl.BlockSpec((None, heads, dk, dk), lambda b, c: (b, 0, 0, 0)),
            pl.BlockSpec((None, SUBLANES, 3 * heads * dk), lambda b, c: (b, 0, 0)),
        ],
        out_shape=[
            jax.ShapeDtypeStruct((batch * seq, heads * dk), BF16),
            jax.ShapeDtypeStruct((batch, heads, dk, dk), F32),
            jax.ShapeDtypeStruct((batch, SUBLANES, 3 * heads * dk), F32),
        ],
        scratch_shapes=[
            pltpu.VMEM((heads, dk, dk), F32),
            pltpu.VMEM((3 * heads, SUBLANES, dk), F32),
            pltpu.VMEM((heads, nchunk, GDN_CHUNK, dk), F32),
            pltpu.VMEM((heads, nchunk, 2 * GDN_CHUNK, dk), BF16),
            pltpu.VMEM((heads, nchunk, GDN_CHUNK, GDN_CHUNK), BF16),
            pltpu.VMEM((heads, nchunk, GDN_CHUNK, dk), BF16),
            pltpu.VMEM((heads, nchunk, SUBLANES, dk), F32),
        ],
        compiler_params=pltpu.CompilerParams(
            dimension_semantics=("parallel", "arbitrary"), vmem_limit_bytes=VMEM_LIMIT),
        name="gdn_prompt",
    )(proj32, proj32, proj32, proj16, proj16, bgc, cw, gn, mincl)


def _gdn_sample_kernel(q_ref, k_ref, v_ref, z_ref, ga_ref, bgc_ref, hq_ref, hk_ref, hv_ref,
                       cw_ref, gn_ref, mincl_ref, s0_ref, *rest, heads, seq):
    o_ref, sout_ref, raw_ref = rest[-3:]
    hpi, rows, dk = q_ref.shape
    nseq = rows // seq
    t_idx = lax.broadcasted_iota(jnp.int32, (nseq, seq, dk), 1)
    bgc = bgc_ref[...]

    def conv(kind, hh, h, u_ref, hist_ref):
        u = u_ref[hh]
        raw_ref[kind, :, hh * dk:(hh + 1) * dk] = u
        u3 = u.reshape(nseq, seq, dk)
        hist3 = hist_ref[:, hh * dk:(hh + 1) * dk].reshape(nseq, seq, dk)

        def delayed(j):
            xj = jnp.where(t_idx < j, pltpu.roll(hist3, j, 1), pltpu.roll(u3, j, 1))
            return xj.reshape(rows, dk)

        return _conv_taps(u, delayed, cw_ref[kind * heads + h])

    qs, ks, vs, betas, gcs = [], [], [], [], []
    for hh in range(hpi):
        h = pl.program_id(1) * hpi + hh
        qs.append(_l2(conv(0, hh, h, q_ref, hq_ref)) * (dk ** -0.5))
        ks.append(_l2(conv(1, hh, h, k_ref, hk_ref)))
        vs.append(conv(2, hh, h, v_ref, hv_ref))
        beta, gc = _head_scalars(bgc, h, heads)
        betas.append(beta)
        gcs.append(gc)
    us, ws, qks, egs = _gdn_chunks_local(qs, ks, vs, betas, gcs, mincl_ref[...], seq)

    pad = jnp.zeros((seq, dk), F32)
    pairs = [(hh, si) for hh in range(hpi) for si in range(nseq)]
    sl = lambda si: slice(si * seq, (si + 1) * seq)
    s0 = {p: s0_ref[p[1], p[0]] for p in pairs}
    qes = [qs[hh] * egs[hh] for hh in range(hpi)]
    wqs = {(hh, si): _dot(jnp.concatenate([ws[hh][sl(si)], qes[hh][sl(si)]], axis=0), s0[(hh, si)])
           for hh, si in pairs}
    v_news = {(hh, si): us[hh][sl(si)] - wqs[(hh, si)][:seq] for hh, si in pairs}
    for hh, si in pairs:
        g_last = gcs[hh][(si + 1) * seq - 1:(si + 1) * seq, :]
        kd = ks[hh][sl(si)] * jnp.exp(g_last - gcs[hh][sl(si)])
        sout_ref[si, hh] = s0[(hh, si)] * jnp.exp(g_last) + _dot_tn(
            jnp.concatenate([kd, pad], axis=0), jnp.concatenate([v_news[(hh, si)], pad], axis=0))
    gn = gn_ref[...]
    for hh in range(hpi):
        o = (jnp.concatenate([wqs[(hh, si)][seq:] for si in range(nseq)], axis=0)
             + _dot(qks[hh], jnp.concatenate([v_news[(hh, si)] for si in range(nseq)], axis=0)))
        o_ref[:, hh * dk:(hh + 1) * dk] = _gdn_finish(o, z_ref[hh], ga_ref[hh], gn)


def _gdn_sample(proj32, proj16, bgc, hist, s0, cw, gn, mincl, s_prev, *, layer, row0, batch, seq, heads, dk, col,
                heads_per_step):
    rows = GDN_CHUNK
    nseq = rows // seq
    r0 = row0 // rows
    depth = s0.shape[0]
    hpi = heads_per_step
    hgroups = heads // hpi
    blk = lambda name: pl.BlockSpec((hpi, rows, dk), lambda b, h, o=col[name]: (o * hgroups + h, r0 + b, 0))
    hblk = lambda name: pl.BlockSpec((None, rows, hpi * dk), lambda b, h, o=col[name]: (layer, b, o * hgroups + h))
    state_blk = pl.BlockSpec((None, nseq, hpi, dk, dk), lambda b, h: (layer, b, h, 0, 0))
    extra_in = [] if s_prev is None else [s_prev]
    return pl.pallas_call(
        functools.partial(_gdn_sample_kernel, heads=heads, seq=seq),
        grid=(batch * seq // rows, hgroups),
        in_specs=[
            blk("q"), blk("k"), blk("v"), blk("z"), blk("ga"),
            pl.BlockSpec((rows, LANES), lambda b, h: (r0 + b, 0)),
            hblk("q"), hblk("k"), hblk("v"),
            pl.BlockSpec((None,) + cw.shape[1:], lambda b, h: (layer, 0, 0, 0)),
            pl.BlockSpec((None, 1, dk), lambda b, h: (layer, 0, 0)),
            pl.BlockSpec((GDN_CHUNK, GDN_CHUNK), lambda b, h: (0, 0)),
            state_blk,
        ] + [pl.BlockSpec(memory_space=pl.ANY)] * len(extra_in),
        out_specs=[
            pl.BlockSpec((rows, hpi * dk), lambda b, h: (b, h)),
            state_blk,
            pl.BlockSpec((3, rows, hpi * dk), lambda b, h: (0, b, h)),
        ],
        out_shape=[
            jax.ShapeDtypeStruct((batch * seq, heads * dk), BF16),
            jax.ShapeDtypeStruct((depth, batch, heads, dk, dk), F32),
            jax.ShapeDtypeStruct((3, batch * seq, heads * dk), F32),
        ],
        input_output_aliases={13: 1} if extra_in else {},
        compiler_params=pltpu.CompilerParams(
            dimension_semantics=("parallel", "parallel"), vmem_limit_bytes=VMEM_LIMIT),
        name="gdn_sample",
    )(proj32, proj32, proj32, proj16, proj16, bgc, hist, hist, hist, cw, gn, mincl, s0, *extra_in)


def _chunk_rms(x, bd, w):
    ms = _dot(x * x, bd)
    return x * lax.rsqrt(ms + EPS) * w


def _wide(ref, first, n):
    return jnp.concatenate([ref[first + i] for i in range(n)], axis=-1)


def _natural_blocks(slabs, hd):
    groups = len(slabs)
    per_block = LANES // hd
    lane = lax.broadcasted_iota(jnp.int32, (slabs[0].shape[0], LANES), 1)
    blocks = []
    for b in range(slabs[0].shape[1] * groups // LANES):
        blk = None
        for i in range(per_block):
            j, g = divmod(b * per_block + i, groups)
            first = (hd * j) // LANES * LANES
            piece = slabs[g][:, first:first + LANES]
            shift = (i * hd - hd * j) % LANES
            if shift:
                piece = pltpu.roll(piece, shift, 1)
            blk = piece if blk is None else jnp.where(lane < i * hd, blk, piece)
        blocks.append(blk)
    return blocks


def _swa_prompt_kernel(sink_ref, q_ref, kp_ref, kc_ref, vp_ref, vc_ref, gb_ref, qw_ref, kw_ref, bd_ref, bias_ref,
                       o_ref, kout_ref, vout_ref, *, kvh, hd, layer):
    blk = kp_ref.shape[1]
    nsub = q_ref.shape[1] // blk
    slab = kvh * hd
    spb = slab // LANES
    groups = q_ref.shape[0] // spb
    bd = bd_ref[...]
    kn = _chunk_rms(_wide(kc_ref, 0, spb).astype(F32), bd, kw_ref[...])
    v_cur = _wide(vc_ref, 0, spb)
    kout_ref[...] = kn[(nsub - 1) * blk:]
    vout_ref[...] = v_cur[(nsub - 1) * blk:].astype(F32)
    k_all = jnp.concatenate([_chunk_rms(_wide(kp_ref, 0, spb).astype(F32), bd, kw_ref[...]), kn],
                            axis=0).astype(BF16)
    v_all = jnp.concatenate([_wide(vp_ref, 0, spb), v_cur], axis=0)
    lane_head = lax.broadcasted_iota(jnp.int32, (blk, slab), 1) // hd
    first_variant = jnp.minimum(pl.program_id(1), 1)

    for sub in range(nsub):
        rows = slice(sub * blk, (sub + 1) * blk)
        kband = k_all[sub * blk:(sub + 2) * blk]
        vband = v_all[sub * blk:(sub + 2) * blk]
        variant = first_variant if sub == 0 else 1

        def scores(g):
            qn = _chunk_rms(_wide(q_ref, g * spb, spb)[rows].astype(F32), bd, qw_ref[...]) * (hd ** -0.5)
            qs = jnp.concatenate([jnp.where(lane_head == j, qn, 0.0) for j in range(kvh)], axis=0).astype(BF16)
            return lax.dot_general(qs, kband, (((1,), (1,)), ((), ())), preferred_element_type=F32)

        s_groups = [scores(g) for g in range(groups)]
        gated = []
        for g in range(groups):
            s_all = s_groups[g]
            probs = []
            for j in range(kvh):
                p = g * kvh + j
                s = s_all[j * blk:(j + 1) * blk] + bias_ref[variant, p]
                sink = sink_ref[layer, p]
                m = jnp.maximum(jnp.max(s, axis=-1, keepdims=True), sink)
                e = jnp.exp(s - m)
                den = jnp.sum(e, axis=-1, keepdims=True) + jnp.exp(sink - m)
                probs.append((e * (1.0 / den)).astype(BF16))
            pv = jnp.dot(jnp.concatenate(probs, axis=0), vband, preferred_element_type=F32)
            acc = pv[(kvh - 1) * blk:]
            for j in range(kvh - 2, -1, -1):
                acc = jnp.where(lane_head == j, pv[j * blk:(j + 1) * blk], acc)
            gated.append(acc * _sigmoid(_wide(gb_ref, g * spb, spb)[rows].astype(F32)))
        for b, blk_out in enumerate(_natural_blocks(gated, hd)):
            o_ref[rows, b * LANES:(b + 1) * LANES] = blk_out.astype(BF16)


def _swa_prompt(proj, sinks, qw, kw, bd, bias, *, layer, batch, seq, window, kvh, hd, nq, col, nsub):
    step = nsub * window
    nb = seq // step
    slab = kvh * hd
    qb, sb = nq // LANES, slab // LANES
    cur = lambda name, n: pl.BlockSpec((n, step, LANES), lambda b, c, o=col[name] // n: (o, b * nb + c, 0))
    prev = lambda name, n: pl.BlockSpec(
        (n, window, LANES), lambda b, c, o=col[name] // n: (o, (b * nb + c) * nsub - jnp.minimum(c, 1), 0))
    const2 = lambda b, c: (0, 0)
    lay3 = lambda b, c: (layer, 0, 0)
    return pl.pallas_call(
        functools.partial(_swa_prompt_kernel, kvh=kvh, hd=hd, layer=layer),
        grid=(batch, nb),
        in_specs=[
            pl.BlockSpec(memory_space=pltpu.SMEM),
            cur("sq", qb), prev("sk", sb), cur("sk", sb), prev("sv", sb), cur("sv", sb), cur("gb", qb),
            pl.BlockSpec((None, 1, slab), lay3),
            pl.BlockSpec((None, 1, slab), lay3),
            pl.BlockSpec((slab, slab), const2),
            pl.BlockSpec(bias.shape, lambda b, c: (0, 0, 0, 0)),
        ],
        out_specs=[
            pl.BlockSpec((step, nq), lambda b, c: (b * nb + c, 0)),
            pl.BlockSpec((None, window, slab), lambda b, c: (b, 0, 0)),
            pl.BlockSpec((None, window, slab), lambda b, c: (b, 0, 0)),
        ],
        out_shape=[
            jax.ShapeDtypeStruct((batch * seq, nq), BF16),
            jax.ShapeDtypeStruct((batch, window, slab), F32),
            jax.ShapeDtypeStruct((batch, window, slab), F32),
        ],
        compiler_params=pltpu.CompilerParams(
            dimension_semantics=("parallel", "arbitrary"), vmem_limit_bytes=VMEM_LIMIT),
        name="swa_prompt",
    )(sinks, proj, proj, proj, proj, proj, proj, qw, kw, bd, bias)


def _swa_sample_kernel(q_ref, k_ref, v_ref, gb_ref, ck_ref, cv_ref, qw_ref, kw_ref, bd_ref, bias_ref, sink_ref,
                       *rest, kvh, hd, seq):
    o_ref, kout_ref, vout_ref = rest[-3:]
    rows = q_ref.shape[1]
    nseq = rows // seq
    slab = kvh * hd
    spb = slab // LANES
    groups = q_ref.shape[0] // spb
    window = ck_ref.shape[1]
    nkeys = bias_ref.shape[1]
    bd = bd_ref[...]
    kn = _chunk_rms(_wide(k_ref, 0, spb).astype(F32), bd, kw_ref[...]).reshape(nseq, seq, slab)
    vn = _wide(v_ref, 0, spb).astype(F32).reshape(nseq, seq, slab)
    ck = ck_ref[...]
    cv = cv_ref[...]
    kout_ref[...] = jnp.concatenate([ck[:, seq:, :], kn], axis=1)
    vout_ref[...] = jnp.concatenate([cv[:, seq:, :], vn], axis=1)
    zpad = jnp.zeros((nseq, nkeys - window - seq, slab), F32)
    k_all = jnp.concatenate([ck, kn, zpad], axis=1).astype(BF16)
    v_all = jnp.concatenate([cv, vn, zpad], axis=1).astype(BF16)

    lane_head = lax.broadcasted_iota(jnp.int32, (rows, slab), 1) // hd
    pieces = []
    for g in range(groups):
        qn = _chunk_rms(_wide(q_ref, g * spb, spb).astype(F32), bd, qw_ref[...]) * (hd ** -0.5)
        for j in range(kvh):
            pieces.append(jnp.where(lane_head == j, qn, 0.0).reshape(nseq, seq, slab))
    q_all = jnp.concatenate(pieces, axis=1).astype(BF16)
    s = jnp.einsum("bqd,bkd->bqk", q_all, k_all, preferred_element_type=F32) + bias_ref[...][None]
    sink = sink_ref[...][None]
    m = jnp.maximum(jnp.max(s, axis=-1, keepdims=True), sink)
    e = jnp.exp(s - m)
    den = jnp.sum(e, axis=-1, keepdims=True) + jnp.exp(sink - m)
    pv = jnp.einsum("bqk,bkd->bqd", (e * (1.0 / den)).astype(BF16), v_all, preferred_element_type=F32)
    lane_head3 = lax.broadcasted_iota(jnp.int32, (nseq, seq, slab), 2) // hd
    gated = []
    for g in range(groups):
        acc = jnp.zeros((nseq, seq, slab), F32)
        for j in range(kvh):
            p = g * kvh + j
            acc = acc + jnp.where(lane_head3 == j, pv[:, p * seq:(p + 1) * seq, :], 0.0)
        gated.append(acc.reshape(rows, slab) * _sigmoid(_wide(gb_ref, g * spb, spb).astype(F32)))
    for b, blk_out in enumerate(_natural_blocks(gated, hd)):
        o_ref[:, b * LANES:(b + 1) * LANES] = blk_out.astype(BF16)


def _swa_sample(proj, cache_k, cache_v, qw, kw, bd, bias, sink_col, kv_prev, *, layer, row0, batch, seq, kvh, hd,
                nq, col, nseq):
    rows = nseq * seq
    slab = kvh * hd
    depth, _, window, _ = cache_k.shape
    r0 = row0 // rows
    cache_blk = pl.BlockSpec((None, nseq, window, slab), lambda b: (layer, b, 0, 0))
    extra_in = [] if kv_prev is None else list(kv_prev)
    qb, sb = nq // LANES, slab // LANES
    cur = lambda name, n: pl.BlockSpec((n, rows, LANES), lambda b, o=col[name] // n: (o, r0 + b, 0))
    const2 = lambda b: (0, 0)
    lay3 = lambda b: (layer, 0, 0)
    return pl.pallas_call(
        functools.partial(_swa_sample_kernel, kvh=kvh, hd=hd, seq=seq),
        grid=(batch // nseq,),
        in_specs=[
            cur("sq", qb), cur("sk", sb), cur("sv", sb), cur("gb", qb),
            cache_blk, cache_blk,
            pl.BlockSpec((None, 1, slab), lay3),
            pl.BlockSpec((None, 1, slab), lay3),
            pl.BlockSpec((slab, slab), const2),
            pl.BlockSpec(bias.shape, const2),
            pl.BlockSpec((None,) + sink_col.shape[1:], lay3),
        ] + [pl.BlockSpec(memory_space=pl.ANY)] * len(extra_in),
        out_specs=[
            pl.BlockSpec((rows, nq), lambda b: (b, 0)),
            cache_blk, cache_blk,
        ],
        out_shape=[
            jax.ShapeDtypeStruct((batch * seq, nq), BF16),
            jax.ShapeDtypeStruct((depth, batch, window, slab), F32),
            jax.ShapeDtypeStruct((depth, batch, window, slab), F32),
        ],
        input_output_aliases={11: 1, 12: 2} if extra_in else {},
        compiler_params=pltpu.CompilerParams(
            dimension_semantics=("parallel",), vmem_limit_bytes=VMEM_LIMIT),
        name="swa_sample",
    )(proj, proj, proj, proj, cache_k, cache_v, qw, kw, bd, bias, sink_col, *extra_in)


def _dense_kernel(hp_ref, hs_ref, oap_ref, oas_ref, obp_ref, obs_ref, pp_ref, ps_ref, wo_ref, nf_ref,
                  wu_ref, wd_ref, np_ref, wg_ref, wp_ref, outp_ref, outs_ref, *, npt, ff_chunk):
    is_prompt = pl.program_id(0) < npt
    oa = jnp.where(is_prompt, oap_ref[...], oas_ref[...])
    ob = jnp.where(is_prompt, obp_ref[...], obs_ref[...])
    h = jnp.where(is_prompt, hp_ref[...], hs_ref[...])
    h = h + jnp.dot(oa + ob, wo_ref[...], preferred_element_type=F32)
    xn = (_rms(h) * nf_ref[...]).astype(BF16)
    acc = jnp.zeros_like(h)
    for c in range(wu_ref.shape[1] // ff_chunk):
        sl = slice(c * ff_chunk, (c + 1) * ff_chunk)
        hid = jnp.maximum(jnp.dot(xn, wu_ref[:, sl], preferred_element_type=F32), 0.0)
        acc = acc + jnp.dot((hid * hid).astype(BF16), wd_ref[sl, :], preferred_element_type=F32)
    h = h + acc
    xn = (_rms(h) * np_ref[...]).astype(BF16)
    gate = _sigmoid(jnp.dot(xn, wg_ref[...], preferred_element_type=F32))
    p = jnp.where(is_prompt, pp_ref[...], ps_ref[...])
    pe = jnp.dot(p.astype(BF16), wp_ref[...], preferred_element_type=F32)
    out = h + gate * pe

    @pl.when(is_prompt)
    def _():
        outp_ref[...] = out

    @pl.when(jnp.logical_not(is_prompt))
    def _():
        outs_ref[...] = out


def _dense(h_p, h_s, oa_p, oa_s, ob_p, ob_s, p_p, p_s, wo, nf, wu, wd, npl, wg, wp, *, layer, tm):
    tp, d = h_p.shape
    ts = h_s.shape[0]
    npt = tp // tm
    rows_p = lambda m: (jnp.minimum(m, npt - 1), 0)
    rows_s = lambda m: (jnp.maximum(m - npt, 0), 0)
    resident = lambda a: pl.BlockSpec((None,) + a.shape[1:], lambda m: (layer, 0, 0), pipeline_mode=pl.Buffered(1))
    return pl.pallas_call(
        functools.partial(_dense_kernel, npt=npt, ff_chunk=1024),
        grid=((tp + ts) // tm,),
        in_specs=[
            pl.BlockSpec((tm, d), rows_p), pl.BlockSpec((tm, d), rows_s),
            pl.BlockSpec((tm, d), rows_p), pl.BlockSpec((tm, d), rows_s),
            pl.BlockSpec((tm, d), rows_p), pl.BlockSpec((tm, d), rows_s),
            pl.BlockSpec((None, tm, p_p.shape[2]), lambda m: (layer, jnp.minimum(m, npt - 1), 0)),
            pl.BlockSpec((None, tm, p_s.shape[2]), lambda m: (layer, jnp.maximum(m - npt, 0), 0)),
            resident(wo), resident(nf), resident(wu), resident(wd),
            resident(npl), resident(wg), resident(wp),
        ],
        out_specs=[pl.BlockSpec((tm, d), rows_p), pl.BlockSpec((tm, d), rows_s)],
        out_shape=[jax.ShapeDtypeStruct((tp, d), F32), jax.ShapeDtypeStruct((ts, d), F32)],
        compiler_params=pltpu.CompilerParams(
            dimension_semantics=("arbitrary",), vmem_limit_bytes=VMEM_LIMIT),
        name="dense",
    )(h_p, h_s, oa_p, oa_s, ob_p, ob_s, p_p, p_s, wo, nf, wu, wd, npl, wg, wp)


def _seq_mask(seq_len):
    i = np.arange(GDN_CHUNK)
    m = (i[:, None] // seq_len == i[None, :] // seq_len) & (i[None, :] <= i[:, None])
    return m.astype(np.float32)


def kernel(x_prompt, x_sample, cache_conv, state_gdn, cache_swa_k, cache_swa_v, p_prompt, p_sample, norm_mix, w_in, conv_w, a_log, dt_bias, gdn_norm, q_norm, k_norm, attn_sinks, w_out, norm_ffn, w_up, w_down, norm_ple, w_ple_gate, w_ple_proj):
    batch, seq, d = x_prompt.shape
    dbatch, dseq, _ = x_sample.shape
    depth = w_in.shape[0]
    heads, dk, dv = state_gdn.shape[2:]
    window, kvh, hd = cache_swa_k.shape[2:]
    nq = d
    nheads = nq // hd
    groups = nheads // kvh
    slab = kvh * hd
    key_dim = heads * dk
    val_dim = heads * dv
    conv_dim = 2 * key_dim + val_dim
    width = conv_w.shape[1]
    tp, ts = batch * seq, dbatch * dseq
    assert dk == LANES and dv == LANES and val_dim == d and key_dim == d and slab % LANES == 0
    assert seq % window == 0 and window == GDN_CHUNK and GDN_CHUNK % dseq == 0 and dseq >= width - 1
    assert dseq == SUBLANES and ts % GDN_CHUNK == 0

    o_z = conv_dim
    o_b = o_z + val_dim
    o_a = o_b + heads
    o_sq = o_a + heads
    o_sk = o_sq + nq
    o_sv = o_sk + slab
    o_g = o_sv + slab
    col_h = {"q": 0, "k": 1, "v": 2, "z": 0, "ga": 2}
    off = {"sq": val_dim, "gb": val_dim + nq + d, "sk": val_dim + nq + 2 * d, "sv": val_dim + nq + 2 * d + slab}
    col_b = {name: o // LANES for name, o in off.items()}
    cb = conv_dim // LANES

    h_p = x_prompt.reshape(tp, d)
    h_s = x_sample.reshape(ts, d)
    bf = lambda a: a.astype(BF16)
    wo_all, wu_all, wd_all = bf(w_out), bf(w_up), bf(w_down)
    wg_all, wp_all = bf(w_ple_gate), bf(w_ple_proj)
    nf_all, np_all = norm_ffn[:, None, :], norm_ple[:, None, :]
    p_p = p_prompt.reshape(depth, tp, -1)
    p_s = p_sample.reshape(depth, ts, -1)

    masks = jnp.asarray(np.stack([_seq_mask(GDN_CHUNK), _seq_mask(dseq)]), BF16)
    mincl_p = jnp.asarray(_seq_mask(GDN_CHUNK), F32)
    mincl_s = jnp.asarray(_seq_mask(dseq), F32)
    bd = jnp.asarray(np.kron(np.eye(kvh), np.full((hd, hd), 1.0 / hd)), BF16)

    head_of = np.array([(p % kvh) * groups + p // kvh for p in range(nheads)])
    slopes = jnp.exp2(-8.0 * (jnp.asarray(head_of, F32) + 1.0) / nheads)
    qi = np.arange(window)[:, None]
    kj = np.arange(2 * window)[None, :]
    dist = window + qi - kj
    ok = (dist >= 0) & (dist <= window)
    ok_first = ok & (kj >= window)
    dist_f = jnp.asarray(dist, F32)
    bias_p = jnp.stack([
        jnp.where(jnp.asarray(okv)[None], -slopes[:, None, None] * dist_f[None], -jnp.inf) for okv in (ok_first, ok)])
    nkeys = -(-(window + dseq) // 16) * 16
    ti = np.tile(np.arange(dseq), nheads)[:, None]
    sj = np.arange(nkeys)[None, :]
    dist_s = ti + window - sj
    ok_s = (dist_s >= 0) & (dist_s <= window) & (sj < window + dseq)
    slopes_rows = jnp.repeat(slopes, dseq)[:, None]
    bias_s = jnp.where(jnp.asarray(ok_s), -slopes_rows * jnp.asarray(dist_s, F32), -jnp.inf)

    hist_all = jnp.pad(cache_conv, ((0, 0), (0, 0), (dseq - cache_conv.shape[2], 0), (0, 0)))
    hist_all = hist_all.reshape(depth, ts, conv_dim)
    ck_all = cache_swa_k.reshape(depth, dbatch, window, slab)
    cv_all = cache_swa_v.reshape(depth, dbatch, window, slab)
    cw_all = jnp.transpose(conv_w.reshape(depth, width, cb, LANES), (0, 2, 1, 3))

    avec_all = (jnp.zeros((depth, 2, LANES), F32).at[:, 0, heads:2 * heads].set(a_log)
                .at[:, 1, heads:2 * heads].set(dt_bias))
    nm_all, gn_all = norm_mix[:, None, :], gdn_norm[:, None, :]
    sinks_all = attn_sinks[:, head_of]
    sink_col_all = jnp.repeat(sinks_all, dseq, axis=1)[:, :, None]
    qw_all = jnp.tile(q_norm, (1, kvh))[:, None, :]
    kw_all = jnp.tile(k_norm, (1, kvh))[:, None, :]
    w_t = jnp.swapaxes(w_in, 1, 2)
    wba_all = jnp.pad(w_t[:, o_b:o_sq], ((0, 0), (0, LANES - 2 * heads), (0, 0))).astype(BF16)

    def regroup(x):
        return jnp.swapaxes(x.reshape(depth, kvh, groups, hd, d), 1, 2).reshape(x.shape)

    w_main_all = jnp.concatenate([
        w_t[:, :o_b], regroup(w_t[:, o_sq:o_sk]), w_t[:, o_g:o_g + d], regroup(w_t[:, o_g + d:]),
        w_t[:, o_sk:o_g]], axis=1).astype(BF16)

    outs = {n: [] for n in ("conv_p", "gdn_p", "k_p", "v_p", "conv_s")}
    gdn_s = None
    kv_s = None
    for i in range(depth):
        proj32, proj16, bgc = _inproj(h_p, h_s, nm_all, w_main_all, wba_all, avec_all, masks, layer=i, heads=heads,
                                      tm=1024, tn=1536, n_f32=conv_dim)

        oa_p, gdn_p, tails = _gdn_prompt(proj32, proj16, bgc, cw_all, gn_all, mincl_p, layer=i, batch=batch, seq=seq,
                                         heads=heads, dk=dk, col=col_h, rows=512, heads_per_iter=4)
        oa_s, gdn_s, raw_s = _gdn_sample(proj32, proj16, bgc, hist_all, state_gdn, cw_all, gn_all, mincl_s, gdn_s, layer=i,
                                  row0=tp, batch=dbatch, seq=dseq, heads=heads, dk=dk, col=col_h, heads_per_step=4)

        ob_p, k_p, v_p = _swa_prompt(proj16, sinks_all, qw_all, kw_all, bd, bias_p, layer=i, batch=batch, seq=seq,
                                     window=window, kvh=kvh, hd=hd, nq=nq, col=col_b, nsub=2)
        ob_s, *kv_s = _swa_sample(proj16, ck_all, cv_all, qw_all, kw_all, bd, bias_s, sink_col_all, kv_s,
                                  layer=i, row0=tp, batch=dbatch, seq=dseq, kvh=kvh, hd=hd, nq=nq, col=col_b, nseq=8)

        h_p, h_s = _dense(h_p, h_s, oa_p, oa_s, ob_p, ob_s, p_p, p_s, wo_all, nf_all, wu_all, wd_all, np_all,
                          wg_all, wp_all, layer=i, tm=512)

        outs["conv_p"].append(tails)
        outs["conv_s"].append(raw_s)
        outs["gdn_p"].append(gdn_p)
        outs["k_p"].append(k_p.reshape(batch, window, kvh, hd))
        outs["v_p"].append(v_p.reshape(batch, window, kvh, hd))

    st = lambda n: jnp.stack(outs[n])
    conv_p = st("conv_p")[:, :, SUBLANES - (width - 1):, :]
    conv_s = st("conv_s").reshape(depth, 3, dbatch, dseq, key_dim)[:, :, :, dseq - (width - 1):, :]
    conv_s = jnp.transpose(conv_s, (0, 2, 3, 1, 4)).reshape(depth, dbatch, width - 1, conv_dim)
    return (h_p.reshape(batch, seq, d), h_s.reshape(dbatch, dseq, d),
            conv_p, st("gdn_p"), st("k_p"), st("v_p"), conv_s, gdn_s,
            kv_s[0].reshape(depth, dbatch, window, kvh, hd), kv_s[1].reshape(depth, dbatch, window, kvh, hd))
```

```python
import functools

import numpy as np
import jax
import jax.numpy as jnp
from jax import lax
from jax.experimental import pallas as pl
from jax.experimental.pallas import tpu as pltpu

F32 = jnp.float32
BF16 = jnp.bfloat16
EPS = 1e-6
LANES = 128
SUBLANES = 8
MXU_N = 256
GDN_CHUNK = 128
VMEM_LIMIT = 56 * 1024 * 1024


def _sigmoid(x):
    return 0.5 * jnp.tanh(0.5 * x) + 0.5


def _silu(x):
    hx = 0.5 * x
    return hx + hx * jnp.tanh(hx)


def _dot(a, b):
    return jnp.dot(a.astype(BF16), b.astype(BF16), preferred_element_type=F32)


def _dot_nt(a, b):
    return lax.dot_general(a.astype(BF16), b.astype(BF16), (((1,), (1,)), ((), ())), preferred_element_type=F32)


def _dot_tn(a, b):
    return lax.dot_general(a.astype(BF16), b.astype(BF16), (((0,), (0,)), ((), ())), preferred_element_type=F32)


def _rms(x):
    return x * lax.rsqrt(jnp.mean(x * x, axis=-1, keepdims=True) + EPS)


def _l2(x):
    return x * lax.rsqrt(jnp.sum(x * x, axis=-1, keepdims=True) + EPS)


def _inproj_kernel(xp_ref, xs_ref, nw_ref, w_ref, wba_ref, avec_ref, mask_ref, proj32_ref, proj16_ref, bgc_ref,
                   xn_ref, *, heads, npt, n32):
    n = pl.program_id(1)

    @pl.when(n == 0)
    def _():
        x = jnp.where(pl.program_id(0) < npt, xp_ref[...], xs_ref[...])
        xn = (_rms(x) * nw_ref[...]).astype(BF16)
        xn_ref[...] = xn
        ba = _dot_nt(xn, wba_ref[...])
        beta = _sigmoid(ba)
        xs = ba + avec_ref[1:2, :]
        softplus = jnp.maximum(xs, 0.0) + jnp.log1p(jnp.exp(-jnp.abs(xs)))
        g = -jnp.exp(avec_ref[0:1, :]) * softplus
        g1 = g.astype(BF16)
        r1 = g - g1.astype(F32)
        g2 = r1.astype(BF16)
        g3 = (r1 - g2.astype(F32)).astype(BF16)
        m = mask_ref[...]
        rows = g.shape[0]
        lane = lax.broadcasted_iota(jnp.int32, (GDN_CHUNK, LANES), 1)
        for r in range(rows // GDN_CHUNK):
            sl = slice(r * GDN_CHUNK, (r + 1) * GDN_CHUNK)
            gc = (jnp.dot(m, g1[sl], preferred_element_type=F32)
                  + jnp.dot(m, g2[sl], preferred_element_type=F32)
                  + jnp.dot(m, g3[sl], preferred_element_type=F32))
            bgc_ref[sl, :] = jnp.where(lane < heads, beta[sl], gc)

    def project(out_ref):
        xn = xn_ref[...]
        for j in range(w_ref.shape[0] // MXU_N):
            r = _dot_nt(xn, w_ref[j * MXU_N:(j + 1) * MXU_N, :])
            for i in range(MXU_N // LANES):
                out_ref[j * (MXU_N // LANES) + i] = r[:, i * LANES:(i + 1) * LANES].astype(out_ref.dtype)

    @pl.when(n < n32)
    def _():
        project(proj32_ref)

    @pl.when(n >= n32)
    def _():
        project(proj16_ref)


def _inproj(h_p, h_s, nw, w, wba, avec, masks, *, layer, heads, tm, tn, n_f32):
    d = h_p.shape[1]
    t_all = h_p.shape[0] + h_s.shape[0]
    n_out = w.shape[1]
    npt = h_p.shape[0] // tm
    n32 = n_f32 // tn
    assert n32 * tn == n_f32
    return pl.pallas_call(
        functools.partial(_inproj_kernel, heads=heads, npt=npt, n32=n32),
        grid=(t_all // tm, n_out // tn),
        in_specs=[
            pl.BlockSpec((tm, d), lambda m, n: (jnp.minimum(m, npt - 1), 0)),
            pl.BlockSpec((tm, d), lambda m, n: (jnp.maximum(m - npt, 0), 0)),
            pl.BlockSpec((None, 1, d), lambda m, n: (layer, 0, 0)),
            pl.BlockSpec((None, tn, d), lambda m, n: (layer, n, 0)),
            pl.BlockSpec((None, LANES, d), lambda m, n: (layer, 0, 0)),
            pl.BlockSpec((None, 2, LANES), lambda m, n: (layer, 0, 0)),
            pl.BlockSpec((None, GDN_CHUNK, GDN_CHUNK), lambda m, n: (jnp.where(m >= npt, 1, 0), 0, 0)),
        ],
        out_specs=[
            pl.BlockSpec((tn // LANES, tm, LANES), lambda m, n: (jnp.minimum(n, n32 - 1), m, 0)),
            pl.BlockSpec((tn // LANES, tm, LANES), lambda m, n: (jnp.maximum(n - n32, 0), m, 0)),
            pl.BlockSpec((tm, LANES), lambda m, n: (m, 0)),
        ],
        out_shape=[
            jax.ShapeDtypeStruct((n_f32 // LANES, t_all, LANES), F32),
            jax.ShapeDtypeStruct(((n_out - n_f32) // LANES, t_all, LANES), BF16),
            jax.ShapeDtypeStruct((t_all, LANES), F32),
        ],
        scratch_shapes=[pltpu.VMEM((tm, d), BF16)],
        compiler_params=pltpu.CompilerParams(
            dimension_semantics=("parallel", "arbitrary"), vmem_limit_bytes=VMEM_LIMIT),
        name="inproj",
    )(h_p, h_s, nw, w, wba, avec, masks)


def _gdn_chunks_local(qs, ks, vs, betas, gcs, mincl, seq_len):
    n = len(qs)
    c = qs[0].shape[0]
    dv = vs[0].shape[1]
    row = lax.broadcasted_iota(jnp.int32, (c, c), 0)
    col = lax.broadcasted_iota(jnp.int32, (c, c), 1)
    decs, egs = [], []
    for gc in gcs:
        gcb = jnp.broadcast_to(gc, (c, c))
        decs.append(jnp.exp(jnp.where(mincl > 0, gcb - gcb.T, -jnp.inf)))
        egs.append(jnp.exp(gc))
    kbs = [ks[i] * betas[i] for i in range(n)]
    kks = [_dot_nt(jnp.concatenate([kbs[i], qs[i]], axis=0), ks[i]) for i in range(n)]
    a_s = [jnp.where(row == col, 0.0, kks[i][:c] * decs[i]) for i in range(n)]
    qks = [kks[i][c:] * decs[i] for i in range(n)]
    same = (row >> 1) == (col >> 1)
    eye = jnp.where(row == col, 1.0, 0.0)
    ts = [eye - jnp.where(same, a, 0.0) for a in a_s]
    na16 = [(-a).astype(BF16) for a in a_s]
    for lvl in range(1, seq_len.bit_length() - 1):
        wider = (row >> (lvl + 1)) == (col >> (lvl + 1))
        sel = wider & jnp.logical_not(same)
        t16 = [t.astype(BF16) for t in ts]
        mids = [jnp.dot(na16[i], t16[i], preferred_element_type=F32) for i in range(n)]
        ts = [jnp.where(sel, jnp.dot(t16[i], mids[i].astype(BF16), preferred_element_type=F32), ts[i])
              for i in range(n)]
        same = wider
    sols = [_dot(ts[i], jnp.concatenate([vs[i] * betas[i], kbs[i] * egs[i]], axis=1)) for i in range(n)]
    return [s[:, :dv] for s in sols], [s[:, dv:] for s in sols], qks, egs


def _conv_taps(u, ru_fn, cw):
    width = cw.shape[0]
    out = u * cw[width - 1:width, :]
    for j in range(1, width):
        out = out + ru_fn(j) * cw[width - 1 - j:width - j, :]
    return _silu(out)


def _gdn_finish(o, z, ga, gn):
    return (_rms(o) * gn * _silu(z.astype(F32)) * _sigmoid(ga.astype(F32))).astype(BF16)


def _head_scalars(bgc, h, heads):
    lane = lax.broadcasted_iota(jnp.int32, bgc.shape, 1)
    beta = jnp.sum(jnp.where(lane == h, bgc, 0.0), axis=-1, keepdims=True)
    gc = jnp.sum(jnp.where(lane == h + heads, bgc, 0.0), axis=-1, keepdims=True)
    return beta, gc


def _gdn_prompt_kernel(q_ref, k_ref, v_ref, z_ref, ga_ref, bgc_ref, cw_ref, gn_ref, mincl_ref,
                       o_ref, sfin_ref, tails_ref,
                       s_ref, tail_ref, u_s, wqe_s, qk_s, kd_s, el_s, *, heads_per_iter):
    c_id = pl.program_id(1)
    heads, rows, dk = q_ref.shape
    nchunk = rows // GDN_CHUNK
    cc = GDN_CHUNK

    @pl.when(c_id == 0)
    def _():
        s_ref[...] = jnp.zeros_like(s_ref)
        tail_ref[...] = jnp.zeros_like(tail_ref)

    row8 = lax.broadcasted_iota(jnp.int32, (SUBLANES, dk), 0)
    bgc = bgc_ref[...]
    mincl = mincl_ref[...]

    def conv(kind, h, u_ref):
        u = u_ref[h]
        tail = tail_ref[kind * heads + h]

        def delayed(j):
            head = jnp.where(row8 < j, pltpu.roll(tail, j, 0), pltpu.roll(u[:SUBLANES], j, 0))
            return jnp.concatenate([head, u_ref[h, pl.ds(SUBLANES - j, rows - SUBLANES), :]], axis=0)

        out = _conv_taps(u, delayed, cw_ref[kind * heads + h])
        tail_ref[kind * heads + h] = u[rows - SUBLANES:]
        return out

    def local_body(it, carry):
        hs, qs, ks, vs, betas, gcs = [], [], [], [], [], []
        for hh in range(heads_per_iter):
            h = it * heads_per_iter + hh
            q = _l2(conv(0, h, q_ref)) * (dk ** -0.5)
            k = _l2(conv(1, h, k_ref))
            v = conv(2, h, v_ref)
            beta, gc = _head_scalars(bgc, h, heads)
            for ci in range(nchunk):
                sl = slice(ci * cc, (ci + 1) * cc)
                hs.append((h, ci))
                qs.append(q[sl]); ks.append(k[sl]); vs.append(v[sl]); betas.append(beta[sl]); gcs.append(gc[sl])
        us, ws, qks, egs = _gdn_chunks_local(qs, ks, vs, betas, gcs, mincl, cc)
        for i, (h, ci) in enumerate(hs):
            g_last = gcs[i][cc - 1:cc, :]
            u_s[h, ci] = us[i]
            wqe_s[h, ci, :cc] = ws[i].astype(BF16)
            wqe_s[h, ci, cc:] = (qs[i] * egs[i]).astype(BF16)
            qk_s[h, ci] = qks[i].astype(BF16)
            kd_s[h, ci] = (ks[i] * jnp.exp(g_last - gcs[i])).astype(BF16)
            el_s[h, ci] = jnp.broadcast_to(jnp.exp(g_last), (SUBLANES, dk))
        return carry

    lax.fori_loop(0, heads // heads_per_iter, local_body, 0)

    states = [s_ref[h] for h in range(heads)]
    gn = gn_ref[...]
    for ci in range(nchunk):
        sl = slice(ci * cc, (ci + 1) * cc)
        wqs = [jnp.dot(wqe_s[h, ci], states[h].astype(BF16), preferred_element_type=F32) for h in range(heads)]
        v_news = [u_s[h, ci] - wqs[h][:cc] for h in range(heads)]
        vn16 = [v.astype(BF16) for v in v_news]
        outs = [wqs[h][cc:] + jnp.dot(qk_s[h, ci], vn16[h], preferred_element_type=F32) for h in range(heads)]
        states = [states[h] * el_s[h, ci][0:1, :]
                  + lax.dot_general(kd_s[h, ci], vn16[h], (((0,), (0,)), ((), ())), preferred_element_type=F32)
                  for h in range(heads)]
        for h in range(heads):
            o_ref[sl, h * dk:(h + 1) * dk] = _gdn_finish(outs[h], z_ref[h, sl, :], ga_ref[h, sl, :], gn)
    for h in range(heads):
        s_ref[h] = states[h]
        sfin_ref[h] = states[h]
    for idx in range(3 * heads):
        tails_ref[:, idx * dk:(idx + 1) * dk] = tail_ref[idx]


def _gdn_prompt(proj32, proj16, bgc, cw, gn, mincl, *, layer, batch, seq, heads, dk, col, rows, heads_per_iter):
    ncb = seq // rows
    nchunk = rows // GDN_CHUNK
    blk = lambda name: pl.BlockSpec((heads, rows, dk), lambda b, c, o=col[name]: (o, b * ncb + c, 0))
    return pl.pallas_call(
        functools.partial(_gdn_prompt_kernel, heads_per_iter=heads_per_iter),
        grid=(batch, ncb),
        in_specs=[
            blk("q"), blk("k"), blk("v"), blk("z"), blk("ga"),
            pl.BlockSpec((rows, LANES), lambda b, c: (b * ncb + c, 0)),
            pl.BlockSpec((None,) + cw.shape[1:], lambda b, c: (layer, 0, 0, 0)),
            pl.BlockSpec((None, 1, dk), lambda b, c: (layer, 0, 0)),
            pl.BlockSpec((GDN_CHUNK, GDN_CHUNK), lambda b, c: (0, 0)),
        ],
        out_specs=[
            pl.BlockSpec((rows, heads * dk), lambda b, c: (b * ncb + c, 0)),
            pl.BlockSpec((None, heads, dk, dk), lambda b, c: (b, 0, 0, 0)),
            pl.BlockSpec((None, SUBLANES, 3 * heads * dk), lambda b, c: (b, 0, 0)),
        ],
        out_shape=[
            jax.ShapeDtypeStruct((batch * seq, heads * dk), BF16),
            jax.ShapeDtypeStruct((batch, heads, dk, dk), F32),
            jax.ShapeDtypeStruct((batch, SUBLANES, 3 * heads * dk), F32),
        ],
        scratch_shapes=[
            pltpu.VMEM((heads, dk, dk), F32),
            pltpu.VMEM((3 * heads, SUBLANES, dk), F32),
            pltpu.VMEM((heads, nchunk, GDN_CHUNK, dk), F32),
            pltpu.VMEM((heads, nchunk, 2 * GDN_CHUNK, dk), BF16),
            pltpu.VMEM((heads, nchunk, GDN_CHUNK, GDN_CHUNK), BF16),
            pltpu.VMEM((heads, nchunk, GDN_CHUNK, dk), BF16),
            pltpu.VMEM((heads, nchunk, SUBLANES, dk), F32),
        ],
        compiler_params=pltpu.CompilerParams(
            dimension_semantics=("parallel", "arbitrary"), vmem_limit_bytes=VMEM_LIMIT),
        name="gdn_prompt",
    )(proj32, proj32, proj32, proj16, proj16, bgc, cw, gn, mincl)


def _gdn_sample_kernel(q_ref, k_ref, v_ref, z_ref, ga_ref, bgc_ref, hq_ref, hk_ref, hv_ref,
                       cw_ref, gn_ref, mincl_ref, s0_ref, *rest, heads, seq):
    o_ref, sout_ref, raw_ref = rest[-3:]
    hpi, rows, dk = q_ref.shape
    nseq = rows // seq
    t_idx = lax.broadcasted_iota(jnp.int32, (nseq, seq, dk), 1)
    bgc = bgc_ref[...]

    def conv(kind, hh, h, u_ref, hist_ref):
        u = u_ref[hh]
        raw_ref[kind, :, hh * dk:(hh + 1) * dk] = u
        u3 = u.reshape(nseq, seq, dk)
        hist3 = hist_ref[:, hh * dk:(hh + 1) * dk].reshape(nseq, seq, dk)

        def delayed(j):
            xj = jnp.where(t_idx < j, pltpu.roll(hist3, j, 1), pltpu.roll(u3, j, 1))
            return xj.reshape(rows, dk)

        return _conv_taps(u, delayed, cw_ref[kind * heads + h])

    qs, ks, vs, betas, gcs = [], [], [], [], []
    for hh in range(hpi):
        h = pl.program_id(1) * hpi + hh
        qs.append(_l2(conv(0, hh, h, q_ref, hq_ref)) * (dk ** -0.5))
        ks.append(_l2(conv(1, hh, h, k_ref, hk_ref)))
        vs.append(conv(2, hh, h, v_ref, hv_ref))
        beta, gc = _head_scalars(bgc, h, heads)
        betas.append(beta)
        gcs.append(gc)
    us, ws, qks, egs = _gdn_chunks_local(qs, ks, vs, betas, gcs, mincl_ref[...], seq)

    pad = jnp.zeros((seq, dk), F32)
    pairs = [(hh, si) for hh in range(hpi) for si in range(nseq)]
    sl = lambda si: slice(si * seq, (si + 1) * seq)
    s0 = {p: s0_ref[p[1], p[0]] for p in pairs}
    qes = [qs[hh] * egs[hh] for hh in range(hpi)]
    wqs = {(hh, si): _dot(jnp.concatenate([ws[hh][sl(si)], qes[hh][sl(si)]], axis=0), s0[(hh, si)])
           for hh, si in pairs}
    v_news = {(hh, si): us[hh][sl(si)] - wqs[(hh, si)][:seq] for hh, si in pairs}
    for hh, si in pairs:
        g_last = gcs[hh][(si + 1) * seq - 1:(si + 1) * seq, :]
        kd = ks[hh][sl(si)] * jnp.exp(g_last - gcs[hh][sl(si)])
        sout_ref[si, hh] = s0[(hh, si)] * jnp.exp(g_last) + _dot_tn(
            jnp.concatenate([kd, pad], axis=0), jnp.concatenate([v_news[(hh, si)], pad], axis=0))
    gn = gn_ref[...]
    for hh in range(hpi):
        o = (jnp.concatenate([wqs[(hh, si)][seq:] for si in range(nseq)], axis=0)
             + _dot(qks[hh], jnp.concatenate([v_news[(hh, si)] for si in range(nseq)], axis=0)))
        o_ref[:, hh * dk:(hh + 1) * dk] = _gdn_finish(o, z_ref[hh], ga_ref[hh], gn)


def _gdn_sample(proj32, proj16, bgc, hist, s0, cw, gn, mincl, s_prev, *, layer, row0, batch, seq, heads, dk, col,
                heads_per_step):
    rows = GDN_CHUNK
    nseq = rows // seq
    r0 = row0 // rows
    depth = s0.shape[0]
    hpi = heads_per_step
    hgroups = heads // hpi
    blk = lambda name: pl.BlockSpec((hpi, rows, dk), lambda b, h, o=col[name]: (o * hgroups + h, r0 + b, 0))
    hblk = lambda name: pl.BlockSpec((None, rows, hpi * dk), lambda b, h, o=col[name]: (layer, b, o * hgroups + h))
    state_blk = pl.BlockSpec((None, nseq, hpi, dk, dk), lambda b, h: (layer, b, h, 0, 0))
    extra_in = [] if s_prev is None else [s_prev]
    return pl.pallas_call(
        functools.partial(_gdn_sample_kernel, heads=heads, seq=seq),
        grid=(batch * seq // rows, hgroups),
        in_specs=[
            blk("q"), blk("k"), blk("v"), blk("z"), blk("ga"),
            pl.BlockSpec((rows, LANES), lambda b, h: (r0 + b, 0)),
            hblk("q"), hblk("k"), hblk("v"),
            pl.BlockSpec((None,) + cw.shape[1:], lambda b, h: (layer, 0, 0, 0)),
            pl.BlockSpec((None, 1, dk), lambda b, h: (layer, 0, 0)),
            pl.BlockSpec((GDN_CHUNK, GDN_CHUNK), lambda b, h: (0, 0)),
            state_blk,
        ] + [pl.BlockSpec(memory_space=pl.ANY)] * len(extra_in),
        out_specs=[
            pl.BlockSpec((rows, hpi * dk), lambda b, h: (b, h)),
            state_blk,
            pl.BlockSpec((3, rows, hpi * dk), lambda b, h: (0, b, h)),
        ],
        out_shape=[
            jax.ShapeDtypeStruct((batch * seq, heads * dk), BF16),
            jax.ShapeDtypeStruct((depth, batch, heads, dk, dk), F32),
            jax.ShapeDtypeStruct((3, batch * seq, heads * dk), F32),
        ],
        input_output_aliases={13: 1} if extra_in else {},
        compiler_params=pltpu.CompilerParams(
            dimension_semantics=("parallel", "parallel"), vmem_limit_bytes=VMEM_LIMIT),
        name="gdn_sample",
    )(proj32, proj32, proj32, proj16, proj16, bgc, hist, hist, hist, cw, gn, mincl, s0, *extra_in)


def _chunk_rms(x, bd, w):
    ms = _dot(x * x, bd)
    return x * lax.rsqrt(ms + EPS) * w


def _wide(ref, first, n):
    return jnp.concatenate([ref[first + i] for i in range(n)], axis=-1)


def _natural_blocks(slabs, hd):
    groups = len(slabs)
    per_block = LANES // hd
    lane = lax.broadcasted_iota(jnp.int32, (slabs[0].shape[0], LANES), 1)
    blocks = []
    for b in range(slabs[0].shape[1] * groups // LANES):
        blk = None
        for i in range(per_block):
            j, g = divmod(b * per_block + i, groups)
            first = (hd * j) // LANES * LANES
            piece = slabs[g][:, first:first + LANES]
            shift = (i * hd - hd * j) % LANES
            if shift:
                piece = pltpu.roll(piece, shift, 1)
            blk = piece if blk is None else jnp.where(lane < i * hd, blk, piece)
        blocks.append(blk)
    return blocks


def _swa_prompt_kernel(sink_ref, q_ref, kp_ref, kc_ref, vp_ref, vc_ref, gb_ref, qw_ref, kw_ref, bd_ref, bias_ref,
                       o_ref, kout_ref, vout_ref, *, kvh, hd, layer):
    blk = kp_ref.shape[1]
    nsub = q_ref.shape[1] // blk
    slab = kvh * hd
    spb = slab // LANES
    groups = q_ref.shape[0] // spb
    bd = bd_ref[...]
    kn = _chunk_rms(_wide(kc_ref, 0, spb).astype(F32), bd, kw_ref[...])
    v_cur = _wide(vc_ref, 0, spb)
    kout_ref[...] = kn[(nsub - 1) * blk:]
    vout_ref[...] = v_cur[(nsub - 1) * blk:].astype(F32)
    k_all = jnp.concatenate([_chunk_rms(_wide(kp_ref, 0, spb).astype(F32), bd, kw_ref[...]), kn],
                            axis=0).astype(BF16)
    v_all = jnp.concatenate([_wide(vp_ref, 0, spb), v_cur], axis=0)
    lane_head = lax.broadcasted_iota(jnp.int32, (blk, slab), 1) // hd
    first_variant = jnp.minimum(pl.program_id(1), 1)

    for sub in range(nsub):
        rows = slice(sub * blk, (sub + 1) * blk)
        kband = k_all[sub * blk:(sub + 2) * blk]
        vband = v_all[sub * blk:(sub + 2) * blk]
        variant = first_variant if sub == 0 else 1

        def scores(g):
            qn = _chunk_rms(_wide(q_ref, g * spb, spb)[rows].astype(F32), bd, qw_ref[...]) * (hd ** -0.5)
            qs = jnp.concatenate([jnp.where(lane_head == j, qn, 0.0) for j in range(kvh)], axis=0).astype(BF16)
            return lax.dot_general(qs, kband, (((1,), (1,)), ((), ())), preferred_element_type=F32)

        s_groups = [scores(g) for g in range(groups)]
        gated = []
        for g in range(groups):
            s_all = s_groups[g]
            probs = []
            for j in range(kvh):
                p = g * kvh + j
                s = s_all[j * blk:(j + 1) * blk] + bias_ref[variant, p]
                sink = sink_ref[layer, p]
                m = jnp.maximum(jnp.max(s, axis=-1, keepdims=True), sink)
                e = jnp.exp(s - m)
                den = jnp.sum(e, axis=-1, keepdims=True) + jnp.exp(sink - m)
                probs.append((e * (1.0 / den)).astype(BF16))
            pv = jnp.dot(jnp.concatenate(probs, axis=0), vband, preferred_element_type=F32)
            acc = pv[(kvh - 1) * blk:]
            for j in range(kvh - 2, -1, -1):
                acc = jnp.where(lane_head == j, pv[j * blk:(j + 1) * blk], acc)
            gated.append(acc * _sigmoid(_wide(gb_ref, g * spb, spb)[rows].astype(F32)))
        for b, blk_out in enumerate(_natural_blocks(gated, hd)):
            o_ref[rows, b * LANES:(b + 1) * LANES] = blk_out.astype(BF16)


def _swa_prompt(proj, sinks, qw, kw, bd, bias, *, layer, batch, seq, window, kvh, hd, nq, col, nsub):
    step = nsub * window
    nb = seq // step
    slab = kvh * hd
    qb, sb = nq // LANES, slab // LANES
    cur = lambda name, n: pl.BlockSpec((n, step, LANES), lambda b, c, o=col[name] // n: (o, b * nb + c, 0))
    prev = lambda name, n: pl.BlockSpec(
        (n, window, LANES), lambda b, c, o=col[name] // n: (o, (b * nb + c) * nsub - jnp.minimum(c, 1), 0))
    const2 = lambda b, c: (0, 0)
    lay3 = lambda b, c: (layer, 0, 0)
    return pl.pallas_call(
        functools.partial(_swa_prompt_kernel, kvh=kvh, hd=hd, layer=layer),
        grid=(batch, nb),
        in_specs=[
            pl.BlockSpec(memory_space=pltpu.SMEM),
            cur("sq", qb), prev("sk", sb), cur("sk", sb), prev("sv", sb), cur("sv", sb), cur("gb", qb),
            pl.BlockSpec((None, 1, slab), lay3),
            pl.BlockSpec((None, 1, slab), lay3),
            pl.BlockSpec((slab, slab), const2),
            pl.BlockSpec(bias.shape, lambda b, c: (0, 0, 0, 0)),
        ],
        out_specs=[
            pl.BlockSpec((step, nq), lambda b, c: (b * nb + c, 0)),
            pl.BlockSpec((None, window, slab), lambda b, c: (b, 0, 0)),
            pl.BlockSpec((None, window, slab), lambda b, c: (b, 0, 0)),
        ],
        out_shape=[
            jax.ShapeDtypeStruct((batch * seq, nq), BF16),
            jax.ShapeDtypeStruct((batch, window, slab), F32),
            jax.ShapeDtypeStruct((batch, window, slab), F32),
        ],
        compiler_params=pltpu.CompilerParams(
            dimension_semantics=("parallel", "arbitrary"), vmem_limit_bytes=VMEM_LIMIT),
        name="swa_prompt",
    )(sinks, proj, proj, proj, proj, proj, proj, qw, kw, bd, bias)


def _swa_sample_kernel(q_ref, k_ref, v_ref, gb_ref, ck_ref, cv_ref, qw_ref, kw_ref, bd_ref, bias_ref, sink_ref,
                       *rest, kvh, hd, seq):
    o_ref, kout_ref, vout_ref = rest[-3:]
    rows = q_ref.shape[1]
    nseq = rows // seq
    slab = kvh * hd
    spb = slab // LANES
    groups = q_ref.shape[0] // spb
    window = ck_ref.shape[1]
    nkeys = bias_ref.shape[1]
    bd = bd_ref[...]
    kn = _chunk_rms(_wide(k_ref, 0, spb).astype(F32), bd, kw_ref[...]).reshape(nseq, seq, slab)
    vn = _wide(v_ref, 0, spb).astype(F32).reshape(nseq, seq, slab)
    ck = ck_ref[...]
    cv = cv_ref[...]
    kout_ref[...] = jnp.concatenate([ck[:, seq:, :], kn], axis=1)
    vout_ref[...] = jnp.concatenate([cv[:, seq:, :], vn], axis=1)
    zpad = jnp.zeros((nseq, nkeys - window - seq, slab), F32)
    k_all = jnp.concatenate([ck, kn, zpad], axis=1).astype(BF16)
    v_all = jnp.concatenate([cv, vn, zpad], axis=1).astype(BF16)

    lane_head = lax.broadcasted_iota(jnp.int32, (rows, slab), 1) // hd
    pieces = []
    for g in range(groups):
        qn = _chunk_rms(_wide(q_ref, g * spb, spb).astype(F32), bd, qw_ref[...]) * (hd ** -0.5)
        for j in range(kvh):
            pieces.append(jnp.where(lane_head == j, qn, 0.0).reshape(nseq, seq, slab))
    q_all = jnp.concatenate(pieces, axis=1).astype(BF16)
    s = jnp.einsum("bqd,bkd->bqk", q_all, k_all, preferred_element_type=F32) + bias_ref[...][None]
    sink = sink_ref[...][None]
    m = jnp.maximum(jnp.max(s, axis=-1, keepdims=True), sink)
    e = jnp.exp(s - m)
    den = jnp.sum(e, axis=-1, keepdims=True) + jnp.exp(sink - m)
    pv = jnp.einsum("bqk,bkd->bqd", (e * (1.0 / den)).astype(BF16), v_all, preferred_element_type=F32)
    lane_head3 = lax.broadcasted_iota(jnp.int32, (nseq, seq, slab), 2) // hd
    gated = []
    for g in range(groups):
        acc = jnp.zeros((nseq, seq, slab), F32)
        for j in range(kvh):
            p = g * kvh + j
            acc = acc + jnp.where(lane_head3 == j, pv[:, p * seq:(p + 1) * seq, :], 0.0)
        gated.append(acc.reshape(rows, slab) * _sigmoid(_wide(gb_ref, g * spb, spb).astype(F32)))
    for b, blk_out in enumerate(_natural_blocks(gated, hd)):
        o_ref[:, b * LANES:(b + 1) * LANES] = blk_out.astype(BF16)


def _swa_sample(proj, cache_k, cache_v, qw, kw, bd, bias, sink_col, kv_prev, *, layer, row0, batch, seq, kvh, hd,
                nq, col, nseq):
    rows = nseq * seq
    slab = kvh * hd
    depth, _, window, _ = cache_k.shape
    r0 = row0 // rows
    cache_blk = pl.BlockSpec((None, nseq, window, slab), lambda b: (layer, b, 0, 0))
    extra_in = [] if kv_prev is None else list(kv_prev)
    qb, sb = nq // LANES, slab // LANES
    cur = lambda name, n: pl.BlockSpec((n, rows, LANES), lambda b, o=col[name] // n: (o, r0 + b, 0))
    const2 = lambda b: (0, 0)
    lay3 = lambda b: (layer, 0, 0)
    return pl.pallas_call(
        functools.partial(_swa_sample_kernel, kvh=kvh, hd=hd, seq=seq),
        grid=(batch // nseq,),
        in_specs=[
            cur("sq", qb), cur("sk", sb), cur("sv", sb), cur("gb", qb),
            cache_blk, cache_blk,
            pl.BlockSpec((None, 1, slab), lay3),
            pl.BlockSpec((None, 1, slab), lay3),
            pl.BlockSpec((slab, slab), const2),
            pl.BlockSpec(bias.shape, const2),
            pl.BlockSpec((None,) + sink_col.shape[1:], lay3),
        ] + [pl.BlockSpec(memory_space=pl.ANY)] * len(extra_in),
        out_specs=[
            pl.BlockSpec((rows, nq), lambda b: (b, 0)),
            cache_blk, cache_blk,
        ],
        out_shape=[
            jax.ShapeDtypeStruct((batch * seq, nq), BF16),
            jax.ShapeDtypeStruct((depth, batch, window, slab), F32),
            jax.ShapeDtypeStruct((depth, batch, window, slab), F32),
        ],
        input_output_aliases={11: 1, 12: 2} if extra_in else {},
        compiler_params=pltpu.CompilerParams(
            dimension_semantics=("parallel",), vmem_limit_bytes=VMEM_LIMIT),
        name="swa_sample",
    )(proj, proj, proj, proj, cache_k, cache_v, qw, kw, bd, bias, sink_col, *extra_in)


def _dense_kernel(hp_ref, hs_ref, oap_ref, oas_ref, obp_ref, obs_ref, pp_ref, ps_ref, wo_ref, nf_ref,
                  wu_ref, wd_ref, np_ref, wg_ref, wp_ref, outp_ref, outs_ref, *, npt, ff_chunk):
    is_prompt = pl.program_id(0) < npt
    oa = jnp.where(is_prompt, oap_ref[...], oas_ref[...])
    ob = jnp.where(is_prompt, obp_ref[...], obs_ref[...])
    h = jnp.where(is_prompt, hp_ref[...], hs_ref[...])
    h = h + jnp.dot(oa + ob, wo_ref[...], preferred_element_type=F32)
    xn = (_rms(h) * nf_ref[...]).astype(BF16)
    acc = jnp.zeros_like(h)
    for c in range(wu_ref.shape[1] // ff_chunk):
        sl = slice(c * ff_chunk, (c + 1) * ff_chunk)
        hid = jnp.maximum(jnp.dot(xn, wu_ref[:, sl], preferred_element_type=F32), 0.0)
        acc = acc + jnp.dot((hid * hid).astype(BF16), wd_ref[sl, :], preferred_element_type=F32)
    h = h + acc
    xn = (_rms(h) * np_ref[...]).astype(BF16)
    gate = _sigmoid(jnp.dot(xn, wg_ref[...], preferred_element_type=F32))
    p = jnp.where(is_prompt, pp_ref[...], ps_ref[...])
    pe = jnp.dot(p.astype(BF16), wp_ref[...], preferred_element_type=F32)
    out = h + gate * pe

    @pl.when(is_prompt)
    def _():
        outp_ref[...] = out

    @pl.when(jnp.logical_not(is_prompt))
    def _():
        outs_ref[...] = out


def _dense(h_p, h_s, oa_p, oa_s, ob_p, ob_s, p_p, p_s, wo, nf, wu, wd, npl, wg, wp, *, layer, tm):
    tp, d = h_p.shape
    ts = h_s.shape[0]
    npt = tp // tm
    rows_p = lambda m: (jnp.minimum(m, npt - 1), 0)
    rows_s = lambda m: (jnp.maximum(m - npt, 0), 0)
    resident = lambda a: pl.BlockSpec((None,) + a.shape[1:], lambda m: (layer, 0, 0), pipeline_mode=pl.Buffered(1))
    return pl.pallas_call(
        functools.partial(_dense_kernel, npt=npt, ff_chunk=1024),
        grid=((tp + ts) // tm,),
        in_specs=[
            pl.BlockSpec((tm, d), rows_p), pl.BlockSpec((tm, d), rows_s),
            pl.BlockSpec((tm, d), rows_p), pl.BlockSpec((tm, d), rows_s),
            pl.BlockSpec((tm, d), rows_p), pl.BlockSpec((tm, d), rows_s),
            pl.BlockSpec((None, tm, p_p.shape[2]), lambda m: (layer, jnp.minimum(m, npt - 1), 0)),
            pl.BlockSpec((None, tm, p_s.shape[2]), lambda m: (layer, jnp.maximum(m - npt, 0), 0)),
            resident(wo), resident(nf), resident(wu), resident(wd),
            resident(npl), resident(wg), resident(wp),
        ],
        out_specs=[pl.BlockSpec((tm, d), rows_p), pl.BlockSpec((tm, d), rows_s)],
        out_shape=[jax.ShapeDtypeStruct((tp, d), F32), jax.ShapeDtypeStruct((ts, d), F32)],
        compiler_params=pltpu.CompilerParams(
            dimension_semantics=("arbitrary",), vmem_limit_bytes=VMEM_LIMIT),
        name="dense",
    )(h_p, h_s, oa_p, oa_s, ob_p, ob_s, p_p, p_s, wo, nf, wu, wd, npl, wg, wp)


def _seq_mask(seq_len):
    i = np.arange(GDN_CHUNK)
    m = (i[:, None] // seq_len == i[None, :] // seq_len) & (i[None, :] <= i[:, None])
    return m.astype(np.float32)


def kernel(x_prompt, x_sample, cache_conv, state_gdn, cache_swa_k, cache_swa_v, p_prompt, p_sample, norm_mix, w_in, conv_w, a_log, dt_bias, gdn_norm, q_norm, k_norm, attn_sinks, w_out, norm_ffn, w_up, w_down, norm_ple, w_ple_gate, w_ple_proj):
    batch, seq, d = x_prompt.shape
    dbatch, dseq, _ = x_sample.shape
    depth = w_in.shape[0]
    heads, dk, dv = state_gdn.shape[2:]
    window, kvh, hd = cache_swa_k.shape[2:]
    nq = d
    nheads = nq // hd
    groups = nheads // kvh
    slab = kvh * hd
    key_dim = heads * dk
    val_dim = heads * dv
    conv_dim = 2 * key_dim + val_dim
    width = conv_w.shape[1]
    tp, ts = batch * seq, dbatch * dseq
    assert dk == LANES and dv == LANES and val_dim == d and key_dim == d and slab % LANES == 0
    assert seq % window == 0 and window == GDN_CHUNK and GDN_CHUNK % dseq == 0 and dseq >= width - 1
    assert dseq == SUBLANES and ts % GDN_CHUNK == 0

    o_z = conv_dim
    o_b = o_z + val_dim
    o_a = o_b + heads
    o_sq = o_a + heads
    o_sk = o_sq + nq
    o_sv = o_sk + slab
    o_g = o_sv + slab
    col_h = {"q": 0, "k": 1, "v": 2, "z": 0, "ga": 2}
    off = {"sq": val_dim, "gb": val_dim + nq + d, "sk": val_dim + nq + 2 * d, "sv": val_dim + nq + 2 * d + slab}
    col_b = {name: o // LANES for name, o in off.items()}
    cb = conv_dim // LANES

    h_p = x_prompt.reshape(tp, d)
    h_s = x_sample.reshape(ts, d)
    bf = lambda a: a.astype(BF16)
    wo_all, wu_all, wd_all = bf(w_out), bf(w_up), bf(w_down)
    wg_all, wp_all = bf(w_ple_gate), bf(w_ple_proj)
    nf_all, np_all = norm_ffn[:, None, :], norm_ple[:, None, :]
    p_p = p_prompt.reshape(depth, tp, -1)
    p_s = p_sample.reshape(depth, ts, -1)

    masks = jnp.asarray(np.stack([_seq_mask(GDN_CHUNK), _seq_mask(dseq)]), BF16)
    mincl_p = jnp.asarray(_seq_mask(GDN_CHUNK), F32)
    mincl_s = jnp.asarray(_seq_mask(dseq), F32)
    bd = jnp.asarray(np.kron(np.eye(kvh), np.full((hd, hd), 1.0 / hd)), BF16)

    head_of = np.array([(p % kvh) * groups + p // kvh for p in range(nheads)])
    slopes = jnp.exp2(-8.0 * (jnp.asarray(head_of, F32) + 1.0) / nheads)
    qi = np.arange(window)[:, None]
    kj = np.arange(2 * window)[None, :]
    dist = window + qi - kj
    ok = (dist >= 0) & (dist <= window)
    ok_first = ok & (kj >= window)
    dist_f = jnp.asarray(dist, F32)
    bias_p = jnp.stack([
        jnp.where(jnp.asarray(okv)[None], -slopes[:, None, None] * dist_f[None], -jnp.inf) for okv in (ok_first, ok)])
    nkeys = -(-(window + dseq) // 16) * 16
    ti = np.tile(np.arange(dseq), nheads)[:, None]
    sj = np.arange(nkeys)[None, :]
    dist_s = ti + window - sj
    ok_s = (dist_s >= 0) & (dist_s <= window) & (sj < window + dseq)
    slopes_rows = jnp.repeat(slopes, dseq)[:, None]
    bias_s = jnp.where(jnp.asarray(ok_s), -slopes_rows * jnp.asarray(dist_s, F32), -jnp.inf)

    hist_all = jnp.pad(cache_conv, ((0, 0), (0, 0), (dseq - cache_conv.shape[2], 0), (0, 0)))
    hist_all = hist_all.reshape(depth, ts, conv_dim)
    ck_all = cache_swa_k.reshape(depth, dbatch, window, slab)
    cv_all = cache_swa_v.reshape(depth, dbatch, window, slab)
    cw_all = jnp.transpose(conv_w.reshape(depth, width, cb, LANES), (0, 2, 1, 3))

    avec_all = (jnp.zeros((depth, 2, LANES), F32).at[:, 0, heads:2 * heads].set(a_log)
                .at[:, 1, heads:2 * heads].set(dt_bias))
    nm_all, gn_all = norm_mix[:, None, :], gdn_norm[:, None, :]
    sinks_all = attn_sinks[:, head_of]
    sink_col_all = jnp.repeat(sinks_all, dseq, axis=1)[:, :, None]
    qw_all = jnp.tile(q_norm, (1, kvh))[:, None, :]
    kw_all = jnp.tile(k_norm, (1, kvh))[:, None, :]
    w_t = jnp.swapaxes(w_in, 1, 2)
    wba_all = jnp.pad(w_t[:, o_b:o_sq], ((0, 0), (0, LANES - 2 * heads), (0, 0))).astype(BF16)

    def regroup(x):
        return jnp.swapaxes(x.reshape(depth, kvh, groups, hd, d), 1, 2).reshape(x.shape)

    w_main_all = jnp.concatenate([
        w_t[:, :o_b], regroup(w_t[:, o_sq:o_sk]), w_t[:, o_g:o_g + d], regroup(w_t[:, o_g + d:]),
        w_t[:, o_sk:o_g]], axis=1).astype(BF16)

    outs = {n: [] for n in ("conv_p", "gdn_p", "k_p", "v_p", "conv_s")}
    gdn_s = None
    kv_s = None
    for i in range(depth):
        proj32, proj16, bgc = _inproj(h_p, h_s, nm_all, w_main_all, wba_all, avec_all, masks, layer=i, heads=heads,
                                      tm=1024, tn=1536, n_f32=conv_dim)

        oa_p, gdn_p, tails = _gdn_prompt(proj32, proj16, bgc, cw_all, gn_all, mincl_p, layer=i, batch=batch, seq=seq,
                                         heads=heads, dk=dk, col=col_h, rows=512, heads_per_iter=4)
        oa_s, gdn_s, raw_s = _gdn_sample(proj32, proj16, bgc, hist_all, state_gdn, cw_all, gn_all, mincl_s, gdn_s, layer=i,
                                  row0=tp, batch=dbatch, seq=dseq, heads=heads, dk=dk, col=col_h, heads_per_step=4)

        ob_p, k_p, v_p = _swa_prompt(proj16, sinks_all, qw_all, kw_all, bd, bias_p, layer=i, batch=batch, seq=seq,
                                     window=window, kvh=kvh, hd=hd, nq=nq, col=col_b, nsub=4)
        ob_s, *kv_s = _swa_sample(proj16, ck_all, cv_all, qw_all, kw_all, bd, bias_s, sink_col_all, kv_s,
                                  layer=i, row0=tp, batch=dbatch, seq=dseq, kvh=kvh, hd=hd, nq=nq, col=col_b, nseq=8)

        h_p, h_s = _dense(h_p, h_s, oa_p, oa_s, ob_p, ob_s, p_p, p_s, wo_all, nf_all, wu_all, wd_all, np_all,
                          wg_all, wp_all, layer=i, tm=512)

        outs["conv_p"].append(tails)
        outs["conv_s"].append(raw_s)
        outs["gdn_p"].append(gdn_p)
        outs["k_p"].append(k_p.reshape(batch, window, kvh, hd))
        outs["v_p"].append(v_p.reshape(batch, window, kvh, hd))

    st = lambda n: jnp.stack(outs[n])
    conv_p = st("conv_p")[:, :, SUBLANES - (width - 1):, :]
    conv_s = st("conv_s").reshape(depth, 3, dbatch, dseq, key_dim)[:, :, :, dseq - (width - 1):, :]
    conv_s = jnp.transpose(conv_s, (0, 2, 3, 1, 4)).reshape(depth, dbatch, width - 1, conv_dim)
    return (h_p.reshape(batch, seq, d), h_s.reshape(dbatch, dseq, d),
            conv_p, st("gdn_p"), st("k_p"), st("v_p"), conv_s, gdn_s,
            kv_s[0].reshape(depth, dbatch, window, kvh, hd), kv_s[1].reshape(depth, dbatch, window, kvh, hd))
```

```python
import functools

import numpy as np
import jax
import jax.numpy as jnp
from jax import lax
from jax.experimental import pallas as pl
from jax.experimental.pallas import tpu as pltpu

F32 = jnp.float32
BF16 = jnp.bfloat16
EPS = 1e-6
LANES = 128
SUBLANES = 8
MXU_N = 256
GDN_CHUNK = 128
VMEM_LIMIT = 56 * 1024 * 1024


def _sigmoid(x):
    return 0.5 * jnp.tanh(0.5 * x) + 0.5


def _silu(x):
    hx = 0.5 * x
    return hx + hx * jnp.tanh(hx)


def _dot(a, b):
    return jnp.dot(a.astype(BF16), b.astype(BF16), preferred_element_type=F32)


def _dot_nt(a, b):
    return lax.dot_general(a.astype(BF16), b.astype(BF16), (((1,), (1,)), ((), ())), preferred_element_type=F32)


def _dot_tn(a, b):
    return lax.dot_general(a.astype(BF16), b.astype(BF16), (((0,), (0,)), ((), ())), preferred_element_type=F32)


def _rms(x):
    return x * lax.rsqrt(jnp.mean(x * x, axis=-1, keepdims=True) + EPS)


def _l2(x):
    return x * lax.rsqrt(jnp.sum(x * x, axis=-1, keepdims=True) + EPS)


def _inproj_kernel(xp_ref, xs_ref, nw_ref, w_ref, wba_ref, avec_ref, mask_ref, proj32_ref, proj16_ref, bgc_ref,
                   xn_ref, *, heads, npt, n32):
    n = pl.program_id(1)

    @pl.when(n == 0)
    def _():
        x = jnp.where(pl.program_id(0) < npt, xp_ref[...], xs_ref[...])
        xn = (_rms(x) * nw_ref[...]).astype(BF16)
        xn_ref[...] = xn
        ba = _dot_nt(xn, wba_ref[...])
        beta = _sigmoid(ba)
        xs = ba + avec_ref[1:2, :]
        softplus = jnp.maximum(xs, 0.0) + jnp.log1p(jnp.exp(-jnp.abs(xs)))
        g = -jnp.exp(avec_ref[0:1, :]) * softplus
        g1 = g.astype(BF16)
        r1 = g - g1.astype(F32)
        g2 = r1.astype(BF16)
        g3 = (r1 - g2.astype(F32)).astype(BF16)
        m = mask_ref[...]
        rows = g.shape[0]
        lane = lax.broadcasted_iota(jnp.int32, (GDN_CHUNK, LANES), 1)
        for r in range(rows // GDN_CHUNK):
            sl = slice(r * GDN_CHUNK, (r + 1) * GDN_CHUNK)
            gc = (jnp.dot(m, g1[sl], preferred_element_type=F32)
                  + jnp.dot(m, g2[sl], preferred_element_type=F32)
                  + jnp.dot(m, g3[sl], preferred_element_type=F32))
            bgc_ref[sl, :] = jnp.where(lane < heads, beta[sl], gc)

    def project(out_ref):
        xn = xn_ref[...]
        for j in range(w_ref.shape[0] // MXU_N):
            r = _dot_nt(xn, w_ref[j * MXU_N:(j + 1) * MXU_N, :])
            for i in range(MXU_N // LANES):
                out_ref[j * (MXU_N // LANES) + i] = r[:, i * LANES:(i + 1) * LANES].astype(out_ref.dtype)

    @pl.when(n < n32)
    def _():
        project(proj32_ref)

    @pl.when(n >= n32)
    def _():
        project(proj16_ref)


def _inproj(h_p, h_s, nw, w, wba, avec, masks, *, layer, heads, tm, tn, n_f32):
    d = h_p.shape[1]
    t_all = h_p.shape[0] + h_s.shape[0]
    n_out = w.shape[1]
    npt = h_p.shape[0] // tm
    n32 = n_f32 // tn
    assert n32 * tn == n_f32
    return pl.pallas_call(
        functools.partial(_inproj_kernel, heads=heads, npt=npt, n32=n32),
        grid=(t_all // tm, n_out // tn),
        in_specs=[
            pl.BlockSpec((tm, d), lambda m, n: (jnp.minimum(m, npt - 1), 0)),
            pl.BlockSpec((tm, d), lambda m, n: (jnp.maximum(m - npt, 0), 0)),
            pl.BlockSpec((None, 1, d), lambda m, n: (layer, 0, 0)),
            pl.BlockSpec((None, tn, d), lambda m, n: (layer, n, 0)),
            pl.BlockSpec((None, LANES, d), lambda m, n: (layer, 0, 0)),
            pl.BlockSpec((None, 2, LANES), lambda m, n: (layer, 0, 0)),
            pl.BlockSpec((None, GDN_CHUNK, GDN_CHUNK), lambda m, n: (jnp.where(m >= npt, 1, 0), 0, 0)),
        ],
        out_specs=[
            pl.BlockSpec((tn // LANES, tm, LANES), lambda m, n: (jnp.minimum(n, n32 - 1), m, 0)),
            pl.BlockSpec((tn // LANES, tm, LANES), lambda m, n: (jnp.maximum(n - n32, 0), m, 0)),
            pl.BlockSpec((tm, LANES), lambda m, n: (m, 0)),
        ],
        out_shape=[
            jax.ShapeDtypeStruct((n_f32 // LANES, t_all, LANES), F32),
            jax.ShapeDtypeStruct(((n_out - n_f32) // LANES, t_all, LANES), BF16),
            jax.ShapeDtypeStruct((t_all, LANES), F32),
        ],
        scratch_shapes=[pltpu.VMEM((tm, d), BF16)],
        compiler_params=pltpu.CompilerParams(
            dimension_semantics=("parallel", "arbitrary"), vmem_limit_bytes=VMEM_LIMIT),
        name="inproj",
    )(h_p, h_s, nw, w, wba, avec, masks)


def _gdn_chunks_local(qs, ks, vs, betas, gcs, mincl, seq_len):
    n = len(qs)
    c = qs[0].shape[0]
    dv = vs[0].shape[1]
    row = lax.broadcasted_iota(jnp.int32, (c, c), 0)
    col = lax.broadcasted_iota(jnp.int32, (c, c), 1)
    decs, egs = [], []
    for gc in gcs:
        gcb = jnp.broadcast_to(gc, (c, c))
        decs.append(jnp.exp(jnp.where(mincl > 0, gcb - gcb.T, -jnp.inf)))
        egs.append(jnp.exp(gc))
    kbs = [ks[i] * betas[i] for i in range(n)]
    kks = [_dot_nt(jnp.concatenate([kbs[i], qs[i]], axis=0), ks[i]) for i in range(n)]
    a_s = [jnp.where(row == col, 0.0, kks[i][:c] * decs[i]) for i in range(n)]
    qks = [kks[i][c:] * decs[i] for i in range(n)]
    same = (row >> 1) == (col >> 1)
    eye = jnp.where(row == col, 1.0, 0.0)
    ts = [eye - jnp.where(same, a, 0.0) for a in a_s]
    na16 = [(-a).astype(BF16) for a in a_s]
    for lvl in range(1, seq_len.bit_length() - 1):
        wider = (row >> (lvl + 1)) == (col >> (lvl + 1))
        sel = wider & jnp.logical_not(same)
        t16 = [t.astype(BF16) for t in ts]
        mids = [jnp.dot(na16[i], t16[i], preferred_element_type=F32) for i in range(n)]
        ts = [jnp.where(sel, jnp.dot(t16[i], mids[i].astype(BF16), preferred_element_type=F32), ts[i])
              for i in range(n)]
        same = wider
    sols = [_dot(ts[i], jnp.concatenate([vs[i] * betas[i], kbs[i] * egs[i]], axis=1)) for i in range(n)]
    return [s[:, :dv] for s in sols], [s[:, dv:] for s in sols], qks, egs


def _conv_taps(u, ru_fn, cw):
    width = cw.shape[0]
    out = u * cw[width - 1:width, :]
    for j in range(1, width):
        out = out + ru_fn(j) * cw[width - 1 - j:width - j, :]
    return _silu(out)


def _gdn_finish(o, z, ga, gn):
    return (_rms(o) * gn * _silu(z.astype(F32)) * _sigmoid(ga.astype(F32))).astype(BF16)


def _head_scalars(bgc, h, heads):
    lane = lax.broadcasted_iota(jnp.int32, bgc.shape, 1)
    beta = jnp.sum(jnp.where(lane == h, bgc, 0.0), axis=-1, keepdims=True)
    gc = jnp.sum(jnp.where(lane == h + heads, bgc, 0.0), axis=-1, keepdims=True)
    return beta, gc


def _gdn_prompt_kernel(q_ref, k_ref, v_ref, z_ref, ga_ref, bgc_ref, cw_ref, gn_ref, mincl_ref,
                       o_ref, sfin_ref, tails_ref,
                       s_ref, tail_ref, u_s, wqe_s, qk_s, kd_s, el_s, *, heads_per_iter):
    c_id = pl.program_id(1)
    heads, rows, dk = q_ref.shape
    nchunk = rows // GDN_CHUNK
    cc = GDN_CHUNK

    @pl.when(c_id == 0)
    def _():
        s_ref[...] = jnp.zeros_like(s_ref)
        tail_ref[...] = jnp.zeros_like(tail_ref)

    row8 = lax.broadcasted_iota(jnp.int32, (SUBLANES, dk), 0)
    bgc = bgc_ref[...]
    mincl = mincl_ref[...]

    def conv(kind, h, u_ref):
        u = u_ref[h]
        tail = tail_ref[kind * heads + h]

        def delayed(j):
            head = jnp.where(row8 < j, pltpu.roll(tail, j, 0), pltpu.roll(u[:SUBLANES], j, 0))
            return jnp.concatenate([head, u_ref[h, pl.ds(SUBLANES - j, rows - SUBLANES), :]], axis=0)

        out = _conv_taps(u, delayed, cw_ref[kind * heads + h])
        tail_ref[kind * heads + h] = u[rows - SUBLANES:]
        return out

    def local_body(it, carry):
        hs, qs, ks, vs, betas, gcs = [], [], [], [], [], []
        for hh in range(heads_per_iter):
            h = it * heads_per_iter + hh
            q = _l2(conv(0, h, q_ref)) * (dk ** -0.5)
            k = _l2(conv(1, h, k_ref))
            v = conv(2, h, v_ref)
            beta, gc = _head_scalars(bgc, h, heads)
            for ci in range(nchunk):
                sl = slice(ci * cc, (ci + 1) * cc)
                hs.append((h, ci))
                qs.append(q[sl]); ks.append(k[sl]); vs.append(v[sl]); betas.append(beta[sl]); gcs.append(gc[sl])
        us, ws, qks, egs = _gdn_chunks_local(qs, ks, vs, betas, gcs, mincl, cc)
        for i, (h, ci) in enumerate(hs):
            g_last = gcs[i][cc - 1:cc, :]
            u_s[h, ci] = us[i]
            wqe_s[h, ci, :cc] = ws[i].astype(BF16)
            wqe_s[h, ci, cc:] = (qs[i] * egs[i]).astype(BF16)
            qk_s[h, ci] = qks[i].astype(BF16)
            kd_s[h, ci] = (ks[i] * jnp.exp(g_last - gcs[i])).astype(BF16)
            el_s[h, ci] = jnp.broadcast_to(jnp.exp(g_last), (SUBLANES, dk))
        return carry

    lax.fori_loop(0, heads // heads_per_iter, local_body, 0)

    states = [s_ref[h] for h in range(heads)]
    gn = gn_ref[...]
    for ci in range(nchunk):
        sl = slice(ci * cc, (ci + 1) * cc)
        wqs = [jnp.dot(wqe_s[h, ci], states[h].astype(BF16), preferred_element_type=F32) for h in range(heads)]
        v_news = [u_s[h, ci] - wqs[h][:cc] for h in range(heads)]
        vn16 = [v.astype(BF16) for v in v_news]
        outs = [wqs[h][cc:] + jnp.dot(qk_s[h, ci], vn16[h], preferred_element_type=F32) for h in range(heads)]
        states = [states[h] * el_s[h, ci][0:1, :]
                  + lax.dot_general(kd_s[h, ci], vn16[h], (((0,), (0,)), ((), ())), preferred_element_type=F32)
                  for h in range(heads)]
        for h in range(heads):
            o_ref[sl, h * dk:(h + 1) * dk] = _gdn_finish(outs[h], z_ref[h, sl, :], ga_ref[h, sl, :], gn)
    for h in range(heads):
        s_ref[h] = states[h]
        sfin_ref[h] = states[h]
    for idx in range(3 * heads):
        tails_ref[:, idx * dk:(idx + 1) * dk] = tail_ref[idx]


def _gdn_prompt(proj32, proj16, bgc, cw, gn, mincl, *, layer, batch, seq, heads, dk, col, rows, heads_per_iter):
    ncb = seq // rows
    nchunk = rows // GDN_CHUNK
    blk = lambda name: pl.BlockSpec((heads, rows, dk), lambda b, c, o=col[name]: (o, b * ncb + c, 0))
    return pl.pallas_call(
        functools.partial(_gdn_prompt_kernel, heads_per_iter=heads_per_iter),
        grid=(batch, ncb),
        in_specs=[
            blk("q"), blk("k"), blk("v"), blk("z"), blk("ga"),
            pl.BlockSpec((rows, LANES), lambda b, c: (b * ncb + c, 0)),
            pl.BlockSpec((None,) + cw.shape[1:], lambda b, c: (layer, 0, 0, 0)),
            pl.BlockSpec((None, 1, dk), lambda b, c: (layer, 0, 0)),
            pl.BlockSpec((GDN_CHUNK, GDN_CHUNK), lambda b, c: (0, 0)),
        ],
        out_specs=[
            pl.BlockSpec((rows, heads * dk), lambda b, c: (b * ncb + c, 0)),
            pl.BlockSpec((None, heads, dk, dk), lambda b, c: (b, 0, 0, 0)),
            pl.BlockSpec((None, SUBLANES, 3 * heads * dk), lambda b, c: (b, 0, 0)),
        ],
        out_shape=[
            jax.ShapeDtypeStruct((batch * seq, heads * dk), BF16),
            jax.ShapeDtypeStruct((batch, heads, dk, dk), F32),
            jax.ShapeDtypeStruct((batch, SUBLANES, 3 * heads * dk), F32),
        ],
        scratch_shapes=[
            pltpu.VMEM((heads, dk, dk), F32),
            pltpu.VMEM((3 * heads, SUBLANES, dk), F32),
            pltpu.VMEM((heads, nchunk, GDN_CHUNK, dk), F32),
            pltpu.VMEM((heads, nchunk, 2 * GDN_CHUNK, dk), BF16),
            pltpu.VMEM((heads, nchunk, GDN_CHUNK, GDN_CHUNK), BF16),
            pltpu.VMEM((heads, nchunk, GDN_CHUNK, dk), BF16),
            pltpu.VMEM((heads, nchunk, SUBLANES, dk), F32),
        ],
        compiler_params=pltpu.CompilerParams(
            dimension_semantics=("parallel", "arbitrary"), vmem_limit_bytes=VMEM_LIMIT),
        name="gdn_prompt",
    )(proj32, proj32, proj32, proj16, proj16, bgc, cw, gn, mincl)


def _gdn_sample_kernel(q_ref, k_ref, v_ref, z_ref, ga_ref, bgc_ref, hq_ref, hk_ref, hv_ref,
                       cw_ref, gn_ref, mincl_ref, s0_ref, *rest, heads, seq):
    o_ref, sout_ref, raw_ref = rest[-3:]
    hpi, rows, dk = q_ref.shape
    nseq = rows // seq
    t_idx = lax.broadcasted_iota(jnp.int32, (nseq, seq, dk), 1)
    bgc = bgc_ref[...]

    def conv(kind, hh, h, u_ref, hist_ref):
        u = u_ref[hh]
        raw_ref[kind, :, hh * dk:(hh + 1) * dk] = u
        u3 = u.reshape(nseq, seq, dk)
        hist3 = hist_ref[:, hh * dk:(hh + 1) * dk].reshape(nseq, seq, dk)

        def delayed(j):
            xj = jnp.where(t_idx < j, pltpu.roll(hist3, j, 1), pltpu.roll(u3, j, 1))
            return xj.reshape(rows, dk)

        return _conv_taps(u, delayed, cw_ref[kind * heads + h])

    qs, ks, vs, betas, gcs = [], [], [], [], []
    for hh in range(hpi):
        h = pl.program_id(1) * hpi + hh
        qs.append(_l2(conv(0, hh, h, q_ref, hq_ref)) * (dk ** -0.5))
        ks.append(_l2(conv(1, hh, h, k_ref, hk_ref)))
        vs.append(conv(2, hh, h, v_ref, hv_ref))
        beta, gc = _head_scalars(bgc, h, heads)
        betas.append(beta)
        gcs.append(gc)
    us, ws, qks, egs = _gdn_chunks_local(qs, ks, vs, betas, gcs, mincl_ref[...], seq)

    pad = jnp.zeros((seq, dk), F32)
    pairs = [(hh, si) for hh in range(hpi) for si in range(nseq)]
    sl = lambda si: slice(si * seq, (si + 1) * seq)
    s0 = {p: s0_ref[p[1], p[0]] for p in pairs}
    qes = [qs[hh] * egs[hh] for hh in range(hpi)]
    wqs = {(hh, si): _dot(jnp.concatenate([ws[hh][sl(si)], qes[hh][sl(si)]], axis=0), s0[(hh, si)])
           for hh, si in pairs}
    v_news = {(hh, si): us[hh][sl(si)] - wqs[(hh, si)][:seq] for hh, si in pairs}
    for hh, si in pairs:
        g_last = gcs[hh][(si + 1) * seq - 1:(si + 1) * seq, :]
        kd = ks[hh][sl(si)] * jnp.exp(g_last - gcs[hh][sl(si)])
        sout_ref[si, hh] = s0[(hh, si)] * jnp.exp(g_last) + _dot_tn(
            jnp.concatenate([kd, pad], axis=0), jnp.concatenate([v_news[(hh, si)], pad], axis=0))
    gn = gn_ref[...]
    for hh in range(hpi):
        o = (jnp.concatenate([wqs[(hh, si)][seq:] for si in range(nseq)], axis=0)
             + _dot(qks[hh], jnp.concatenate([v_news[(hh, si)] for si in range(nseq)], axis=0)))
        o_ref[:, hh * dk:(hh + 1) * dk] = _gdn_finish(o, z_ref[hh], ga_ref[hh], gn)


def _gdn_sample(proj32, proj16, bgc, hist, s0, cw, gn, mincl, s_prev, *, layer, row0, batch, seq, heads, dk, col,
                heads_per_step):
    rows = GDN_CHUNK
    nseq = rows // seq
    r0 = row0 // rows
    depth = s0.shape[0]
    hpi = heads_per_step
    hgroups = heads // hpi
    blk = lambda name: pl.BlockSpec((hpi, rows, dk), lambda b, h, o=col[name]: (o * hgroups + h, r0 + b, 0))
    hblk = lambda name: pl.BlockSpec((None, rows, hpi * dk), lambda b, h, o=col[name]: (layer, b, o * hgroups + h))
    state_blk = pl.BlockSpec((None, nseq, hpi, dk, dk), lambda b, h: (layer, b, h, 0, 0))
    extra_in = [] if s_prev is None else [s_prev]
    return pl.pallas_call(
        functools.partial(_gdn_sample_kernel, heads=heads, seq=seq),
        grid=(batch * seq // rows, hgroups),
        in_specs=[
            blk("q"), blk("k"), blk("v"), blk("z"), blk("ga"),
            pl.BlockSpec((rows, LANES), lambda b, h: (r0 + b, 0)),
            hblk("q"), hblk("k"), hblk("v"),
            pl.BlockSpec((None,) + cw.shape[1:], lambda b, h: (layer, 0, 0, 0)),
            pl.BlockSpec((None, 1, dk), lambda b, h: (layer, 0, 0)),
            pl.BlockSpec((GDN_CHUNK, GDN_CHUNK), lambda b, h: (0, 0)),
            state_blk,
        ] + [pl.BlockSpec(memory_space=pl.ANY)] * len(extra_in),
        out_specs=[
            pl.BlockSpec((rows, hpi * dk), lambda b, h: (b, h)),
            state_blk,
            pl.BlockSpec((3, rows, hpi * dk), lambda b, h: (0, b, h)),
        ],
        out_shape=[
            jax.ShapeDtypeStruct((batch * seq, heads * dk), BF16),
            jax.ShapeDtypeStruct((depth, batch, heads, dk, dk), F32),
            jax.ShapeDtypeStruct((3, batch * seq, heads * dk), F32),
        ],
        input_output_aliases={13: 1} if extra_in else {},
        compiler_params=pltpu.CompilerParams(
            dimension_semantics=("parallel", "parallel"), vmem_limit_bytes=VMEM_LIMIT),
        name="gdn_sample",
    )(proj32, proj32, proj32, proj16, proj16, bgc, hist, hist, hist, cw, gn, mincl, s0, *extra_in)


def _chunk_rms(x, bd, w):
    ms = _dot(x * x, bd)
    return x * lax.rsqrt(ms + EPS) * w


def _wide(ref, first, n):
    return jnp.concatenate([ref[first + i] for i in range(n)], axis=-1)


def _natural_blocks(slabs, hd):
    groups = len(slabs)
    per_block = LANES // hd
    lane = lax.broadcasted_iota(jnp.int32, (slabs[0].shape[0], LANES), 1)
    blocks = []
    for b in range(slabs[0].shape[1] * groups // LANES):
        blk = None
        for i in range(per_block):
            j, g = divmod(b * per_block + i, groups)
            first = (hd * j) // LANES * LANES
            piece = slabs[g][:, first:first + LANES]
            shift = (i * hd - hd * j) % LANES
            if shift:
                piece = pltpu.roll(piece, shift, 1)
            blk = piece if blk is None else jnp.where(lane < i * hd, blk, piece)
        blocks.append(blk)
    return blocks


def _swa_prompt_kernel(sink_ref, q_ref, kp_ref, kc_ref, vp_ref, vc_ref, gb_ref, qw_ref, kw_ref, bd_ref, bias_ref,
                       o_ref, kout_ref, vout_ref, *, kvh, hd, layer):
    blk = kp_ref.shape[1]
    nsub = q_ref.shape[1] // blk
    slab = kvh * hd
    spb = slab // LANES
    groups = q_ref.shape[0] // spb
    bd = bd_ref[...]
    kn = _chunk_rms(_wide(kc_ref, 0, spb).astype(F32), bd, kw_ref[...])
    v_cur = _wide(vc_ref, 0, spb)
    kout_ref[...] = kn[(nsub - 1) * blk:]
    vout_ref[...] = v_cur[(nsub - 1) * blk:].astype(F32)
    k_all = jnp.concatenate([_chunk_rms(_wide(kp_ref, 0, spb).astype(F32), bd, kw_ref[...]), kn],
                            axis=0).astype(BF16)
    v_all = jnp.concatenate([_wide(vp_ref, 0, spb), v_cur], axis=0)
    lane_head = lax.broadcasted_iota(jnp.int32, (blk, slab), 1) // hd
    first_variant = jnp.minimum(pl.program_id(1), 1)

    for sub in range(nsub):
        rows = slice(sub * blk, (sub + 1) * blk)
        kband = k_all[sub * blk:(sub + 2) * blk]
        vband = v_all[sub * blk:(sub + 2) * blk]
        variant = first_variant if sub == 0 else 1

        def scores(g):
            qn = _chunk_rms(_wide(q_ref, g * spb, spb)[rows].astype(F32), bd, qw_ref[...]) * (hd ** -0.5)
            qs = jnp.concatenate([jnp.where(lane_head == j, qn, 0.0) for j in range(kvh)], axis=0).astype(BF16)
            return lax.dot_general(qs, kband, (((1,), (1,)), ((), ())), preferred_element_type=F32)

        s_groups = [scores(g) for g in range(groups)]
        gated = []
        for g in range(groups):
            s_all = s_groups[g]
            probs = []
            for j in range(kvh):
                p = g * kvh + j
                s = s_all[j * blk:(j + 1) * blk] + bias_ref[variant, p]
                sink = sink_ref[layer, p]
                m = jnp.maximum(jnp.max(s, axis=-1, keepdims=True), sink)
                e = jnp.exp(s - m)
                den = jnp.sum(e, axis=-1, keepdims=True) + jnp.exp(sink - m)
                probs.append((e * (1.0 / den)).astype(BF16))
            pv = jnp.dot(jnp.concatenate(probs, axis=0), vband, preferred_element_type=F32)
            acc = pv[(kvh - 1) * blk:]
            for j in range(kvh - 2, -1, -1):
                acc = jnp.where(lane_head == j, pv[j * blk:(j + 1) * blk], acc)
            gated.append(acc * _sigmoid(_wide(gb_ref, g * spb, spb)[rows].astype(F32)))
        for b, blk_out in enumerate(_natural_blocks(gated, hd)):
            o_ref[rows, b * LANES:(b + 1) * LANES] = blk_out.astype(BF16)


def _swa_prompt(proj, sinks, qw, kw, bd, bias, *, layer, batch, seq, window, kvh, hd, nq, col, nsub):
    step = nsub * window
    nb = seq // step
    slab = kvh * hd
    qb, sb = nq // LANES, slab // LANES
    cur = lambda name, n: pl.BlockSpec((n, step, LANES), lambda b, c, o=col[name] // n: (o, b * nb + c, 0))
    prev = lambda name, n: pl.BlockSpec(
        (n, window, LANES), lambda b, c, o=col[name] // n: (o, (b * nb + c) * nsub - jnp.minimum(c, 1), 0))
    const2 = lambda b, c: (0, 0)
    lay3 = lambda b, c: (layer, 0, 0)
    return pl.pallas_call(
        functools.partial(_swa_prompt_kernel, kvh=kvh, hd=hd, layer=layer),
        grid=(batch, nb),
        in_specs=[
            pl.BlockSpec(memory_space=pltpu.SMEM),
            cur("sq", qb), prev("sk", sb), cur("sk", sb), prev("sv", sb), cur("sv", sb), cur("gb", qb),
            pl.BlockSpec((None, 1, slab), lay3),
            pl.BlockSpec((None, 1, slab), lay3),
            pl.BlockSpec((slab, slab), const2),
            pl.BlockSpec(bias.shape, lambda b, c: (0, 0, 0, 0)),
        ],
        out_specs=[
            pl.BlockSpec((step, nq), lambda b, c: (b * nb + c, 0)),
            pl.BlockSpec((None, window, slab), lambda b, c: (b, 0, 0)),
            pl.BlockSpec((None, window, slab), lambda b, c: (b, 0, 0)),
        ],
        out_shape=[
            jax.ShapeDtypeStruct((batch * seq, nq), BF16),
            jax.ShapeDtypeStruct((batch, window, slab), F32),
            jax.ShapeDtypeStruct((batch, window, slab), F32),
        ],
        compiler_params=pltpu.CompilerParams(
            dimension_semantics=("parallel", "arbitrary"), vmem_limit_bytes=VMEM_LIMIT),
        name="swa_prompt",
    )(sinks, proj, proj, proj, proj, proj, proj, qw, kw, bd, bias)


def _swa_sample_kernel(q_ref, k_ref, v_ref, gb_ref, ck_ref, cv_ref, qw_ref, kw_ref, bd_ref, bias_ref, sink_ref,
                       *rest, kvh, hd, seq):
    o_ref, kout_ref, vout_ref = rest[-3:]
    rows = q_ref.shape[1]
    nseq = rows // seq
    slab = kvh * hd
    spb = slab // LANES
    groups = q_ref.shape[0] // spb
    window = ck_ref.shape[1]
    nkeys = bias_ref.shape[1]
    bd = bd_ref[...]
    kn = _chunk_rms(_wide(k_ref, 0, spb).astype(F32), bd, kw_ref[...]).reshape(nseq, seq, slab)
    vn = _wide(v_ref, 0, spb).astype(F32).reshape(nseq, seq, slab)
    ck = ck_ref[...]
    cv = cv_ref[...]
    kout_ref[...] = jnp.concatenate([ck[:, seq:, :], kn], axis=1)
    vout_ref[...] = jnp.concatenate([cv[:, seq:, :], vn], axis=1)
    zpad = jnp.zeros((nseq, nkeys - window - seq, slab), F32)
    k_all = jnp.concatenate([ck, kn, zpad], axis=1).astype(BF16)
    v_all = jnp.concatenate([cv, vn, zpad], axis=1).astype(BF16)

    lane_head = lax.broadcasted_iota(jnp.int32, (rows, slab), 1) // hd
    pieces = []
    for g in range(groups):
        qn = _chunk_rms(_wide(q_ref, g * spb, spb).astype(F32), bd, qw_ref[...]) * (hd ** -0.5)
        for j in range(kvh):
            pieces.append(jnp.where(lane_head == j, qn, 0.0).reshape(nseq, seq, slab))
    q_all = jnp.concatenate(pieces, axis=1).astype(BF16)
    s = jnp.einsum("bqd,bkd->bqk", q_all, k_all, preferred_element_type=F32) + bias_ref[...][None]
    sink = sink_ref[...][None]
    m = jnp.maximum(jnp.max(s, axis=-1, keepdims=True), sink)
    e = jnp.exp(s - m)
    den = jnp.sum(e, axis=-1, keepdims=True) + jnp.exp(sink - m)
    pv = jnp.einsum("bqk,bkd->bqd", (e * (1.0 / den)).astype(BF16), v_all, preferred_element_type=F32)
    lane_head3 = lax.broadcasted_iota(jnp.int32, (nseq, seq, slab), 2) // hd
    gated = []
    for g in range(groups):
        acc = jnp.zeros((nseq, seq, slab), F32)
        for j in range(kvh):
            p = g * kvh + j
            acc = acc + jnp.where(lane_head3 == j, pv[:, p * seq:(p + 1) * seq, :], 0.0)
        gated.append(acc.reshape(rows, slab) * _sigmoid(_wide(gb_ref, g * spb, spb).astype(F32)))
    for b, blk_out in enumerate(_natural_blocks(gated, hd)):
        o_ref[:, b * LANES:(b + 1) * LANES] = blk_out.astype(BF16)


def _swa_sample(proj, cache_k, cache_v, qw, kw, bd, bias, sink_col, kv_prev, *, layer, row0, batch, seq, kvh, hd,
                nq, col, nseq):
    rows = nseq * seq
    slab = kvh * hd
    depth, _, window, _ = cache_k.shape
    r0 = row0 // rows
    cache_blk = pl.BlockSpec((None, nseq, window, slab), lambda b: (layer, b, 0, 0))
    extra_in = [] if kv_prev is None else list(kv_prev)
    qb, sb = nq // LANES, slab // LANES
    cur = lambda name, n: pl.BlockSpec((n, rows, LANES), lambda b, o=col[name] // n: (o, r0 + b, 0))
    const2 = lambda b: (0, 0)
    lay3 = lambda b: (layer, 0, 0)
    return pl.pallas_call(
        functools.partial(_swa_sample_kernel, kvh=kvh, hd=hd, seq=seq),
        grid=(batch // nseq,),
        in_specs=[
            cur("sq", qb), cur("sk", sb), cur("sv", sb), cur("gb", qb),
            cache_blk, cache_blk,
            pl.BlockSpec((None, 1, slab), lay3),
            pl.BlockSpec((None, 1, slab), lay3),
            pl.BlockSpec((slab, slab), const2),
            pl.BlockSpec(bias.shape, const2),
            pl.BlockSpec((None,) + sink_col.shape[1:], lay3),
        ] + [pl.BlockSpec(memory_space=pl.ANY)] * len(extra_in),
        out_specs=[
            pl.BlockSpec((rows, nq), lambda b: (b, 0)),
            cache_blk, cache_blk,
        ],
        out_shape=[
            jax.ShapeDtypeStruct((batch * seq, nq), BF16),
            jax.ShapeDtypeStruct((depth, batch, window, slab), F32),
            jax.ShapeDtypeStruct((depth, batch, window, slab), F32),
        ],
        input_output_aliases={11: 1, 12: 2} if extra_in else {},
        compiler_params=pltpu.CompilerParams(
            dimension_semantics=("parallel",), vmem_limit_bytes=VMEM_LIMIT),
        name="swa_sample",
    )(proj, proj, proj, proj, cache_k, cache_v, qw, kw, bd, bias, sink_col, *extra_in)


def _dense_kernel(hp_ref, hs_ref, oap_ref, oas_ref, obp_ref, obs_ref, pp_ref, ps_ref, wo_ref, nf_ref,
                  wu_ref, wd_ref, np_ref, wg_ref, wp_ref, outp_ref, outs_ref, *, npt, ff_chunk):
    is_prompt = pl.program_id(0) < npt
    oa = jnp.where(is_prompt, oap_ref[...], oas_ref[...])
    ob = jnp.where(is_prompt, obp_ref[...], obs_ref[...])
    h = jnp.where(is_prompt, hp_ref[...], hs_ref[...])
    h = h + jnp.dot(oa + ob, wo_ref[...], preferred_element_type=F32)
    xn = (_rms(h) * nf_ref[...]).astype(BF16)
    acc = jnp.zeros_like(h)
    for c in range(wu_ref.shape[1] // ff_chunk):
        sl = slice(c * ff_chunk, (c + 1) * ff_chunk)
        hid = jnp.maximum(jnp.dot(xn, wu_ref[:, sl], preferred_element_type=F32), 0.0)
        acc = acc + jnp.dot((hid * hid).astype(BF16), wd_ref[sl, :], preferred_element_type=F32)
    h = h + acc
    xn = (_rms(h) * np_ref[...]).astype(BF16)
    gate = _sigmoid(jnp.dot(xn, wg_ref[...], preferred_element_type=F32))
    p = jnp.where(is_prompt, pp_ref[...], ps_ref[...])
    pe = jnp.dot(p.astype(BF16), wp_ref[...], preferred_element_type=F32)
    out = h + gate * pe

    @pl.when(is_prompt)
    def _():
        outp_ref[...] = out

    @pl.when(jnp.logical_not(is_prompt))
    def _():
        outs_ref[...] = out


def _dense(h_p, h_s, oa_p, oa_s, ob_p, ob_s, p_p, p_s, wo, nf, wu, wd, npl, wg, wp, *, layer, tm):
    tp, d = h_p.shape
    ts = h_s.shape[0]
    npt = tp // tm
    rows_p = lambda m: (jnp.minimum(m, npt - 1), 0)
    rows_s = lambda m: (jnp.maximum(m - npt, 0), 0)
    resident = lambda a: pl.BlockSpec((None,) + a.shape[1:], lambda m: (layer, 0, 0), pipeline_mode=pl.Buffered(1))
    return pl.pallas_call(
        functools.partial(_dense_kernel, npt=npt, ff_chunk=1024),
        grid=((tp + ts) // tm,),
        in_specs=[
            pl.BlockSpec((tm, d), rows_p), pl.BlockSpec((tm, d), rows_s),
            pl.BlockSpec((tm, d), rows_p), pl.BlockSpec((tm, d), rows_s),
            pl.BlockSpec((tm, d), rows_p), pl.BlockSpec((tm, d), rows_s),
            pl.BlockSpec((None, tm, p_p.shape[2]), lambda m: (layer, jnp.minimum(m, npt - 1), 0)),
            pl.BlockSpec((None, tm, p_s.shape[2]), lambda m: (layer, jnp.maximum(m - npt, 0), 0)),
            resident(wo), resident(nf), resident(wu), resident(wd),
            resident(npl), resident(wg), resident(wp),
        ],
        out_specs=[pl.BlockSpec((tm, d), rows_p), pl.BlockSpec((tm, d), rows_s)],
        out_shape=[jax.ShapeDtypeStruct((tp, d), F32), jax.ShapeDtypeStruct((ts, d), F32)],
        compiler_params=pltpu.CompilerParams(
            dimension_semantics=("arbitrary",), vmem_limit_bytes=VMEM_LIMIT),
        name="dense",
    )(h_p, h_s, oa_p, oa_s, ob_p, ob_s, p_p, p_s, wo, nf, wu, wd, npl, wg, wp)


def _seq_mask(seq_len):
    i = np.arange(GDN_CHUNK)
    m = (i[:, None] // seq_len == i[None, :] // seq_len) & (i[None, :] <= i[:, None])
    return m.astype(np.float32)


def kernel(x_prompt, x_sample, cache_conv, state_gdn, cache_swa_k, cache_swa_v, p_prompt, p_sample, norm_mix, w_in, conv_w, a_log, dt_bias, gdn_norm, q_norm, k_norm, attn_sinks, w_out, norm_ffn, w_up, w_down, norm_ple, w_ple_gate, w_ple_proj):
    batch, seq, d = x_prompt.shape
    dbatch, dseq, _ = x_sample.shape
    depth = w_in.shape[0]
    heads, dk, dv = state_gdn.shape[2:]
    window, kvh, hd = cache_swa_k.shape[2:]
    nq = d
    nheads = nq // hd
    groups = nheads // kvh
    slab = kvh * hd
    key_dim = heads * dk
    val_dim = heads * dv
    conv_dim = 2 * key_dim + val_dim
    width = conv_w.shape[1]
    tp, ts = batch * seq, dbatch * dseq
    assert dk == LANES and dv == LANES and val_dim == d and key_dim == d and slab % LANES == 0
    assert seq % window == 0 and window == GDN_CHUNK and GDN_CHUNK % dseq == 0 and dseq >= width - 1
    assert dseq == SUBLANES and ts % GDN_CHUNK == 0

    o_z = conv_dim
    o_b = o_z + val_dim
    o_a = o_b + heads
    o_sq = o_a + heads
    o_sk = o_sq + nq
    o_sv = o_sk + slab
    o_g = o_sv + slab
    col_h = {"q": 0, "k": 1, "v": 2, "z": 0, "ga": 2}
    off = {"sq": val_dim, "gb": val_dim + nq + d, "sk": val_dim + nq + 2 * d, "sv": val_dim + nq + 2 * d + slab}
    col_b = {name: o // LANES for name, o in off.items()}
    cb = conv_dim // LANES

    h_p = x_prompt.reshape(tp, d)
    h_s = x_sample.reshape(ts, d)
    bf = lambda a: a.astype(BF16)
    wo_all, wu_all, wd_all = bf(w_out), bf(w_up), bf(w_down)
    wg_all, wp_all = bf(w_ple_gate), bf(w_ple_proj)
    nf_all, np_all = norm_ffn[:, None, :], norm_ple[:, None, :]
    p_p = p_prompt.reshape(depth, tp, -1)
    p_s = p_sample.reshape(depth, ts, -1)

    masks = jnp.asarray(np.stack([_seq_mask(GDN_CHUNK), _seq_mask(dseq)]), BF16)
    mincl_p = jnp.asarray(_seq_mask(GDN_CHUNK), F32)
    mincl_s = jnp.asarray(_seq_mask(dseq), F32)
    bd = jnp.asarray(np.kron(np.eye(kvh), np.full((hd, hd), 1.0 / hd)), BF16)

    head_of = np.array([(p % kvh) * groups + p // kvh for p in range(nheads)])
    slopes = jnp.exp2(-8.0 * (jnp.asarray(head_of, F32) + 1.0) / nheads)
    qi = np.arange(window)[:, None]
    kj = np.arange(2 * window)[None, :]
    dist = window + qi - kj
    ok = (dist >= 0) & (dist <= window)
    ok_first = ok & (kj >= window)
    dist_f = jnp.asarray(dist, F32)
    bias_p = jnp.stack([
        jnp.where(jnp.asarray(okv)[None], -slopes[:, None, None] * dist_f[None], -jnp.inf) for okv in (ok_first, ok)])
    nkeys = -(-(window + dseq) // 16) * 16
    ti = np.tile(np.arange(dseq), nheads)[:, None]
    sj = np.arange(nkeys)[None, :]
    dist_s = ti + window - sj
    ok_s = (dist_s >= 0) & (dist_s <= window) & (sj < window + dseq)
    slopes_rows = jnp.repeat(slopes, dseq)[:, None]
    bias_s = jnp.where(jnp.asarray(ok_s), -slopes_rows * jnp.asarray(dist_s, F32), -jnp.inf)

    hist_all = jnp.pad(cache_conv, ((0, 0), (0, 0), (dseq - cache_conv.shape[2], 0), (0, 0)))
    hist_all = hist_all.reshape(depth, ts, conv_dim)
    ck_all = cache_swa_k.reshape(depth, dbatch, window, slab)
    cv_all = cache_swa_v.reshape(depth, dbatch, window, slab)
    cw_all = jnp.transpose(conv_w.reshape(depth, width, cb, LANES), (0, 2, 1, 3))

    avec_all = (jnp.zeros((depth, 2, LANES), F32).at[:, 0, heads:2 * heads].set(a_log)
                .at[:, 1, heads:2 * heads].set(dt_bias))
    nm_all, gn_all = norm_mix[:, None, :], gdn_norm[:, None, :]
    sinks_all = attn_sinks[:, head_of]
    sink_col_all = jnp.repeat(sinks_all, dseq, axis=1)[:, :, None]
    qw_all = jnp.tile(q_norm, (1, kvh))[:, None, :]
    kw_all = jnp.tile(k_norm, (1, kvh))[:, None, :]
    w_t = jnp.swapaxes(w_in, 1, 2)
    wba_all = jnp.pad(w_t[:, o_b:o_sq], ((0, 0), (0, LANES - 2 * heads), (0, 0))).astype(BF16)

    def regroup(x):
        return jnp.swapaxes(x.reshape(depth, kvh, groups, hd, d), 1, 2).reshape(x.shape)

    w_main_all = jnp.concatenate([
        w_t[:, :o_b], regroup(w_t[:, o_sq:o_sk]), w_t[:, o_g:o_g + d], regroup(w_t[:, o_g + d:]),
        w_t[:, o_sk:o_g]], axis=1).astype(BF16)

    outs = {n: [] for n in ("conv_p", "gdn_p", "k_p", "v_p", "conv_s")}
    gdn_s = None
    kv_s = None
    for i in range(depth):
        proj32, proj16, bgc = _inproj(h_p, h_s, nm_all, w_main_all, wba_all, avec_all, masks, layer=i, heads=heads,
                                      tm=1024, tn=1536, n_f32=conv_dim)

        oa_p, gdn_p, tails = _gdn_prompt(proj32, proj16, bgc, cw_all, gn_all, mincl_p, layer=i, batch=batch, seq=seq,
                                         heads=heads, dk=dk, col=col_h, rows=512, heads_per_iter=4)
        oa_s, gdn_s, raw_s = _gdn_sample(proj32, proj16, bgc, hist_all, state_gdn, cw_all, gn_all, mincl_s, gdn_s, layer=i,
                                  row0=tp, batch=dbatch, seq=dseq, heads=heads, dk=dk, col=col_h, heads_per_step=4)

        ob_p, k_p, v_p = _swa_prompt(proj16, sinks_all, qw_all, kw_all, bd, bias_p, layer=i, batch=batch, seq=seq,
                                     window=window, kvh=kvh, hd=hd, nq=nq, col=col_b, nsub=8)
        ob_s, *kv_s = _swa_sample(proj16, ck_all, cv_all, qw_all, kw_all, bd, bias_s, sink_col_all, kv_s,
                                  layer=i, row0=tp, batch=dbatch, seq=dseq, kvh=kvh, hd=hd, nq=nq, col=col_b, nseq=8)

        h_p, h_s = _dense(h_p, h_s, oa_p, oa_s, ob_p, ob_s, p_p, p_s, wo_all, nf_all, wu_all, wd_all, np_all,
                          wg_all, wp_all, layer=i, tm=512)

        outs["conv_p"].append(tails)
        outs["conv_s"].append(raw_s)
        outs["gdn_p"].append(gdn_p)
        outs["k_p"].append(k_p.reshape(batch, window, kvh, hd))
        outs["v_p"].append(v_p.reshape(batch, window, kvh, hd))

    st = lambda n: jnp.stack(outs[n])
    conv_p = st("conv_p")[:, :, SUBLANES - (width - 1):, :]
    conv_s = st("conv_s").reshape(depth, 3, dbatch, dseq, key_dim)[:, :, :, dseq - (width - 1):, :]
    conv_s = jnp.transpose(conv_s, (0, 2, 3, 1, 4)).reshape(depth, dbatch, width - 1, conv_dim)
    return (h_p.reshape(batch, seq, d), h_s.reshape(dbatch, dseq, d),
            conv_p, st("gdn_p"), st("k_p"), st("v_p"), conv_s, gdn_s,
            kv_s[0].reshape(depth, dbatch, window, kvh, hd), kv_s[1].reshape(depth, dbatch, window, kvh, hd))
```
